```python
import math
import jax
import jax.numpy as jnp
from jax import lax
import numpy as np

D_MODEL = 1024
BATCH = 8
SEQ = 4096
DEPTH = 2

HEAD_DIM = 64
RET_HEADS = 4
MOBA_HEADS = 8
GDN_HEADS = 4
RET_W = RET_HEADS * HEAD_DIM
MOBA_W = MOBA_HEADS * HEAD_DIM
GDN_W = GDN_HEADS * HEAD_DIM
D_MIX = RET_W + MOBA_W + GDN_W
IN_COLS = 4 * RET_W + 3 * MOBA_W + 4 * GDN_W + 2 * GDN_HEADS
RET_CHUNK = 64
ROPE_BASE = 10000.0
MOBA_BLOCK = 256
MOBA_TOPK = 3
MOBA_Q_CHUNK = 64
REL_BUCKETS = 32
REL_MAX_DIST = 128
GDN_CHUNK = 64
GDN_CONV = 4
MEM_LEN = 256
CROSS_HEADS = 4
CROSS_HEAD_DIM = 128
CROSS_W = CROSS_HEADS * CROSS_HEAD_DIM
D_FF = 2816
FFN_CONV = 3
EPS = 1e-6

kernel_name = 'hybrid_retention_moba_gdn_block'


def rms_norm(x, w):
    xf = x.astype(jnp.float32)
    y = xf * lax.rsqrt(jnp.mean(xf * xf, axis=-1, keepdims=True) + EPS)
    return (y * w.astype(jnp.float32)).astype(x.dtype)


def l2_norm(x):
    xf = x.astype(jnp.float32)
    return xf * lax.rsqrt(jnp.sum(xf * xf, axis=-1, keepdims=True) + EPS)


def causal_dwconv(x, w):
    k, c = w.shape
    return lax.conv_general_dilated(
        x, w[:, None, :].astype(x.dtype), window_strides=(1,), padding=[(k - 1, 0)],
        dimension_numbers=('NWC', 'WIO', 'NWC'), feature_group_count=c)


def to_heads(t, n_heads):
    b, s, _ = t.shape
    return t.reshape(b, s, n_heads, -1).transpose(0, 2, 1, 3)


def from_heads(t):
    b, h, s, d = t.shape
    return t.transpose(0, 2, 1, 3).reshape(b, s, h * d)


def rotary(x, pos):
    half = x.shape[-1] // 2
    inv_freq = ROPE_BASE ** (-jnp.arange(half, dtype=jnp.float32) / half)
    ang = pos[:, None] * inv_freq[None, :]
    cos, sin = jnp.cos(ang), jnp.sin(ang)
    xf = x.astype(jnp.float32)
    x1, x2 = xf[..., :half], xf[..., half:]
    return jnp.concatenate([x1 * cos - x2 * sin, x2 * cos + x1 * sin], axis=-1).astype(x.dtype)


def t5_bucket(rel):
    n = jnp.maximum(rel, 0)
    exact = REL_BUCKETS // 2
    nf = jnp.maximum(n, exact).astype(jnp.float32)
    large = exact + (jnp.log(nf / exact) / math.log(REL_MAX_DIST / exact)
                     * (REL_BUCKETS - exact)).astype(jnp.int32)
    return jnp.where(n < exact, n, jnp.minimum(large, REL_BUCKETS - 1))


def retention(q, k, v):
    b, h, s, dh = q.shape
    c = RET_CHUNK
    n = s // c
    log_gamma = jnp.log1p(-jnp.exp2(-5.0 - jnp.arange(h, dtype=jnp.float32)))
    idx = jnp.arange(c, dtype=jnp.float32)
    diff = idx[:, None] - idx[None, :]
    d_intra = jnp.where(diff >= 0, jnp.exp(log_gamma[:, None, None] * jnp.maximum(diff, 0.0)), 0.0)
    xi = jnp.exp(log_gamma[:, None] * (idx + 1.0))
    zeta = jnp.exp(log_gamma[:, None] * (c - 1.0 - idx))
    g_chunk = jnp.exp(log_gamma * c)
    qc = q.astype(jnp.float32).reshape(b, h, n, c, dh)
    kc = (k.astype(jnp.float32) * dh ** -0.5).reshape(b, h, n, c, dh)
    vc = v.astype(jnp.float32).reshape(b, h, n, c, dh)
    scores = jnp.einsum('bhncd,bhnjd->bhncj', qc, kc) * d_intra[None, :, None]
    inner = jnp.einsum('bhncj,bhnje->bhnce', scores, vc)
    kv_inc = jnp.einsum('bhncd,bhnce->bhnde', kc * zeta[None, :, None, :, None], vc)

    def step(state, inc):
        return g_chunk[None, :, None, None] * state + inc, state

    _, prev = lax.scan(step, jnp.zeros((b, h, dh, dh), jnp.float32), jnp.moveaxis(kv_inc, 2, 0))
    prev = jnp.moveaxis(prev, 0, 2)
    cross = jnp.einsum('bhncd,bhnde->bhnce', qc, prev) * xi[None, :, None, :, None]
    return (inner + cross).reshape(b, h, s, dh)


def moba_attention(q, k, v, rel_bias):
    b, h, s, dh = q.shape
    nb = -(-s // MOBA_BLOCK)
    s_pad = nb * MOBA_BLOCK
    pad = ((0, 0), (0, 0), (0, s_pad - s), (0, 0))
    q, k, v = jnp.pad(q, pad), jnp.pad(k, pad), jnp.pad(v, pad)
    kb = k.reshape(b, h, nb, MOBA_BLOCK, dh)
    vb = v.reshape(b, h, nb, MOBA_BLOCK, dh)
    k_mean = jnp.mean(kb.astype(jnp.float32), axis=3)
    topk = min(MOBA_TOPK, nb)
    scale = dh ** -0.5
    b_idx = jnp.arange(b)[:, None, None, None]
    h_idx = jnp.arange(h)[None, :, None, None]
    key_off = jnp.arange(MOBA_BLOCK, dtype=jnp.int32)
    blk_ids = jnp.arange(nb, dtype=jnp.int32)

    def query_chunk(ci):
        q0 = ci * MOBA_Q_CHUNK
        qc = lax.dynamic_slice_in_dim(q, q0, MOBA_Q_CHUNK, axis=2).astype(jnp.float32)
        q_pos = q0 + jnp.arange(MOBA_Q_CHUNK, dtype=jnp.int32)
        blk = q0 // MOBA_BLOCK
        gate = jnp.einsum('bhqd,bhnd->bhqn', qc, k_mean)
        gate = jnp.where(blk_ids < blk, gate, -jnp.inf)
        _, sel = lax.top_k(gate, topk)
        valid = sel < blk
        k_sel = kb[b_idx, h_idx, sel].astype(jnp.float32)
        v_sel = vb[b_idx, h_idx, sel].astype(jnp.float32)
        k_own = lax.dynamic_index_in_dim(kb, blk, axis=2, keepdims=False).astype(jnp.float32)
        v_own = lax.dynamic_index_in_dim(vb, blk, axis=2, keepdims=False).astype(jnp.float32)
        sel_pos = sel[..., None] * MOBA_BLOCK + key_off
        own_pos = blk * MOBA_BLOCK + key_off
        bias_sel = rel_bias[h_idx[..., None], t5_bucket(q_pos[None, None, :, None, None] - sel_pos)]
        bias_own = rel_bias[:, t5_bucket(q_pos[:, None] - own_pos[None, :])]
        s_sel = jnp.einsum('bhqd,bhqtkd->bhqtk', qc, k_sel) * scale + bias_sel
        s_sel = jnp.where(valid[..., None], s_sel, -jnp.inf)
        s_own = jnp.einsum('bhqd,bhkd->bhqk', qc, k_own) * scale + bias_own[None]
        s_own = jnp.where(own_pos[None, :] <= q_pos[:, None], s_own, -jnp.inf)
        logits = jnp.concatenate([s_sel.reshape(b, h, MOBA_Q_CHUNK, topk * MOBA_BLOCK), s_own], axis=-1)
        p = jax.nn.softmax(logits, axis=-1)
        p_sel = p[..., :topk * MOBA_BLOCK].reshape(b, h, MOBA_Q_CHUNK, topk, MOBA_BLOCK)
        p_own = p[..., topk * MOBA_BLOCK:]
        return (jnp.einsum('bhqtk,bhqtkd->bhqd', p_sel, v_sel)
                + jnp.einsum('bhqk,bhkd->bhqd', p_own, v_own))

    outs = lax.map(query_chunk, jnp.arange(s // MOBA_Q_CHUNK, dtype=jnp.int32))
    return jnp.moveaxis(outs, 0, 2).reshape(b, h, s, dh).astype(q.dtype)


def gated_delta_rule(q, k, v, log_decay, beta):
    b, h, s, dk = q.shape
    dv = v.shape[-1]
    c = GDN_CHUNK
    n = s // c
    qc = (q.astype(jnp.float32) * dk ** -0.5).reshape(b, h, n, c, dk)
    kc = k.astype(jnp.float32).reshape(b, h, n, c, dk)
    vc = v.astype(jnp.float32).reshape(b, h, n, c, dv)
    bc = beta.astype(jnp.float32).reshape(b, h, n, c)
    gcum = jnp.cumsum(log_decay.astype(jnp.float32).reshape(b, h, n, c), axis=-1)
    incl = jnp.tril(jnp.ones((c, c), dtype=bool))
    strict = jnp.tril(jnp.ones((c, c), dtype=bool), -1)
    decay = jnp.exp(jnp.where(incl, gcum[..., :, None] - gcum[..., None, :], -jnp.inf))
    kk = jnp.einsum('bhncd,bhnjd->bhncj', kc, kc)
    a = jnp.where(strict, bc[..., :, None] * kk * decay, 0.0) + jnp.eye(c, dtype=jnp.float32)
    rhs = jnp.concatenate([vc * bc[..., None], kc * (bc * jnp.exp(gcum))[..., None]], axis=-1)
    sol = lax.linalg.triangular_solve(a, rhs, left_side=True, lower=True, unit_diagonal=True)
    u, w = sol[..., :dv], sol[..., dv:]
    qk = jnp.einsum('bhncd,bhnjd->bhncj', qc, kc) * decay
    q_dec = qc * jnp.exp(gcum)[..., None]
    k_dec = kc * jnp.exp(gcum[..., -1:] - gcum)[..., None]
    g_last = jnp.exp(gcum[..., -1])

    def step(state, xs):
        u_n, w_n, qk_n, q_n, k_n, gl = xs
        v_new = u_n - jnp.einsum('bhcd,bhde->bhce', w_n, state)
        o = jnp.einsum('bhcd,bhde->bhce', q_n, state) + jnp.einsum('bhcj,bhje->bhce', qk_n, v_new)
        state = state * gl[..., None, None] + jnp.einsum('bhcd,bhce->bhde', k_n, v_new)
        return state, o

    xs = tuple(jnp.moveaxis(t, 2, 0) for t in (u, w, qk, q_dec, k_dec, g_last))
    _, o = lax.scan(step, jnp.zeros((b, h, dk, dv), jnp.float32), xs)
    return jnp.moveaxis(o, 0, 2).reshape(b, h, s, dv)


def hybrid_mixer(h, w_in, ret_norm, moba_q_norm, moba_k_norm, gdn_conv, gdn_a_log, gdn_dt_bias,
                 gdn_norm, w_out, rel_bias):
    s = h.shape[1]
    widths = [RET_W] * 4 + [MOBA_W] * 3 + [3 * GDN_W, GDN_W, GDN_HEADS, GDN_HEADS]
    cuts = [int(cc) for cc in np.cumsum(widths)[:-1]]
    rq, rk, rv, rg, mq, mk, mv, g_qkv, g_z, g_b, g_a = jnp.split(h @ w_in, cuts, axis=-1)

    pos = jnp.arange(s, dtype=jnp.float32)
    ret = retention(rotary(to_heads(rq, RET_HEADS), pos), rotary(to_heads(rk, RET_HEADS), pos),
                    to_heads(rv, RET_HEADS))
    y_ret = from_heads(rms_norm(ret, ret_norm)).astype(h.dtype) * jax.nn.silu(rg)

    y_moba = from_heads(moba_attention(rms_norm(to_heads(mq, MOBA_HEADS), moba_q_norm),
                                       rms_norm(to_heads(mk, MOBA_HEADS), moba_k_norm),
                                       to_heads(mv, MOBA_HEADS), rel_bias))

    gq, gk, gv = jnp.split(jax.nn.silu(causal_dwconv(g_qkv, gdn_conv)), 3, axis=-1)
    beta = jax.nn.sigmoid(g_b.astype(jnp.float32)).transpose(0, 2, 1)
    log_decay = (-jnp.exp(gdn_a_log.astype(jnp.float32))
                 * jax.nn.softplus(g_a.astype(jnp.float32) + gdn_dt_bias.astype(jnp.float32))).transpose(0, 2, 1)
    o = gated_delta_rule(l2_norm(to_heads(gq, GDN_HEADS)), l2_norm(to_heads(gk, GDN_HEADS)),
                         to_heads(gv, GDN_HEADS), log_decay, beta)
    y_gdn = from_heads(rms_norm(o, gdn_norm)).astype(h.dtype) * jax.nn.silu(g_z)

    return jnp.concatenate([y_ret, y_moba, y_gdn], axis=-1) @ w_out


def memory_cross_attention(h, mem_h, wq, wkv, q_norm, k_norm, wo):
    b, s, _ = h.shape
    m = mem_h.shape[1]
    q = rms_norm((h @ wq).reshape(b, s, CROSS_HEADS, CROSS_HEAD_DIM), q_norm)
    k, v = jnp.split(mem_h @ wkv, 2, axis=-1)
    k = rms_norm(k.reshape(b, m, CROSS_HEADS, CROSS_HEAD_DIM), k_norm)
    v = v.reshape(b, m, CROSS_HEADS, CROSS_HEAD_DIM)
    logits = jnp.einsum('bshd,bmhd->bhsm', q.astype(jnp.float32), k.astype(jnp.float32)) * CROSS_HEAD_DIM ** -0.5
    p = jax.nn.softmax(logits, axis=-1)
    o = jnp.einsum('bhsm,bmhd->bshd', p, v.astype(jnp.float32)).reshape(b, s, CROSS_W).astype(h.dtype)
    return o @ wo


def conv_ffn(h, w_up, conv_w, conv_b, w_down):
    u = causal_dwconv(h @ w_up, conv_w) + conv_b
    gate, val = jnp.split(u, 2, axis=-1)
    return (jax.nn.silu(gate) * val) @ w_down


def setup_inputs(seed: int = 0) -> dict:
    key = jax.random.key(seed)
    keys = jax.random.split(key, 32)
    L = DEPTH

    def normal(i, shape, scale):
        return jax.random.normal(keys[i], shape, jnp.float32) * scale

    def gain(i, dim):
        return 1.0 + normal(i, (L, dim), 0.02)

    dt = jnp.exp(jax.random.uniform(keys[8], (L, GDN_HEADS), jnp.float32, math.log(1e-3), math.log(1e-1)))
    return {
        'x': normal(0, (BATCH, SEQ, D_MODEL), 1.0),
        'mem': normal(1, (BATCH, MEM_LEN, D_MODEL), 1.0),
        'norm_mix': gain(2, D_MODEL),
        'w_in': normal(3, (L, D_MODEL, IN_COLS), D_MODEL ** -0.5),
        'ret_norm': gain(4, HEAD_DIM),
        'moba_q_norm': gain(5, HEAD_DIM),
        'moba_k_norm': gain(6, HEAD_DIM),
        'gdn_conv': normal(7, (L, GDN_CONV, 3 * GDN_W), GDN_CONV ** -0.5),
        'gdn_a_log': jnp.log(jax.random.uniform(keys[9], (L, GDN_HEADS), jnp.float32, 1.0, 16.0)),
        'gdn_dt_bias': dt + jnp.log(-jnp.expm1(-dt)),
        'gdn_norm': gain(10, HEAD_DIM),
        'w_out': normal(11, (L, D_MIX, D_MODEL), D_MIX ** -0.5),
        'norm_cross': gain(12, D_MODEL),
        'norm_mem': gain(13, D_MODEL),
        'cross_wq': normal(14, (L, D_MODEL, CROSS_W), D_MODEL ** -0.5),
        'cross_wkv': normal(15, (L, D_MODEL, 2 * CROSS_W), D_MODEL ** -0.5),
        'cross_q_norm': gain(16, CROSS_HEAD_DIM),
        'cross_k_norm': gain(17, CROSS_HEAD_DIM),
        'cross_wo': normal(18, (L, CROSS_W, D_MODEL), CROSS_W ** -0.5),
        'norm_ffn': gain(19, D_MODEL),
        'ffn_up': normal(20, (L, D_MODEL, 2 * D_FF), D_MODEL ** -0.5),
        'ffn_conv': normal(21, (L, FFN_CONV, 2 * D_FF), FFN_CONV ** -0.5),
        'ffn_conv_b': normal(22, (L, 2 * D_FF), 0.01),
        'ffn_down': normal(23, (L, D_FF, D_MODEL), D_FF ** -0.5),
        'rel_bias': normal(24, (MOBA_HEADS, REL_BUCKETS), 0.2),
    }


def reference(x, mem, norm_mix, w_in, ret_norm, moba_q_norm, moba_k_norm, gdn_conv, gdn_a_log,
              gdn_dt_bias, gdn_norm, w_out, norm_cross, norm_mem, cross_wq, cross_wkv, cross_q_norm,
              cross_k_norm, cross_wo, norm_ffn, ffn_up, ffn_conv, ffn_conv_b, ffn_down, rel_bias):
    for l in range(DEPTH):
        h = rms_norm(x, norm_mix[l])
        x = x + hybrid_mixer(h, w_in[l], ret_norm[l], moba_q_norm[l], moba_k_norm[l], gdn_conv[l],
                             gdn_a_log[l], gdn_dt_bias[l], gdn_norm[l], w_out[l], rel_bias)
        h = rms_norm(x, norm_cross[l])
        x = x + memory_cross_attention(h, rms_norm(mem, norm_mem[l]), cross_wq[l], cross_wkv[l],
                                       cross_q_norm[l], cross_k_norm[l], cross_wo[l])
        h = rms_norm(x, norm_ffn[l])
        x = x + conv_ffn(h, ffn_up[l], ffn_conv[l], ffn_conv_b[l], ffn_down[l])
    return x
```

```python
import functools
import math

import numpy as np
import jax
import jax.numpy as jnp
from jax import lax
from jax.experimental import pallas as pl
from jax.experimental.pallas import tpu as pltpu

F32 = jnp.float32
BF16 = jnp.bfloat16

HEAD_DIM = 64
RET_HEADS = 4
MOBA_HEADS = 8
GDN_HEADS = 4
RET_W = RET_HEADS * HEAD_DIM
MOBA_W = MOBA_HEADS * HEAD_DIM
GDN_W = GDN_HEADS * HEAD_DIM
ROPE_BASE = 10000.0
MOBA_BLOCK = 256
MOBA_TOPK = 3
REL_BUCKETS = 32
REL_MAX_DIST = 128
GDN_CHUNK = 64
GDN_CONV = 4
CROSS_HEADS = 4
CROSS_HEAD_DIM = 128
FFN_CONV = 3
EPS = 1e-6

LANES = 128
SUBLANES = 8
BF16_ROWS = 16
VMEM_LIMIT = 56 * 1024 * 1024


def _bf(a):
    return a.astype(BF16)


def _mm(a, b):
    return jnp.dot(_bf(a), _bf(b), preferred_element_type=F32)


def _mm_nt(a, b):
    return lax.dot_general(_bf(a), _bf(b), (((1,), (1,)), ((), ())), preferred_element_type=F32)


def _split(a):
    hi = a.astype(BF16)
    lo = (a - hi.astype(F32)).astype(BF16)
    return hi, lo


def _mm3(a, b):
    ah, al = _split(a)
    bh, bl = _split(b)
    d = functools.partial(jnp.dot, preferred_element_type=F32)
    return d(ah, bh) + (d(ah, bl) + d(al, bh))


def _mm3_nt(a, b):
    ah, al = _split(a)
    bh, bl = _split(b)
    d = functools.partial(lax.dot_general, dimension_numbers=(((1,), (1,)), ((), ())),
                          preferred_element_type=F32)
    return d(ah, bh) + (d(ah, bl) + d(al, bh))


def _mm2_l(a, b_exact):
    ah, al = _split(a)
    d = functools.partial(jnp.dot, preferred_element_type=F32)
    return d(ah, b_exact) + d(al, b_exact)


def _mm2_r(a_exact, b):
    bh, bl = _split(b)
    d = functools.partial(jnp.dot, preferred_element_type=F32)
    return d(a_exact, bh) + d(a_exact, bl)


def _sigmoid(x):
    return 1.0 / (1.0 + jnp.exp(-x))


def _silu(x):
    return x * _sigmoid(x)


def _softplus(x):
    return jnp.maximum(x, 0.0) + jnp.log1p(jnp.exp(-jnp.abs(x)))


def _rms_rows(x, w):
    return x * lax.rsqrt(jnp.mean(x * x, axis=-1, keepdims=True) + EPS) * w


def _shift_rows(u, prev, k):
    r = pltpu.roll(u, k, axis=0)
    rp = pltpu.roll(prev, k, axis=0)[:SUBLANES]
    row = lax.broadcasted_iota(jnp.int32, (SUBLANES, u.shape[1]), 0)
    top = jnp.where(row < k, rp, r[:SUBLANES])
    return jnp.concatenate([top, r[SUBLANES:]], axis=0)


def _const_spec(shape):
    nd = len(shape)
    return pl.BlockSpec(shape, lambda *_: (0,) * nd)


def _params(*sem):
    return pltpu.CompilerParams(dimension_semantics=sem, vmem_limit_bytes=VMEM_LIMIT)


def _inproj_kernel(x_ref, nw_ref, w_ref, ws_ref, wst_ref, ret_ref, moba_ref, gdn_ref, scol_ref, srow_ref):
    x = x_ref[...]
    hf = _rms_rows(x, nw_ref[...])
    h = _bf(hf)
    off = 0
    for ref in (ret_ref, moba_ref, gdn_ref):
        width = ref.shape[1]
        for c in range(0, width, 512):
            ref[:, c:c + 512] = jnp.dot(h, w_ref[:, off + c:off + c + 512],
                                        preferred_element_type=F32).astype(ref.dtype)
        off += width
    scol_ref[...] = _mm3(hf, ws_ref[...])
    srow_ref[...] = _mm3_nt(wst_ref[...], hf)


def _inproj(x, norm_w, w_in, tm):
    m, d = x.shape
    main = 4 * RET_W + 3 * MOBA_W + 4 * GDN_W
    w_main = _bf(w_in[:, :main])
    w_small = jnp.zeros((d, LANES), F32).at[:, :2 * GDN_HEADS].set(w_in[:, main:])
    w_small_t = jnp.zeros((BF16_ROWS, d), F32).at[:2 * GDN_HEADS, :].set(w_in[:, main:].T)
    out_shape = (
        jax.ShapeDtypeStruct((m, 4 * RET_W), BF16),
        jax.ShapeDtypeStruct((m, 3 * MOBA_W), BF16),
        jax.ShapeDtypeStruct((m, 4 * GDN_W), BF16),
        jax.ShapeDtypeStruct((m, LANES), F32),
        jax.ShapeDtypeStruct((BF16_ROWS, m), F32),
    )
    return pl.pallas_call(
        _inproj_kernel,
        out_shape=out_shape,
        grid=(m // tm,),
        in_specs=[
            pl.BlockSpec((tm, d), lambda i: (i, 0)),
            _const_spec((1, d)),
            _const_spec((d, main)),
            _const_spec((d, LANES)),
            _const_spec((BF16_ROWS, d)),
        ],
        out_specs=(
            pl.BlockSpec((tm, 4 * RET_W), lambda i: (i, 0)),
            pl.BlockSpec((tm, 3 * MOBA_W), lambda i: (i, 0)),
            pl.BlockSpec((tm, 4 * GDN_W), lambda i: (i, 0)),
            pl.BlockSpec((tm, LANES), lambda i: (i, 0)),
            pl.BlockSpec((BF16_ROWS, tm), lambda i: (0, i)),
        ),
        compiler_params=_params("parallel"),
        name="inproj",
    )(x, norm_w.reshape(1, d), w_main, w_small, w_small_t)


def _head_consts(n_heads):
    w = n_heads * HEAD_DIM
    head = np.arange(w) // HEAD_DIM
    same = (head[:, None] == head[None, :]).astype(np.float32)
    hmask = (head[None, None, :] == np.arange(n_heads)[:, None, None]).astype(np.float32)
    return same, hmask


def _ret_kernel(in_ref, cos_ref, sin_ref, gn_ref, rot_ref, eye_ref, dmat_ref, xi_ref, zeta_ref, gt_ref,
                hmask_ref, same_ref, out_ref, s_ref, *, n_heads):
    w = n_heads * HEAD_DIM
    t_len = in_ref.shape[0]

    @pl.when(pl.program_id(1) == 0)
    def _():
        s_ref[...] = jnp.zeros_like(s_ref)

    xin = in_ref[...]
    q = xin[:, :w]
    k = xin[:, w:2 * w]
    v = xin[:, 2 * w:3 * w]
    g = xin[:, 3 * w:].astype(F32)
    cos = cos_ref[...]
    sin = sin_ref[...]
    rot = rot_ref[...]
    qr = q.astype(F32) * cos + jnp.dot(q, rot, preferred_element_type=F32) * sin
    kr = k.astype(F32) * cos + jnp.dot(k, rot, preferred_element_type=F32) * sin
    same = same_ref[...]
    k_t = _mm_nt(eye_ref[...], kr)
    state = s_ref[...]
    o = _mm(qr, state) * xi_ref[...]
    vf = v.astype(F32)
    for h in range(n_heads):
        hm = hmask_ref[h]
        s = _mm_nt(qr * hm, kr) * dmat_ref[h]
        o = o + _mm(s, vf * hm)
    zr = jnp.concatenate(
        [jnp.broadcast_to(zeta_ref[h:h + 1, :], (HEAD_DIM, t_len)) for h in range(n_heads)], axis=0)
    s_ref[...] = state * gt_ref[...] + _mm(k_t * zr, v) * same
    ms = _mm2_l(o * o, _bf(same)) * (1.0 / HEAD_DIM)
    y = o * lax.rsqrt(ms + EPS) * gn_ref[...] * _silu(g)
    out_ref[...] = y.astype(out_ref.dtype)


def _retention(ret_in, ret_norm, batch, seq, t_len):
    n_heads = RET_HEADS
    w = n_heads * HEAD_DIM
    m = batch * seq
    nt = seq // t_len
    half = HEAD_DIM // 2
    pos = np.arange(seq, dtype=np.float32)
    inv_freq = (ROPE_BASE ** (-jnp.arange(half, dtype=F32) / half))
    ang = jnp.asarray(pos)[:, None] * inv_freq[None, :]
    cos = jnp.tile(jnp.concatenate([jnp.cos(ang)] * 2, axis=-1), (1, n_heads))
    sin = jnp.tile(jnp.concatenate([jnp.sin(ang)] * 2, axis=-1), (1, n_heads))
    j = np.arange(w)
    rot = np.zeros((w, w), np.float32)
    first = (j % HEAD_DIM) < half
    rot[j[first] + half, j[first]] = -1.0
    rot[j[~first] - half, j[~first]] = 1.0
    same, hmask = _head_consts(n_heads)
    log_gamma = jnp.log1p(-jnp.exp2(-5.0 - jnp.arange(n_heads, dtype=F32)))
    idx = jnp.arange(t_len, dtype=F32)
    diff = idx[:, None] - idx[None, :]
    scale = HEAD_DIM ** -0.5
    dmat = jnp.where(diff >= 0, jnp.exp(log_gamma[:, None, None] * jnp.maximum(diff, 0.0)), 0.0) * scale
    xi = jnp.repeat(jnp.exp(log_gamma[:, None] * (idx + 1.0)).T, HEAD_DIM, axis=1)
    zeta = jnp.exp(log_gamma[:, None] * (t_len - 1.0 - idx)) * scale
    zeta = jnp.zeros((SUBLANES, t_len), F32).at[:n_heads].set(zeta)
    g_chunk = jnp.repeat(jnp.exp(log_gamma * t_len), HEAD_DIM)[None, :]
    gn = jnp.tile(ret_norm.astype(F32), n_heads)[None, :]
    kern = functools.partial(_ret_kernel, n_heads=n_heads)
    return pl.pallas_call(
        kern,
        out_shape=jax.ShapeDtypeStruct((m, w), BF16),
        grid=(batch, nt),
        in_specs=[
            pl.BlockSpec((t_len, 4 * w), lambda b, t: (b * nt + t, 0)),
            pl.BlockSpec((t_len, w), lambda b, t: (t, 0)),
            pl.BlockSpec((t_len, w), lambda b, t: (t, 0)),
            _const_spec((1, w)),
            _const_spec((w, w)),
            _const_spec((w, w)),
            _const_spec((n_heads, t_len, t_len)),
            _const_spec((t_len, w)),
            _const_spec((SUBLANES, t_len)),
            _const_spec((1, w)),
            _const_spec((n_heads, 1, w)),
            _const_spec((w, w)),
        ],
        out_specs=pl.BlockSpec((t_len, w), lambda b, t: (b * nt + t, 0)),
        scratch_shapes=[pltpu.VMEM((w, w), F32)],
        compiler_params=_params("parallel", "arbitrary"),
        name="retention",
    )(ret_in, cos, sin, gn, jnp.asarray(rot, BF16), jnp.eye(w, dtype=BF16), dmat, xi, zeta, g_chunk,
      jnp.asarray(hmask), jnp.asarray(same))


def _t5_bucket(rel):
    n = jnp.maximum(rel, 0)
    exact = REL_BUCKETS // 2
    nf = jnp.maximum(n, exact).astype(F32)
    large = exact + (jnp.log(nf / exact) / math.log(REL_MAX_DIST / exact)
                     * (REL_BUCKETS - exact)).astype(jnp.int32)
    return jnp.where(n < exact, n, jnp.minimum(large, REL_BUCKETS - 1))


def _moba_kernel(q_ref, k_ref, v_ref, bown_ref, bprev_ref, bfar_ref, qg_ref, kg_ref, same_ref, out_ref,
                 kn_ref, kmean_ref, *, n_blocks):
    blk = MOBA_BLOCK
    qi = pl.program_id(2)
    same = same_ref[...]
    inv_d = 1.0 / HEAD_DIM
    scale = HEAD_DIM ** -0.5
    neg_inf = -jnp.inf

    @pl.when(qi == 0)
    def _():
        def body(j, c):
            r0 = pl.multiple_of(j * blk, blk)
            kb = k_ref[pl.ds(r0, blk), :].astype(F32)
            kn = kb * lax.rsqrt(_mm2_l(kb * kb, same) * inv_d + EPS) * kg_ref[...]
            kn_ref[pl.ds(r0, blk), :] = kn.astype(BF16)
            kmean_ref[pl.ds(j, 1), :] = jnp.mean(kn, axis=0, keepdims=True)
            return c
        lax.fori_loop(0, n_blocks, body, 0)

    qb = q_ref[...].astype(F32)
    qn = qb * lax.rsqrt(_mm2_l(qb * qb, same) * inv_d + EPS) * qg_ref[...]
    lane = lax.broadcasted_iota(jnp.int32, (1, 2 * HEAD_DIM), 1)
    km = kmean_ref[...]
    blk_id = lax.broadcasted_iota(jnp.int32, (1, n_blocks), 1)
    row = lax.broadcasted_iota(jnp.int32, (blk, blk), 0)
    col = lax.broadcasted_iota(jnp.int32, (blk, blk), 1)
    q0 = pl.multiple_of(qi * blk, blk)
    k_own = kn_ref[pl.ds(q0, blk), :]
    v_own = v_ref[pl.ds(q0, blk), :]
    outs = []
    for e in range(2):
        qm = jnp.where(lane >= HEAD_DIM, qn, 0.0) if e else jnp.where(lane < HEAD_DIM, qn, 0.0)
        gate = _mm3_nt(qm, km)
        gate = jnp.where(blk_id < qi, gate, neg_inf)
        rank = jnp.zeros_like(gate)
        for i in range(n_blocks):
            gi = gate[:, i:i + 1]
            tie = jnp.where(blk_id > i, 1.0, 0.0)
            rank = rank + jnp.where(gi > gate, 1.0, jnp.where(gi == gate, tie, 0.0))
        sel = jnp.where(blk_id < qi, jnp.where(rank < MOBA_TOPK, 1.0, 0.0), 0.0)

        s = _mm_nt(qm, k_own) * scale + bown_ref[e]
        s = jnp.where(col <= row, s, neg_inf)
        m0 = jnp.max(s, axis=-1, keepdims=True)
        p = jnp.exp(s - m0)
        l0 = jnp.sum(p, axis=-1, keepdims=True)
        acc0 = _mm(p, v_own)

        def body(j, carry, qm=qm, sel=sel, e=e):
            m_i, l_i, acc = carry
            r0 = pl.multiple_of(j * blk, blk)
            kj = kn_ref[pl.ds(r0, blk), :]
            vj = v_ref[pl.ds(r0, blk), :]
            bias = jnp.where(j == qi - 1, bprev_ref[e], bfar_ref[e])
            sj = _mm_nt(qm, kj) * scale + bias
            selc = jnp.sum(jnp.where(blk_id == j, sel, 0.0), axis=-1, keepdims=True)
            sj = jnp.where(selc > 0.0, sj, neg_inf)
            m_n = jnp.maximum(m_i, jnp.max(sj, axis=-1, keepdims=True))
            alpha = jnp.exp(m_i - m_n)
            pj = jnp.exp(sj - m_n)
            l_n = alpha * l_i + jnp.sum(pj, axis=-1, keepdims=True)
            return m_n, l_n, alpha * acc + _mm(pj, vj)

        _, l_f, acc_f = lax.fori_loop(0, qi, body, (m0, l0, acc0))
        outs.append(acc_f / l_f)
    out_ref[...] = jnp.where(lane < HEAD_DIM, outs[0], outs[1]).astype(out_ref.dtype)


def _moba(moba_in, q_norm, k_norm, rel_bias, batch, seq):
    m = batch * seq
    blk = MOBA_BLOCK
    nb = seq // blk
    pairs = MOBA_HEADS // 2
    pw = 2 * HEAD_DIM
    d = jnp.arange(blk, dtype=jnp.int32)[:, None] - jnp.arange(blk, dtype=jnp.int32)[None, :]
    bown = rel_bias[:, _t5_bucket(d)].astype(F32)
    bprev = rel_bias[:, _t5_bucket(d + blk)].astype(F32)
    far = rel_bias[:, _t5_bucket(jnp.asarray(blk + 1, jnp.int32))].astype(F32)
    bfar = jnp.broadcast_to(far[:, None, None], (MOBA_HEADS, 1, blk))
    same, _ = _head_consts(2)
    qg = jnp.tile(q_norm.astype(F32), 2)[None, :]
    kg = jnp.tile(k_norm.astype(F32), 2)[None, :]
    kern = functools.partial(_moba_kernel, n_blocks=nb)
    return pl.pallas_call(
        kern,
        out_shape=jax.ShapeDtypeStruct((m, MOBA_W), BF16),
        grid=(batch, pairs, nb),
        in_specs=[
            pl.BlockSpec((blk, pw), lambda b, p, i: (b * nb + i, p)),
            pl.BlockSpec((seq, pw), lambda b, p, i: (b, pairs + p)),
            pl.BlockSpec((seq, pw), lambda b, p, i: (b, 2 * pairs + p)),
            pl.BlockSpec((2, blk, blk), lambda b, p, i: (p, 0, 0)),
            pl.BlockSpec((2, blk, blk), lambda b, p, i: (p, 0, 0)),
            pl.BlockSpec((2, 1, blk), lambda b, p, i: (p, 0, 0)),
            _const_spec((1, pw)),
            _const_spec((1, pw)),
            _const_spec((pw, pw)),
        ],
        out_specs=pl.BlockSpec((blk, pw), lambda b, p, i: (b * nb + i, p)),
        scratch_shapes=[pltpu.VMEM((seq, pw), BF16), pltpu.VMEM((nb, pw), F32)],
        compiler_params=_params("parallel", "parallel", "arbitrary"),
        name="moba",
    )(moba_in, moba_in, moba_in, bown, bprev, bfar, qg, kg, jnp.asarray(same, BF16))


def _tri_inverse(n, eye):
    x = eye - n
    p = n
    steps = int(math.log2(GDN_CHUNK)) - 1
    for _ in range(steps):
        p = _mm3(p, p)
        x = x + _mm3(x, p)
    return x


def _gdn_kernel(in_ref, scol_ref, srow_ref, cw_ref, alog_b_ref, dtb_b_ref, alog_r_ref, dtb_r_ref, gn_ref,
                lblk_ref, ublk_ref, mstrict_ref, mincl_ref, hmask_ref, same_ref, eye_ref, sela_ref, selb_ref,
                out_ref, s_ref, tail_ref, *, n_heads):
    w = n_heads * HEAD_DIM
    t_len = in_ref.shape[0]
    c_len = GDN_CHUNK
    neg_inf = -jnp.inf

    @pl.when(pl.program_id(1) == 0)
    def _():
        s_ref[...] = jnp.zeros_like(s_ref)
        tail_ref[...] = jnp.zeros_like(tail_ref)

    xin = in_ref[...].astype(F32)
    raw = xin[:, :3 * w]
    z = xin[:, 3 * w:]
    prev = tail_ref[...]
    cw = cw_ref[...]
    acc = raw * cw[GDN_CONV - 1:GDN_CONV, :]
    for kk in range(1, GDN_CONV):
        acc = acc + _shift_rows(raw, prev, kk) * cw[GDN_CONV - 1 - kk:GDN_CONV - kk, :]
    tail_ref[...] = raw[t_len - SUBLANES:, :]
    y = _silu(acc)
    q = y[:, :w]
    k = y[:, w:2 * w]
    v = y[:, 2 * w:]
    same = same_ref[...]
    same_b = _bf(same)
    qn = q * lax.rsqrt(_mm2_l(q * q, same_b) + EPS) * (HEAD_DIM ** -0.5)
    kn = k * lax.rsqrt(_mm2_l(k * k, same_b) + EPS)

    sc = scol_ref[...]
    beta_b = _sigmoid(_mm2_l(sc, selb_ref[...]))
    ld_b = -jnp.exp(alog_b_ref[...]) * _softplus(_mm2_l(sc, sela_ref[...]) + dtb_b_ref[...])
    g_b = _mm2_r(lblk_ref[...], ld_b)
    sr = srow_ref[...]
    ld_r = -jnp.exp(alog_r_ref[...]) * _softplus(sr + dtb_r_ref[...])
    g_r = _mm2_l(ld_r, ublk_ref[...])
    rem_r = _mm2_l(ld_r, _bf(mstrict_ref[...]))

    eye = eye_ref[...]
    k_t = _mm_nt(eye, kn)
    rem_rows = jnp.concatenate(
        [jnp.broadcast_to(rem_r[n_heads + h:n_heads + h + 1, :], (HEAD_DIM, t_len)) for h in range(n_heads)],
        axis=0)
    kdec_t = k_t * jnp.exp(rem_rows)
    eg = jnp.exp(g_b)
    qdec = qn * eg
    wrhs = kn * beta_b * eg
    urhs = v * beta_b
    mstrict = mstrict_ref[...]
    mincl = mincl_ref[...]
    eye_f = eye.astype(F32)
    u = jnp.zeros((t_len, w), F32)
    wv = jnp.zeros((t_len, w), F32)
    qks = []
    for h in range(n_heads):
        hm = hmask_ref[h]
        gcol = g_b[:, h * HEAD_DIM:h * HEAD_DIM + 1]
        grow = g_r[n_heads + h:n_heads + h + 1, :]
        dec = jnp.exp(jnp.where(mincl > 0.0, gcol - grow, neg_inf))
        kk_h = _mm_nt(kn * hm, kn)
        bcol = beta_b[:, h * HEAD_DIM:h * HEAD_DIM + 1]
        n_h = jnp.where(mstrict > 0.0, bcol * kk_h * dec, 0.0)
        t_inv = _tri_inverse(n_h, eye_f)
        u = u + _mm3(t_inv, urhs * hm)
        wv = wv + _mm3(t_inv, wrhs * hm)
        qks.append(_mm_nt(qn * hm, kn) * dec)

    state = s_ref[...]
    vnews = []
    ointer = []
    for c in range(t_len // c_len):
        rc = slice(c * c_len, (c + 1) * c_len)
        vnew = u[rc] - _mm(wv[rc], state)
        vnews.append(vnew)
        ointer.append(_mm(qdec[rc], state))
        gl = eg[(c + 1) * c_len - 1:(c + 1) * c_len, :]
        state = state * gl + _mm(kdec_t[:, rc], vnew) * same
    s_ref[...] = state
    vn = jnp.concatenate(vnews, axis=0)
    o = jnp.concatenate(ointer, axis=0)
    for h in range(n_heads):
        o = o + _mm(qks[h], vn * hmask_ref[h])
    ms = _mm2_l(o * o, same_b) * (1.0 / HEAD_DIM)
    yo = o * lax.rsqrt(ms + EPS) * gn_ref[...] * _silu(z)
    out_ref[...] = yo.astype(out_ref.dtype)


def _gdn(gdn_in, scol, srow, conv_w, a_log, dt_bias, gdn_norm, batch, seq, t_len):
    n_heads = GDN_HEADS
    w = n_heads * HEAD_DIM
    m = batch * seq
    nt = seq // t_len
    chunk = np.arange(t_len) // GDN_CHUNK
    pos = np.arange(t_len)
    samec = chunk[:, None] == chunk[None, :]
    mincl = (samec & (pos[None, :] <= pos[:, None])).astype(np.float32)
    mstrict = (samec & (pos[None, :] < pos[:, None])).astype(np.float32)
    same, hmask = _head_consts(n_heads)
    head = np.arange(w) // HEAD_DIM
    selb = np.zeros((LANES, w), np.float32)
    sela = np.zeros((LANES, w), np.float32)
    selb[head, np.arange(w)] = 1.0
    sela[n_heads + head, np.arange(w)] = 1.0
    alog_b = jnp.repeat(a_log.astype(F32), HEAD_DIM)[None, :]
    dtb_b = jnp.repeat(dt_bias.astype(F32), HEAD_DIM)[None, :]
    alog_r = jnp.zeros((BF16_ROWS, 1), F32).at[n_heads:2 * n_heads, 0].set(a_log.astype(F32))
    dtb_r = jnp.zeros((BF16_ROWS, 1), F32).at[n_heads:2 * n_heads, 0].set(dt_bias.astype(F32))
    gn = jnp.tile(gdn_norm.astype(F32), n_heads)[None, :]
    kern = functools.partial(_gdn_kernel, n_heads=n_heads)
    return pl.pallas_call(
        kern,
        out_shape=jax.ShapeDtypeStruct((m, w), BF16),
        grid=(batch, nt),
        in_specs=[
            pl.BlockSpec((t_len, 4 * w), lambda b, t: (b * nt + t, 0)),
            pl.BlockSpec((t_len, LANES), lambda b, t: (b * nt + t, 0)),
            pl.BlockSpec((BF16_ROWS, t_len), lambda b, t: (0, b * nt + t)),
            _const_spec((GDN_CONV, 3 * w)),
            _const_spec((1, w)),
            _const_spec((1, w)),
            _const_spec((BF16_ROWS, 1)),
            _const_spec((BF16_ROWS, 1)),
            _const_spec((1, w)),
            _const_spec((t_len, t_len)),
            _const_spec((t_len, t_len)),
            _const_spec((t_len, t_len)),
            _const_spec((t_len, t_len)),
            _const_spec((n_heads, 1, w)),
            _const_spec((w, w)),
            _const_spec((w, w)),
            _const_spec((LANES, w)),
            _const_spec((LANES, w)),
        ],
        out_specs=pl.BlockSpec((t_len, w), lambda b, t: (b * nt + t, 0)),
        scratch_shapes=[pltpu.VMEM((w, w), F32), pltpu.VMEM((SUBLANES, 3 * w), F32)],
        compiler_params=_params("parallel", "arbitrary"),
        name="gdn",
    )(gdn_in, scol, srow, conv_w.astype(F32), alog_b, dtb_b, alog_r, dtb_r, gn,
      jnp.asarray(mincl, BF16), jnp.asarray(mincl.T, BF16), jnp.asarray(mstrict), jnp.asarray(mincl),
      jnp.asarray(hmask), jnp.asarray(same), jnp.eye(w, dtype=BF16), jnp.asarray(sela, BF16),
      jnp.asarray(selb, BF16))


def _outproj_kernel(yr_ref, ym_ref, yg_ref, wr_ref, wm_ref, wg_ref, x_ref, out_ref):
    d = functools.partial(jnp.dot, preferred_element_type=F32)
    out_ref[...] = x_ref[...] + (d(yr_ref[...], wr_ref[...]) + d(ym_ref[...], wm_ref[...])
                                 + d(yg_ref[...], wg_ref[...]))


def _outproj(y_ret, y_moba, y_gdn, w_out, x, tm):
    m, d = x.shape
    wb = _bf(w_out)
    wr = wb[:RET_W]
    wm = wb[RET_W:RET_W + MOBA_W]
    wg = wb[RET_W + MOBA_W:]
    return pl.pallas_call(
        _outproj_kernel,
        out_shape=jax.ShapeDtypeStruct((m, d), F32),
        grid=(m // tm,),
        in_specs=[
            pl.BlockSpec((tm, RET_W), lambda i: (i, 0)),
            pl.BlockSpec((tm, MOBA_W), lambda i: (i, 0)),
            pl.BlockSpec((tm, GDN_W), lambda i: (i, 0)),
            _const_spec((RET_W, d)),
            _const_spec((MOBA_W, d)),
            _const_spec((GDN_W, d)),
            pl.BlockSpec((tm, d), lambda i: (i, 0)),
        ],
        out_specs=pl.BlockSpec((tm, d), lambda i: (i, 0)),
        compiler_params=_params("parallel"),
        name="outproj",
    )(y_ret, y_moba, y_gdn, wr, wm, wg, x)


def _memkv_kernel(mem_ref, nw_ref, wkv_ref, kg_ref, k_ref, v_ref):
    cw = CROSS_HEADS * CROSS_HEAD_DIM
    h = _bf(_rms_rows(mem_ref[...], nw_ref[...]))
    kv = jnp.dot(h, wkv_ref[...], preferred_element_type=F32)
    for hd in range(CROSS_HEADS):
        sl = slice(hd * CROSS_HEAD_DIM, (hd + 1) * CROSS_HEAD_DIM)
        k_ref[:, sl] = _rms_rows(kv[:, sl], kg_ref[...]).astype(k_ref.dtype)
    v_ref[...] = kv[:, cw:].astype(v_ref.dtype)


def _memkv(mem, norm_mem, wkv, k_norm, tm):
    m, d = mem.shape
    cw = CROSS_HEADS * CROSS_HEAD_DIM
    return pl.pallas_call(
        _memkv_kernel,
        out_shape=(jax.ShapeDtypeStruct((m, cw), BF16), jax.ShapeDtypeStruct((m, cw), BF16)),
        grid=(m // tm,),
        in_specs=[
            pl.BlockSpec((tm, d), lambda i: (i, 0)),
            _const_spec((1, d)),
            _const_spec((d, 2 * cw)),
            _const_spec((1, CROSS_HEAD_DIM)),
        ],
        out_specs=(pl.BlockSpec((tm, cw), lambda i: (i, 0)), pl.BlockSpec((tm, cw), lambda i: (i, 0))),
        compiler_params=_params("parallel"),
        name="memkv",
    )(mem, norm_mem.reshape(1, d), _bf(wkv), k_norm.reshape(1, CROSS_HEAD_DIM).astype(F32))


def _cross_kernel(x_ref, nw_ref, wq_ref, qg_ref, k_ref, v_ref, wo_ref, out_ref):
    x = x_ref[...]
    h = _bf(_rms_rows(x, nw_ref[...]))
    q = jnp.dot(h, wq_ref[...], preferred_element_type=F32)
    scale = CROSS_HEAD_DIM ** -0.5
    outs = []
    for hd in range(CROSS_HEADS):
        sl = slice(hd * CROSS_HEAD_DIM, (hd + 1) * CROSS_HEAD_DIM)
        qh = _rms_rows(q[:, sl], qg_ref[...])
        s = _mm_nt(qh, k_ref[:, sl]) * scale
        p = jnp.exp(s - jnp.max(s, axis=-1, keepdims=True))
        l = jnp.sum(p, axis=-1, keepdims=True)
        outs.append(_mm(p, v_ref[:, sl]) / l)
    o = _bf(jnp.concatenate(outs, axis=-1))
    out_ref[...] = x + jnp.dot(o, wo_ref[...], preferred_element_type=F32)


def _cross(x, norm_w, wq, q_norm, kn, v, wo, batch, seq, mem_len, tm):
    m, d = x.shape
    cw = CROSS_HEADS * CROSS_HEAD_DIM
    nt = seq // tm
    return pl.pallas_call(
        _cross_kernel,
        out_shape=jax.ShapeDtypeStruct((m, d), F32),
        grid=(batch, nt),
        in_specs=[
            pl.BlockSpec((tm, d), lambda b, t: (b * nt + t, 0)),
            _const_spec((1, d)),
            _const_spec((d, cw)),
            _const_spec((1, CROSS_HEAD_DIM)),
            pl.BlockSpec((mem_len, cw), lambda b, t: (b, 0)),
            pl.BlockSpec((mem_len, cw), lambda b, t: (b, 0)),
            _const_spec((cw, d)),
        ],
        out_specs=pl.BlockSpec((tm, d), lambda b, t: (b * nt + t, 0)),
        compiler_params=_params("parallel", "parallel"),
        name="cross",
    )(x, norm_w.reshape(1, d), _bf(wq), q_norm.reshape(1, CROSS_HEAD_DIM).astype(F32), kn, v, _bf(wo))


def _ffn_kernel(x_ref, xp_ref, nw_ref, wg_ref, wv_ref, cg_ref, cv_ref, bg_ref, bv_ref, wd_ref, out_ref,
                *, tiles_per_seq, n_chunks):
    x = x_ref[...]
    nw = nw_ref[...]
    h = _bf(_rms_rows(x, nw))
    keep = jnp.where(pl.program_id(0) % tiles_per_seq == 0, 0.0, 1.0)
    hp = _bf(_rms_rows(xp_ref[...], nw))
    out_ref[...] = x

    def conv(u, up, cw, b):
        y = u * cw[FFN_CONV - 1:FFN_CONV, :] + b
        for kk in range(1, FFN_CONV):
            y = y + _shift_rows(u, up, kk) * cw[FFN_CONV - 1 - kk:FFN_CONV - kk, :]
        return y

    def body(c, carry):
        wg = wg_ref[c]
        wv = wv_ref[c]
        d = functools.partial(jnp.dot, preferred_element_type=F32)
        gate = conv(d(h, wg), d(hp, wg) * keep, cg_ref[c], bg_ref[c])
        val = conv(d(h, wv), d(hp, wv) * keep, cv_ref[c], bv_ref[c])
        act = _bf(_silu(gate) * val)
        out_ref[...] += d(act, wd_ref[c])
        return carry

    lax.fori_loop(0, n_chunks, body, 0)


def _ffn(x, norm_w, w_up, conv_w, conv_b, w_down, seq, tm, fc):
    m, d = x.shape
    d_ff = w_down.shape[0]
    nc = d_ff // fc
    halo = BF16_ROWS

    def chunks(a):
        return a.reshape(a.shape[0], nc, fc).transpose(1, 0, 2)

    wg = chunks(_bf(w_up[:, :d_ff]))
    wv = chunks(_bf(w_up[:, d_ff:]))
    cg = chunks(conv_w[:, :d_ff].astype(F32))
    cv = chunks(conv_w[:, d_ff:].astype(F32))
    bg = chunks(conv_b[None, :d_ff].astype(F32))
    bv = chunks(conv_b[None, d_ff:].astype(F32))
    wd = _bf(w_down).reshape(nc, fc, d)
    kern = functools.partial(_ffn_kernel, tiles_per_seq=seq // tm, n_chunks=nc)
    return pl.pallas_call(
        kern,
        out_shape=jax.ShapeDtypeStruct((m, d), F32),
        grid=(m // tm,),
        in_specs=[
            pl.BlockSpec((tm, d), lambda i: (i, 0)),
            pl.BlockSpec((halo, d), lambda i: (jnp.maximum(i * (tm // halo) - 1, 0), 0)),
            _const_spec((1, d)),
            _const_spec((nc, d, fc)),
            _const_spec((nc, d, fc)),
            _const_spec((nc, FFN_CONV, fc)),
            _const_spec((nc, FFN_CONV, fc)),
            _const_spec((nc, 1, fc)),
            _const_spec((nc, 1, fc)),
            _const_spec((nc, fc, d)),
        ],
        out_specs=pl.BlockSpec((tm, d), lambda i: (i, 0)),
        compiler_params=_params("parallel"),
        name="ffn",
    )(x, x, norm_w.reshape(1, d), wg, wv, cg, cv, bg, bv, wd)


def _tiles(seq):
    tm = min(512, seq)
    t_len = min(256, seq)
    return tm, t_len


def kernel(x, mem, norm_mix, w_in, ret_norm, moba_q_norm, moba_k_norm, gdn_conv, gdn_a_log, gdn_dt_bias,
           gdn_norm, w_out, norm_cross, norm_mem, cross_wq, cross_wkv, cross_q_norm, cross_k_norm, cross_wo,
           norm_ffn, ffn_up, ffn_conv, ffn_conv_b, ffn_down, rel_bias):
    batch, seq, d = x.shape
    mem_len = mem.shape[1]
    depth = w_in.shape[0]
    tm, t_len = _tiles(seq)
    xf = x.reshape(batch * seq, d)
    memf = mem.reshape(batch * mem_len, d)
    for l in range(depth):
        ret_in, moba_in, gdn_in, scol, srow = _inproj(xf, norm_mix[l], w_in[l], tm)
        y_ret = _retention(ret_in, ret_norm[l], batch, seq, t_len)
        y_moba = _moba(moba_in, moba_q_norm[l], moba_k_norm[l], rel_bias, batch, seq)
        y_gdn = _gdn(gdn_in, scol, srow, gdn_conv[l], gdn_a_log[l], gdn_dt_bias[l], gdn_norm[l],
                     batch, seq, t_len)
        xf = _outproj(y_ret, y_moba, y_gdn, w_out[l], xf, tm)
        kn, v = _memkv(memf, norm_mem[l], cross_wkv[l], cross_k_norm[l], mem_len)
        xf = _cross(xf, norm_cross[l], cross_wq[l], cross_q_norm[l], kn, v, cross_wo[l],
                    batch, seq, mem_len, tm)
        xf = _ffn(xf, norm_ffn[l], ffn_up[l], ffn_conv[l], ffn_conv_b[l], ffn_down[l], seq, tm, 256)
    return xf.reshape(batch, seq, d)
```

```python
import functools
import math

import numpy as np
import jax
import jax.numpy as jnp
from jax import lax
from jax.experimental import pallas as pl
from jax.experimental.pallas import tpu as pltpu

F32 = jnp.float32
BF16 = jnp.bfloat16

HEAD_DIM = 64
RET_HEADS = 4
MOBA_HEADS = 8
GDN_HEADS = 4
RET_W = RET_HEADS * HEAD_DIM
MOBA_W = MOBA_HEADS * HEAD_DIM
GDN_W = GDN_HEADS * HEAD_DIM
ROPE_BASE = 10000.0
MOBA_BLOCK = 256
MOBA_TOPK = 3
REL_BUCKETS = 32
REL_MAX_DIST = 128
GDN_CHUNK = 64
GDN_CONV = 4
CROSS_HEADS = 4
CROSS_HEAD_DIM = 128
FFN_CONV = 3
EPS = 1e-6

LANES = 128
SUBLANES = 8
BF16_ROWS = 16
VMEM_LIMIT = 56 * 1024 * 1024


def _bf(a):
    return a.astype(BF16)


def _mm(a, b):
    return jnp.dot(_bf(a), _bf(b), preferred_element_type=F32)


def _mm_nt(a, b):
    return lax.dot_general(_bf(a), _bf(b), (((1,), (1,)), ((), ())), preferred_element_type=F32)


def _split(a):
    hi = a.astype(BF16)
    lo = (a - hi.astype(F32)).astype(BF16)
    return hi, lo


def _mm3(a, b):
    ah, al = _split(a)
    bh, bl = _split(b)
    d = functools.partial(jnp.dot, preferred_element_type=F32)
    return d(ah, bh) + (d(ah, bl) + d(al, bh))


def _mm3_nt(a, b):
    ah, al = _split(a)
    bh, bl = _split(b)
    d = functools.partial(lax.dot_general, dimension_numbers=(((1,), (1,)), ((), ())),
                          preferred_element_type=F32)
    return d(ah, bh) + (d(ah, bl) + d(al, bh))


def _mm2_l(a, b_exact):
    ah, al = _split(a)
    d = functools.partial(jnp.dot, preferred_element_type=F32)
    return d(ah, b_exact) + d(al, b_exact)


def _mm2_r(a_exact, b):
    bh, bl = _split(b)
    d = functools.partial(jnp.dot, preferred_element_type=F32)
    return d(a_exact, bh) + d(a_exact, bl)


def _sigmoid(x):
    return 1.0 / (1.0 + jnp.exp(-x))


def _silu(x):
    return x * _sigmoid(x)


def _softplus(x):
    return jnp.maximum(x, 0.0) + jnp.log1p(jnp.exp(-jnp.abs(x)))


def _rms_rows(x, w):
    return x * lax.rsqrt(jnp.mean(x * x, axis=-1, keepdims=True) + EPS) * w


def _shift_rows(u, prev, k):
    r = pltpu.roll(u, k, axis=0)
    rp = pltpu.roll(prev, k, axis=0)[:SUBLANES]
    row = lax.broadcasted_iota(jnp.int32, (SUBLANES, u.shape[1]), 0)
    top = jnp.where(row < k, rp, r[:SUBLANES])
    return jnp.concatenate([top, r[SUBLANES:]], axis=0)


def _const_spec(shape):
    nd = len(shape)
    return pl.BlockSpec(shape, lambda *_: (0,) * nd)


def _params(*sem):
    return pltpu.CompilerParams(dimension_semantics=sem, vmem_limit_bytes=VMEM_LIMIT)


def _inproj_kernel(x_ref, nw_ref, w_ref, ws_ref, wst_ref, ret_ref, moba_ref, gdn_ref, scol_ref, srow_ref):
    x = x_ref[...]
    hf = _rms_rows(x, nw_ref[...])
    h = _bf(hf)
    off = 0
    for ref in (ret_ref, moba_ref, gdn_ref):
        width = ref.shape[1]
        for c in range(0, width, 512):
            ref[:, c:c + 512] = jnp.dot(h, w_ref[:, off + c:off + c + 512],
                                        preferred_element_type=F32).astype(ref.dtype)
        off += width
    scol_ref[...] = _mm3(hf, ws_ref[...])
    srow_ref[...] = _mm3_nt(wst_ref[...], hf)


def _inproj(x, norm_w, w_in, tm):
    m, d = x.shape
    main = 4 * RET_W + 3 * MOBA_W + 4 * GDN_W
    w_main = _bf(w_in[:, :main])
    w_small = jnp.zeros((d, LANES), F32).at[:, :2 * GDN_HEADS].set(w_in[:, main:])
    w_small_t = jnp.zeros((BF16_ROWS, d), F32).at[:2 * GDN_HEADS, :].set(w_in[:, main:].T)
    out_shape = (
        jax.ShapeDtypeStruct((m, 4 * RET_W), BF16),
        jax.ShapeDtypeStruct((m, 3 * MOBA_W), BF16),
        jax.ShapeDtypeStruct((m, 4 * GDN_W), BF16),
        jax.ShapeDtypeStruct((m, LANES), F32),
        jax.ShapeDtypeStruct((BF16_ROWS, m), F32),
    )
    return pl.pallas_call(
        _inproj_kernel,
        out_shape=out_shape,
        grid=(m // tm,),
        in_specs=[
            pl.BlockSpec((tm, d), lambda i: (i, 0)),
            _const_spec((1, d)),
            _const_spec((d, main)),
            _const_spec((d, LANES)),
            _const_spec((BF16_ROWS, d)),
        ],
        out_specs=(
            pl.BlockSpec((tm, 4 * RET_W), lambda i: (i, 0)),
            pl.BlockSpec((tm, 3 * MOBA_W), lambda i: (i, 0)),
            pl.BlockSpec((tm, 4 * GDN_W), lambda i: (i, 0)),
            pl.BlockSpec((tm, LANES), lambda i: (i, 0)),
            pl.BlockSpec((BF16_ROWS, tm), lambda i: (0, i)),
        ),
        compiler_params=_params("parallel"),
        name="inproj",
    )(x, norm_w.reshape(1, d), w_main, w_small, w_small_t)


def _head_consts(n_heads):
    w = n_heads * HEAD_DIM
    head = np.arange(w) // HEAD_DIM
    same = (head[:, None] == head[None, :]).astype(np.float32)
    hmask = (head[None, None, :] == np.arange(n_heads)[:, None, None]).astype(np.float32)
    return same, hmask


def _ret_kernel(in_ref, cos_ref, sin_ref, gn_ref, rot_ref, eye_ref, dmat_ref, xi_ref, zeta_ref, gt_ref,
                hmask_ref, same_ref, out_ref, s_ref, *, n_heads):
    w = n_heads * HEAD_DIM
    t_len = in_ref.shape[0]

    @pl.when(pl.program_id(1) == 0)
    def _():
        s_ref[...] = jnp.zeros_like(s_ref)

    xin = in_ref[...]
    q = xin[:, :w]
    k = xin[:, w:2 * w]
    v = xin[:, 2 * w:3 * w]
    g = xin[:, 3 * w:].astype(F32)
    cos = cos_ref[...]
    sin = sin_ref[...]
    rot = rot_ref[...]
    qr = q.astype(F32) * cos + jnp.dot(q, rot, preferred_element_type=F32) * sin
    kr = k.astype(F32) * cos + jnp.dot(k, rot, preferred_element_type=F32) * sin
    same = same_ref[...]
    k_t = _mm_nt(eye_ref[...], kr)
    state = s_ref[...]
    o = _mm(qr, state) * xi_ref[...]
    vf = v.astype(F32)
    for h in range(n_heads):
        hm = hmask_ref[h]
        s = _mm_nt(qr * hm, kr) * dmat_ref[h]
        o = o + _mm(s, vf * hm)
    zr = jnp.concatenate(
        [jnp.broadcast_to(zeta_ref[h:h + 1, :], (HEAD_DIM, t_len)) for h in range(n_heads)], axis=0)
    s_ref[...] = state * gt_ref[...] + _mm(k_t * zr, v) * same
    ms = _mm2_l(o * o, _bf(same)) * (1.0 / HEAD_DIM)
    y = o * lax.rsqrt(ms + EPS) * gn_ref[...] * _silu(g)
    out_ref[...] = y.astype(out_ref.dtype)


def _retention(ret_in, ret_norm, batch, seq, t_len):
    n_heads = RET_HEADS
    w = n_heads * HEAD_DIM
    m = batch * seq
    nt = seq // t_len
    half = HEAD_DIM // 2
    pos = np.arange(seq, dtype=np.float32)
    inv_freq = (ROPE_BASE ** (-jnp.arange(half, dtype=F32) / half))
    ang = jnp.asarray(pos)[:, None] * inv_freq[None, :]
    cos = jnp.tile(jnp.concatenate([jnp.cos(ang)] * 2, axis=-1), (1, n_heads))
    sin = jnp.tile(jnp.concatenate([jnp.sin(ang)] * 2, axis=-1), (1, n_heads))
    j = np.arange(w)
    rot = np.zeros((w, w), np.float32)
    first = (j % HEAD_DIM) < half
    rot[j[first] + half, j[first]] = -1.0
    rot[j[~first] - half, j[~first]] = 1.0
    same, hmask = _head_consts(n_heads)
    log_gamma = jnp.log1p(-jnp.exp2(-5.0 - jnp.arange(n_heads, dtype=F32)))
    idx = jnp.arange(t_len, dtype=F32)
    diff = idx[:, None] - idx[None, :]
    scale = HEAD_DIM ** -0.5
    dmat = jnp.where(diff >= 0, jnp.exp(log_gamma[:, None, None] * jnp.maximum(diff, 0.0)), 0.0) * scale
    xi = jnp.repeat(jnp.exp(log_gamma[:, None] * (idx + 1.0)).T, HEAD_DIM, axis=1)
    zeta = jnp.exp(log_gamma[:, None] * (t_len - 1.0 - idx)) * scale
    zeta = jnp.zeros((SUBLANES, t_len), F32).at[:n_heads].set(zeta)
    g_chunk = jnp.repeat(jnp.exp(log_gamma * t_len), HEAD_DIM)[None, :]
    gn = jnp.tile(ret_norm.astype(F32), n_heads)[None, :]
    kern = functools.partial(_ret_kernel, n_heads=n_heads)
    return pl.pallas_call(
        kern,
        out_shape=jax.ShapeDtypeStruct((m, w), BF16),
        grid=(batch, nt),
        in_specs=[
            pl.BlockSpec((t_len, 4 * w), lambda b, t: (b * nt + t, 0)),
            pl.BlockSpec((t_len, w), lambda b, t: (t, 0)),
            pl.BlockSpec((t_len, w), lambda b, t: (t, 0)),
            _const_spec((1, w)),
            _const_spec((w, w)),
            _const_spec((w, w)),
            _const_spec((n_heads, t_len, t_len)),
            _const_spec((t_len, w)),
            _const_spec((SUBLANES, t_len)),
            _const_spec((1, w)),
            _const_spec((n_heads, 1, w)),
            _const_spec((w, w)),
        ],
        out_specs=pl.BlockSpec((t_len, w), lambda b, t: (b * nt + t, 0)),
        scratch_shapes=[pltpu.VMEM((w, w), F32)],
        compiler_params=_params("parallel", "arbitrary"),
        name="retention",
    )(ret_in, cos, sin, gn, jnp.asarray(rot, BF16), jnp.eye(w, dtype=BF16), dmat, xi, zeta, g_chunk,
      jnp.asarray(hmask), jnp.asarray(same))


def _t5_bucket(rel):
    n = jnp.maximum(rel, 0)
    exact = REL_BUCKETS // 2
    nf = jnp.maximum(n, exact).astype(F32)
    large = exact + (jnp.log(nf / exact) / math.log(REL_MAX_DIST / exact)
                     * (REL_BUCKETS - exact)).astype(jnp.int32)
    return jnp.where(n < exact, n, jnp.minimum(large, REL_BUCKETS - 1))


MOBA_MASK = -1e30
MOBA_EXP_LIMIT = 1e37


def _moba_kernel(q_ref, k_ref, v_ref, bown_ref, bprev_ref, bfar_ref, qg_ref, kg_ref, same_ref, eye_ref,
                 out_ref, kaug_ref, vaug_ref, kmean_ref, *, n_blocks):
    blk = MOBA_BLOCK
    hd = HEAD_DIM
    qi = pl.program_id(2)
    same = same_ref[...]
    inv_d = 1.0 / hd
    scale = hd ** -0.5
    lane = lax.broadcasted_iota(jnp.int32, (1, 2 * hd), 1)
    in_h0 = lane < hd

    @pl.when(qi == 0)
    def _():
        def body(j, c):
            r0 = pl.multiple_of(j * blk, blk)
            kb = k_ref[pl.ds(r0, blk), :].astype(F32)
            kn = kb * lax.rsqrt(_mm2_l(kb * kb, same) * inv_d + EPS) * kg_ref[...]
            kmean_ref[pl.ds(j, 1), :] = jnp.mean(kn, axis=0, keepdims=True)
            vb = v_ref[pl.ds(r0, blk), :].astype(F32)
            for e in range(2):
                xo = (1 - e) * hd
                ones = (lane >= xo + n_blocks) & (lane < xo + n_blocks + 3)
                ext = jnp.where((lane == xo + j) | ones, 1.0, 0.0)
                mine = in_h0 if e == 0 else jnp.logical_not(in_h0)
                kaug_ref[e, pl.ds(r0, blk), :] = jnp.where(mine, kn, ext).astype(BF16)
                vaug_ref[e, pl.ds(r0, blk), :] = jnp.where(mine, vb, 1.0).astype(BF16)
            return c
        lax.fori_loop(0, n_blocks, body, 0)

    qb = q_ref[...].astype(F32)
    qn = qb * lax.rsqrt(_mm2_l(qb * qb, same) * inv_d + EPS) * qg_ref[...]
    km = kmean_ref[...]
    blk_row = lax.broadcasted_iota(jnp.int32, (n_blocks, blk), 0)
    row = lax.broadcasted_iota(jnp.int32, (blk, blk), 0)
    col = lax.broadcasted_iota(jnp.int32, (blk, blk), 1)
    eye = eye_ref[...]
    q0 = pl.multiple_of(qi * blk, blk)
    qp = pl.multiple_of(jnp.maximum(qi - 1, 0) * blk, blk)

    qbase, bown, bprev = [], [], []
    for e in range(2):
        mine = in_h0 if e == 0 else jnp.logical_not(in_h0)
        gate = _mm3_nt(jnp.where(mine, km, 0.0), qn)
        gate = jnp.where(blk_row < qi, gate, -jnp.inf)
        rank = jnp.zeros_like(gate)
        for i in range(n_blocks):
            gi = gate[i:i + 1, :]
            tie = jnp.where(blk_row > i, 1.0, 0.0)
            rank = rank + jnp.where(gi > gate, 1.0, jnp.where(gi == gate, tie, 0.0))
        chosen = ((blk_row < qi) & (rank < MOBA_TOPK)) | (blk_row == qi)
        mask_t = jnp.where(chosen, 0.0, MOBA_MASK)
        far = bfar_ref[e]
        far_hi = far.astype(BF16).astype(F32)
        crow = jnp.concatenate([far_hi, far - far_hi, jnp.zeros((SUBLANES - 2, blk), F32)], axis=0)
        pad = jnp.zeros((hd - n_blocks - SUBLANES, blk), F32)
        zq = jnp.zeros((hd, blk), F32)
        parts = [zq, mask_t, crow, pad] if e == 0 else [mask_t, crow, pad, zq]
        extra = _mm_nt(eye, jnp.concatenate(parts, axis=0))
        qbase.append(jnp.where(mine, qn * scale, 0.0) + extra)
        bown.append(bown_ref[e] - far)
        bprev.append(bprev_ref[e] - far)

    def scores(qa, e, r0, bias=None):
        s = _mm_nt(qa, kaug_ref[e, pl.ds(r0, blk), :])
        return s if bias is None else s + bias

    def finish(accs):
        o = [acc / pltpu.roll(acc, hd, axis=1) for acc in accs]
        return jnp.where(in_h0, o[0], o[1])

    def own_scores(e):
        s = jnp.where(col <= row, scores(_bf(qbase[e]), e, q0, bown[e]), -jnp.inf)
        return s, jnp.max(s, axis=-1, keepdims=True)

    q_shift, acc_fast = [], []
    for e in range(2):
        s, m0 = own_scores(e)
        shift = m0.astype(BF16).astype(F32)
        acc_fast.append(_mm(jnp.exp(s - shift), vaug_ref[e, pl.ds(q0, blk), :]))
        c_lane = (1 - e) * hd + n_blocks + 2
        q_shift.append(_bf(qbase[e] + jnp.where(lane == c_lane, -shift, 0.0)))

    def fast_tile(accs, r0, biases=(None, None)):
        return tuple(acc + _mm(jnp.exp(scores(q_shift[e], e, r0, biases[e])), vaug_ref[e, pl.ds(r0, blk), :])
                     for e, acc in enumerate(accs))

    accs = lax.cond(qi >= 1, lambda a: fast_tile(a, qp, bprev), lambda a: a, tuple(acc_fast))
    n_far = jnp.maximum(qi - 1, 0)

    def far_pair(t, a):
        a = fast_tile(a, pl.multiple_of(2 * t * blk, blk))
        return fast_tile(a, pl.multiple_of((2 * t + 1) * blk, blk))

    accs = lax.fori_loop(0, n_far // 2, far_pair, accs)
    last = pl.multiple_of(jnp.maximum(n_far - 1, 0) * blk, blk)
    accs = lax.cond(n_far % 2 == 1, lambda a: fast_tile(a, last), lambda a: a, accs)

    unsafe = sum(jnp.sum(jnp.where(jnp.abs(acc) < MOBA_EXP_LIMIT, 0.0, 1.0)) for acc in accs)
    overflowed = unsafe > 0.0

    def safe_path(_):
        def update(carry, s, vaug):
            m_i, acc = carry
            m_n = jnp.maximum(m_i, jnp.max(s, axis=-1, keepdims=True))
            return m_n, jnp.exp(m_i - m_n) * acc + _mm(jnp.exp(s - m_n), vaug)

        def block(j, c):
            r0 = pl.multiple_of(j * blk, blk)
            out = []
            for e in range(2):
                s = scores(_bf(qbase[e]), e, r0) + jnp.where(j == qi - 1, bprev[e], 0.0)
                out.append(update(c[e], s, vaug_ref[e, pl.ds(r0, blk), :]))
            return tuple(out)

        init = []
        for e in range(2):
            s, m0 = own_scores(e)
            init.append((m0, _mm(jnp.exp(s - m0), vaug_ref[e, pl.ds(q0, blk), :])))
        c = lax.fori_loop(0, qi, block, tuple(init))
        return finish([c[0][1], c[1][1]])

    out = lax.cond(overflowed, safe_path, lambda _: finish(accs), 0)
    out_ref[...] = out.astype(out_ref.dtype)


def _moba_bias_tables(rel_bias):
    blk = MOBA_BLOCK
    d = jnp.arange(blk, dtype=jnp.int32)[:, None] - jnp.arange(blk, dtype=jnp.int32)[None, :]
    bown = rel_bias[:, _t5_bucket(d)].astype(F32)
    bprev = rel_bias[:, _t5_bucket(d + blk)].astype(F32)
    far = rel_bias[:, _t5_bucket(jnp.asarray(blk + 1, jnp.int32))].astype(F32)
    bfar = jnp.broadcast_to(far[:, None, None], (MOBA_HEADS, 1, blk))
    return bown, bprev, bfar


def _moba(moba_in, q_norm, k_norm, bias_tabs, batch, seq):
    m = batch * seq
    blk = MOBA_BLOCK
    nb = seq // blk
    assert nb % SUBLANES == 0 and nb + SUBLANES <= HEAD_DIM
    pairs = MOBA_HEADS // 2
    pw = 2 * HEAD_DIM
    bown, bprev, bfar = bias_tabs
    same, _ = _head_consts(2)
    qg = jnp.tile(q_norm.astype(F32), 2)[None, :]
    kg = jnp.tile(k_norm.astype(F32), 2)[None, :]
    kern = functools.partial(_moba_kernel, n_blocks=nb)
    return pl.pallas_call(
        kern,
        out_shape=jax.ShapeDtypeStruct((m, MOBA_W), BF16),
        grid=(batch, pairs, nb),
        in_specs=[
            pl.BlockSpec((blk, pw), lambda b, p, i: (b * nb + i, p)),
            pl.BlockSpec((seq, pw), lambda b, p, i: (b, pairs + p)),
            pl.BlockSpec((seq, pw), lambda b, p, i: (b, 2 * pairs + p)),
            pl.BlockSpec((2, blk, blk), lambda b, p, i: (p, 0, 0)),
            pl.BlockSpec((2, blk, blk), lambda b, p, i: (p, 0, 0)),
            pl.BlockSpec((2, 1, blk), lambda b, p, i: (p, 0, 0)),
            _const_spec((1, pw)),
            _const_spec((1, pw)),
            _const_spec((pw, pw)),
            _const_spec((blk, blk)),
        ],
        out_specs=pl.BlockSpec((blk, pw), lambda b, p, i: (b * nb + i, p)),
        scratch_shapes=[pltpu.VMEM((2, seq, pw), BF16), pltpu.VMEM((2, seq, pw), BF16),
                        pltpu.VMEM((nb, pw), F32)],
        compiler_params=_params("parallel", "parallel", "arbitrary"),
        name="moba",
    )(moba_in, moba_in, moba_in, bown, bprev, bfar, qg, kg, jnp.asarray(same, BF16),
      jnp.eye(blk, dtype=BF16))


def _tri_inverse(n, eye):
    x = eye - n
    p = n
    steps = int(math.log2(GDN_CHUNK)) - 1
    for _ in range(steps):
        p = _mm3(p, p)
        x = x + _mm3(x, p)
    return x


def _gdn_kernel(in_ref, scol_ref, srow_ref, cw_ref, alog_b_ref, dtb_b_ref, alog_r_ref, dtb_r_ref, gn_ref,
                lblk_ref, ublk_ref, mstrict_ref, mincl_ref, hmask_ref, same_ref, eye_ref, sela_ref, selb_ref,
                out_ref, s_ref, tail_ref, *, n_heads):
    w = n_heads * HEAD_DIM
    t_len = in_ref.shape[0]
    c_len = GDN_CHUNK
    neg_inf = -jnp.inf

    @pl.when(pl.program_id(1) == 0)
    def _():
        s_ref[...] = jnp.zeros_like(s_ref)
        tail_ref[...] = jnp.zeros_like(tail_ref)

    xin = in_ref[...].astype(F32)
    raw = xin[:, :3 * w]
    z = xin[:, 3 * w:]
    prev = tail_ref[...]
    cw = cw_ref[...]
    acc = raw * cw[GDN_CONV - 1:GDN_CONV, :]
    for kk in range(1, GDN_CONV):
        acc = acc + _shift_rows(raw, prev, kk) * cw[GDN_CONV - 1 - kk:GDN_CONV - kk, :]
    tail_ref[...] = raw[t_len - SUBLANES:, :]
    y = _silu(acc)
    q = y[:, :w]
    k = y[:, w:2 * w]
    v = y[:, 2 * w:]
    same = same_ref[...]
    same_b = _bf(same)
    qn = q * lax.rsqrt(_mm2_l(q * q, same_b) + EPS) * (HEAD_DIM ** -0.5)
    kn = k * lax.rsqrt(_mm2_l(k * k, same_b) + EPS)

    sc = scol_ref[...]
    beta_b = _sigmoid(_mm2_l(sc, selb_ref[...]))
    ld_b = -jnp.exp(alog_b_ref[...]) * _softplus(_mm2_l(sc, sela_ref[...]) + dtb_b_ref[...])
    g_b = _mm2_r(lblk_ref[...], ld_b)
    sr = srow_ref[...]
    ld_r = -jnp.exp(alog_r_ref[...]) * _softplus(sr + dtb_r_ref[...])
    g_r = _mm2_l(ld_r, ublk_ref[...])
    rem_r = _mm2_l(ld_r, _bf(mstrict_ref[...]))

    eye = eye_ref[...]
    k_t = _mm_nt(eye, kn)
    rem_rows = jnp.concatenate(
        [jnp.broadcast_to(rem_r[n_heads + h:n_heads + h + 1, :], (HEAD_DIM, t_len)) for h in range(n_heads)],
        axis=0)
    kdec_t = k_t * jnp.exp(rem_rows)
    eg = jnp.exp(g_b)
    qdec = qn * eg
    wrhs = kn * beta_b * eg
    urhs = v * beta_b
    mstrict = mstrict_ref[...]
    mincl = mincl_ref[...]
    eye_f = eye.astype(F32)
    u = jnp.zeros((t_len, w), F32)
    wv = jnp.zeros((t_len, w), F32)
    qks = []
    for h in range(n_heads):
        hm = hmask_ref[h]
        gcol = g_b[:, h * HEAD_DIM:h * HEAD_DIM + 1]
        grow = g_r[n_heads + h:n_heads + h + 1, :]
        dec = jnp.exp(jnp.where(mincl > 0.0, gcol - grow, neg_inf))
        kk_h = _mm_nt(kn * hm, kn)
        bcol = beta_b[:, h * HEAD_DIM:h * HEAD_DIM + 1]
        n_h = jnp.where(mstrict > 0.0, bcol * kk_h * dec, 0.0)
        t_inv = _tri_inverse(n_h, eye_f)
        u = u + _mm3(t_inv, urhs * hm)
        wv = wv + _mm3(t_inv, wrhs * hm)
        qks.append(_mm_nt(qn * hm, kn) * dec)

    state = s_ref[...]
    vnews = []
    ointer = []
    for c in range(t_len // c_len):
        rc = slice(c * c_len, (c + 1) * c_len)
        vnew = u[rc] - _mm(wv[rc], state)
        vnews.append(vnew)
        ointer.append(_mm(qdec[rc], state))
        gl = eg[(c + 1) * c_len - 1:(c + 1) * c_len, :]
        state = state * gl + _mm(kdec_t[:, rc], vnew) * same
    s_ref[...] = state
    vn = jnp.concatenate(vnews, axis=0)
    o = jnp.concatenate(ointer, axis=0)
    for h in range(n_heads):
        o = o + _mm(qks[h], vn * hmask_ref[h])
    ms = _mm2_l(o * o, same_b) * (1.0 / HEAD_DIM)
    yo = o * lax.rsqrt(ms + EPS) * gn_ref[...] * _silu(z)
    out_ref[...] = yo.astype(out_ref.dtype)


def _gdn(gdn_in, scol, srow, conv_w, a_log, dt_bias, gdn_norm, batch, seq, t_len):
    n_heads = GDN_HEADS
    w = n_heads * HEAD_DIM
    m = batch * seq
    nt = seq // t_len
    chunk = np.arange(t_len) // GDN_CHUNK
    pos = np.arange(t_len)
    samec = chunk[:, None] == chunk[None, :]
    mincl = (samec & (pos[None, :] <= pos[:, None])).astype(np.float32)
    mstrict = (samec & (pos[None, :] < pos[:, None])).astype(np.float32)
    same, hmask = _head_consts(n_heads)
    head = np.arange(w) // HEAD_DIM
    selb = np.zeros((LANES, w), np.float32)
    sela = np.zeros((LANES, w), np.float32)
    selb[head, np.arange(w)] = 1.0
    sela[n_heads + head, np.arange(w)] = 1.0
    alog_b = jnp.repeat(a_log.astype(F32), HEAD_DIM)[None, :]
    dtb_b = jnp.repeat(dt_bias.astype(F32), HEAD_DIM)[None, :]
    alog_r = jnp.zeros((BF16_ROWS, 1), F32).at[n_heads:2 * n_heads, 0].set(a_log.astype(F32))
    dtb_r = jnp.zeros((BF16_ROWS, 1), F32).at[n_heads:2 * n_heads, 0].set(dt_bias.astype(F32))
    gn = jnp.tile(gdn_norm.astype(F32), n_heads)[None, :]
    kern = functools.partial(_gdn_kernel, n_heads=n_heads)
    return pl.pallas_call(
        kern,
        out_shape=jax.ShapeDtypeStruct((m, w), BF16),
        grid=(batch, nt),
        in_specs=[
            pl.BlockSpec((t_len, 4 * w), lambda b, t: (b * nt + t, 0)),
            pl.BlockSpec((t_len, LANES), lambda b, t: (b * nt + t, 0)),
            pl.BlockSpec((BF16_ROWS, t_len), lambda b, t: (0, b * nt + t)),
            _const_spec((GDN_CONV, 3 * w)),
            _const_spec((1, w)),
            _const_spec((1, w)),
            _const_spec((BF16_ROWS, 1)),
            _const_spec((BF16_ROWS, 1)),
            _const_spec((1, w)),
            _const_spec((t_len, t_len)),
            _const_spec((t_len, t_len)),
            _const_spec((t_len, t_len)),
            _const_spec((t_len, t_len)),
            _const_spec((n_heads, 1, w)),
            _const_spec((w, w)),
            _const_spec((w, w)),
            _const_spec((LANES, w)),
            _const_spec((LANES, w)),
        ],
        out_specs=pl.BlockSpec((t_len, w), lambda b, t: (b * nt + t, 0)),
        scratch_shapes=[pltpu.VMEM((w, w), F32), pltpu.VMEM((SUBLANES, 3 * w), F32)],
        compiler_params=_params("parallel", "arbitrary"),
        name="gdn",
    )(gdn_in, scol, srow, conv_w.astype(F32), alog_b, dtb_b, alog_r, dtb_r, gn,
      jnp.asarray(mincl, BF16), jnp.asarray(mincl.T, BF16), jnp.asarray(mstrict), jnp.asarray(mincl),
      jnp.asarray(hmask), jnp.asarray(same), jnp.eye(w, dtype=BF16), jnp.asarray(sela, BF16),
      jnp.asarray(selb, BF16))


def _outproj_kernel(yr_ref, ym_ref, yg_ref, wr_ref, wm_ref, wg_ref, x_ref, out_ref):
    d = functools.partial(jnp.dot, preferred_element_type=F32)
    out_ref[...] = x_ref[...] + (d(yr_ref[...], wr_ref[...]) + d(ym_ref[...], wm_ref[...])
                                 + d(yg_ref[...], wg_ref[...]))


def _outproj(y_ret, y_moba, y_gdn, w_out, x, tm):
    m, d = x.shape
    wb = _bf(w_out)
    wr = wb[:RET_W]
    wm = wb[RET_W:RET_W + MOBA_W]
    wg = wb[RET_W + MOBA_W:]
    return pl.pallas_call(
        _outproj_kernel,
        out_shape=jax.ShapeDtypeStruct((m, d), F32),
        grid=(m // tm,),
        in_specs=[
            pl.BlockSpec((tm, RET_W), lambda i: (i, 0)),
            pl.BlockSpec((tm, MOBA_W), lambda i: (i, 0)),
            pl.BlockSpec((tm, GDN_W), lambda i: (i, 0)),
            _const_spec((RET_W, d)),
            _const_spec((MOBA_W, d)),
            _const_spec((GDN_W, d)),
            pl.BlockSpec((tm, d), lambda i: (i, 0)),
        ],
        out_specs=pl.BlockSpec((tm, d), lambda i: (i, 0)),
        compiler_params=_params("parallel"),
        name="outproj",
    )(y_ret, y_moba, y_gdn, wr, wm, wg, x)


def _memkv_kernel(mem_ref, nw_ref, wkv_ref, kg_ref, k_ref, v_ref):
    cw = CROSS_HEADS * CROSS_HEAD_DIM
    h = _bf(_rms_rows(mem_ref[...], nw_ref[...]))
    kv = jnp.dot(h, wkv_ref[...], preferred_element_type=F32)
    for hd in range(CROSS_HEADS):
        sl = slice(hd * CROSS_HEAD_DIM, (hd + 1) * CROSS_HEAD_DIM)
        k_ref[:, sl] = _rms_rows(kv[:, sl], kg_ref[...]).astype(k_ref.dtype)
    v_ref[...] = kv[:, cw:].astype(v_ref.dtype)


def _memkv(mem, norm_mem, wkv, k_norm, tm):
    m, d = mem.shape
    cw = CROSS_HEADS * CROSS_HEAD_DIM
    return pl.pallas_call(
        _memkv_kernel,
        out_shape=(jax.ShapeDtypeStruct((m, cw), BF16), jax.ShapeDtypeStruct((m, cw), BF16)),
        grid=(m // tm,),
        in_specs=[
            pl.BlockSpec((tm, d), lambda i: (i, 0)),
            _const_spec((1, d)),
            _const_spec((d, 2 * cw)),
            _const_spec((1, CROSS_HEAD_DIM)),
        ],
        out_specs=(pl.BlockSpec((tm, cw), lambda i: (i, 0)), pl.BlockSpec((tm, cw), lambda i: (i, 0))),
        compiler_params=_params("parallel"),
        name="memkv",
    )(mem, norm_mem.reshape(1, d), _bf(wkv), k_norm.reshape(1, CROSS_HEAD_DIM).astype(F32))


def _cross_kernel(x_ref, nw_ref, wq_ref, qg_ref, k_ref, v_ref, wo_ref, out_ref):
    x = x_ref[...]
    h = _bf(_rms_rows(x, nw_ref[...]))
    q = jnp.dot(h, wq_ref[...], preferred_element_type=F32)
    scale = CROSS_HEAD_DIM ** -0.5
    outs = []
    for hd in range(CROSS_HEADS):
        sl = slice(hd * CROSS_HEAD_DIM, (hd + 1) * CROSS_HEAD_DIM)
        qh = _rms_rows(q[:, sl], qg_ref[...])
        s = _mm_nt(qh, k_ref[:, sl]) * scale
        p = jnp.exp(s - jnp.max(s, axis=-1, keepdims=True))
        l = jnp.sum(p, axis=-1, keepdims=True)
        outs.append(_mm(p, v_ref[:, sl]) / l)
    o = _bf(jnp.concatenate(outs, axis=-1))
    out_ref[...] = x + jnp.dot(o, wo_ref[...], preferred_element_type=F32)


def _cross(x, norm_w, wq, q_norm, kn, v, wo, batch, seq, mem_len, tm):
    m, d = x.shape
    cw = CROSS_HEADS * CROSS_HEAD_DIM
    nt = seq // tm
    return pl.pallas_call(
        _cross_kernel,
        out_shape=jax.ShapeDtypeStruct((m, d), F32),
        grid=(batch, nt),
        in_specs=[
            pl.BlockSpec((tm, d), lambda b, t: (b * nt + t, 0)),
            _const_spec((1, d)),
            _const_spec((d, cw)),
            _const_spec((1, CROSS_HEAD_DIM)),
            pl.BlockSpec((mem_len, cw), lambda b, t: (b, 0)),
            pl.BlockSpec((mem_len, cw), lambda b, t: (b, 0)),
            _const_spec((cw, d)),
        ],
        out_specs=pl.BlockSpec((tm, d), lambda b, t: (b * nt + t, 0)),
        compiler_params=_params("parallel", "parallel"),
        name="cross",
    )(x, norm_w.reshape(1, d), _bf(wq), q_norm.reshape(1, CROSS_HEAD_DIM).astype(F32), kn, v, _bf(wo))


def _ffn_kernel(x_ref, xp_ref, nw_ref, wg_ref, wv_ref, cg_ref, cv_ref, bg_ref, bv_ref, wd_ref, out_ref,
                *, tiles_per_seq, n_chunks):
    x = x_ref[...]
    nw = nw_ref[...]
    h = _bf(_rms_rows(x, nw))
    keep = jnp.where(pl.program_id(0) % tiles_per_seq == 0, 0.0, 1.0)
    hp = _bf(_rms_rows(xp_ref[...], nw))
    out_ref[...] = x

    def conv(u, up, cw, b):
        y = u * cw[FFN_CONV - 1:FFN_CONV, :] + b
        for kk in range(1, FFN_CONV):
            y = y + _shift_rows(u, up, kk) * cw[FFN_CONV - 1 - kk:FFN_CONV - kk, :]
        return y

    def body(c, carry):
        wg = wg_ref[c]
        wv = wv_ref[c]
        d = functools.partial(jnp.dot, preferred_element_type=F32)
        gate = conv(d(h, wg), d(hp, wg) * keep, cg_ref[c], bg_ref[c])
        val = conv(d(h, wv), d(hp, wv) * keep, cv_ref[c], bv_ref[c])
        act = _bf(_silu(gate) * val)
        out_ref[...] += d(act, wd_ref[c])
        return carry

    lax.fori_loop(0, n_chunks, body, 0)


def _ffn(x, norm_w, w_up, conv_w, conv_b, w_down, seq, tm, fc):
    m, d = x.shape
    d_ff = w_down.shape[0]
    nc = d_ff // fc
    halo = BF16_ROWS

    def chunks(a):
        return a.reshape(a.shape[0], nc, fc).transpose(1, 0, 2)

    wg = chunks(_bf(w_up[:, :d_ff]))
    wv = chunks(_bf(w_up[:, d_ff:]))
    cg = chunks(conv_w[:, :d_ff].astype(F32))
    cv = chunks(conv_w[:, d_ff:].astype(F32))
    bg = chunks(conv_b[None, :d_ff].astype(F32))
    bv = chunks(conv_b[None, d_ff:].astype(F32))
    wd = _bf(w_down).reshape(nc, fc, d)
    kern = functools.partial(_ffn_kernel, tiles_per_seq=seq // tm, n_chunks=nc)
    return pl.pallas_call(
        kern,
        out_shape=jax.ShapeDtypeStruct((m, d), F32),
        grid=(m // tm,),
        in_specs=[
            pl.BlockSpec((tm, d), lambda i: (i, 0)),
            pl.BlockSpec((halo, d), lambda i: (jnp.maximum(i * (tm // halo) - 1, 0), 0)),
            _const_spec((1, d)),
            _const_spec((nc, d, fc)),
            _const_spec((nc, d, fc)),
            _const_spec((nc, FFN_CONV, fc)),
            _const_spec((nc, FFN_CONV, fc)),
            _const_spec((nc, 1, fc)),
            _const_spec((nc, 1, fc)),
            _const_spec((nc, fc, d)),
        ],
        out_specs=pl.BlockSpec((tm, d), lambda i: (i, 0)),
        compiler_params=_params("parallel"),
        name="ffn",
    )(x, x, norm_w.reshape(1, d), wg, wv, cg, cv, bg, bv, wd)


def _tiles(seq):
    tm = min(512, seq)
    t_len = min(256, seq)
    return tm, t_len


def kernel(x, mem, norm_mix, w_in, ret_norm, moba_q_norm, moba_k_norm, gdn_conv, gdn_a_log, gdn_dt_bias,
           gdn_norm, w_out, norm_cross, norm_mem, cross_wq, cross_wkv, cross_q_norm, cross_k_norm, cross_wo,
           norm_ffn, ffn_up, ffn_conv, ffn_conv_b, ffn_down, rel_bias):
    batch, seq, d = x.shape
    mem_len = mem.shape[1]
    depth = w_in.shape[0]
    tm, t_len = _tiles(seq)
    xf = x.reshape(batch * seq, d)
    memf = mem.reshape(batch * mem_len, d)
    bias_tabs = _moba_bias_tables(rel_bias)
    for l in range(depth):
        ret_in, moba_in, gdn_in, scol, srow = _inproj(xf, norm_mix[l], w_in[l], tm)
        y_ret = _retention(ret_in, ret_norm[l], batch, seq, t_len)
        y_moba = _moba(moba_in, moba_q_norm[l], moba_k_norm[l], bias_tabs, batch, seq)
        y_gdn = _gdn(gdn_in, scol, srow, gdn_conv[l], gdn_a_log[l], gdn_dt_bias[l], gdn_norm[l],
                     batch, seq, t_len)
        xf = _outproj(y_ret, y_moba, y_gdn, w_out[l], xf, tm)
        kn, v = _memkv(memf, norm_mem[l], cross_wkv[l], cross_k_norm[l], mem_len)
        xf = _cross(xf, norm_cross[l], cross_wq[l], cross_q_norm[l], kn, v, cross_wo[l],
                    batch, seq, mem_len, tm)
        xf = _ffn(xf, norm_ffn[l], ffn_up[l], ffn_conv[l], ffn_conv_b[l], ffn_down[l], seq, tm, 256)
    return xf.reshape(batch, seq, d)
```

```python
import functools
import math

import numpy as np
import jax
import jax.numpy as jnp
from jax import lax
from jax.experimental import pallas as pl
from jax.experimental.pallas import tpu as pltpu

F32 = jnp.float32
BF16 = jnp.bfloat16

HEAD_DIM = 64
RET_HEADS = 4
MOBA_HEADS = 8
GDN_HEADS = 4
RET_W = RET_HEADS * HEAD_DIM
MOBA_W = MOBA_HEADS * HEAD_DIM
GDN_W = GDN_HEADS * HEAD_DIM
ROPE_BASE = 10000.0
MOBA_BLOCK = 256
MOBA_TOPK = 3
REL_BUCKETS = 32
REL_MAX_DIST = 128
GDN_CHUNK = 64
GDN_CONV = 4
CROSS_HEADS = 4
CROSS_HEAD_DIM = 128
FFN_CONV = 3
EPS = 1e-6

LANES = 128
SUBLANES = 8
BF16_ROWS = 16
VMEM_LIMIT = 56 * 1024 * 1024


def _bf(a):
    return a.astype(BF16)


def _mm(a, b):
    return jnp.dot(_bf(a), _bf(b), preferred_element_type=F32)


def _mm_nt(a, b):
    return lax.dot_general(_bf(a), _bf(b), (((1,), (1,)), ((), ())), preferred_element_type=F32)


def _split(a):
    hi = a.astype(BF16)
    lo = (a - hi.astype(F32)).astype(BF16)
    return hi, lo


def _mm3(a, b):
    ah, al = _split(a)
    bh, bl = _split(b)
    d = functools.partial(jnp.dot, preferred_element_type=F32)
    return d(ah, bh) + (d(ah, bl) + d(al, bh))


def _mm3_nt(a, b):
    ah, al = _split(a)
    bh, bl = _split(b)
    d = functools.partial(lax.dot_general, dimension_numbers=(((1,), (1,)), ((), ())),
                          preferred_element_type=F32)
    return d(ah, bh) + (d(ah, bl) + d(al, bh))


def _mm2_l(a, b_exact):
    ah, al = _split(a)
    d = functools.partial(jnp.dot, preferred_element_type=F32)
    return d(ah, b_exact) + d(al, b_exact)


def _mm2_r(a_exact, b):
    bh, bl = _split(b)
    d = functools.partial(jnp.dot, preferred_element_type=F32)
    return d(a_exact, bh) + d(a_exact, bl)


def _sigmoid(x):
    return 1.0 / (1.0 + jnp.exp(-x))


def _silu(x):
    return x * _sigmoid(x)


def _softplus(x):
    return jnp.maximum(x, 0.0) + jnp.log1p(jnp.exp(-jnp.abs(x)))


def _rms_rows(x, w):
    return x * lax.rsqrt(jnp.mean(x * x, axis=-1, keepdims=True) + EPS) * w


def _shift_rows(u, prev, k):
    r = pltpu.roll(u, k, axis=0)
    rp = pltpu.roll(prev, k, axis=0)[:SUBLANES]
    row = lax.broadcasted_iota(jnp.int32, (SUBLANES, u.shape[1]), 0)
    top = jnp.where(row < k, rp, r[:SUBLANES])
    return jnp.concatenate([top, r[SUBLANES:]], axis=0)


def _const_spec(shape):
    nd = len(shape)
    return pl.BlockSpec(shape, lambda *_: (0,) * nd)


def _params(*sem):
    return pltpu.CompilerParams(dimension_semantics=sem, vmem_limit_bytes=VMEM_LIMIT)


def _inproj_kernel(x_ref, nw_ref, w_ref, ws_ref, wst_ref, ret_ref, moba_ref, gdn_ref, scol_ref, srow_ref):
    x = x_ref[...]
    hf = _rms_rows(x, nw_ref[...])
    h = _bf(hf)
    off = 0
    for ref in (ret_ref, moba_ref, gdn_ref):
        width = ref.shape[1]
        for c in range(0, width, 512):
            ref[:, c:c + 512] = jnp.dot(h, w_ref[:, off + c:off + c + 512],
                                        preferred_element_type=F32).astype(ref.dtype)
        off += width
    scol_ref[...] = _mm3(hf, ws_ref[...])
    srow_ref[...] = _mm3_nt(wst_ref[...], hf)


def _inproj(x, norm_w, w_in, tm):
    m, d = x.shape
    main = 4 * RET_W + 3 * MOBA_W + 4 * GDN_W
    w_main = _bf(w_in[:, :main])
    w_small = jnp.zeros((d, LANES), F32).at[:, :2 * GDN_HEADS].set(w_in[:, main:])
    w_small_t = jnp.zeros((BF16_ROWS, d), F32).at[:2 * GDN_HEADS, :].set(w_in[:, main:].T)
    out_shape = (
        jax.ShapeDtypeStruct((m, 4 * RET_W), BF16),
        jax.ShapeDtypeStruct((m, 3 * MOBA_W), BF16),
        jax.ShapeDtypeStruct((m, 4 * GDN_W), BF16),
        jax.ShapeDtypeStruct((m, LANES), F32),
        jax.ShapeDtypeStruct((BF16_ROWS, m), F32),
    )
    return pl.pallas_call(
        _inproj_kernel,
        out_shape=out_shape,
        grid=(m // tm,),
        in_specs=[
            pl.BlockSpec((tm, d), lambda i: (i, 0)),
            _const_spec((1, d)),
            _const_spec((d, main)),
            _const_spec((d, LANES)),
            _const_spec((BF16_ROWS, d)),
        ],
        out_specs=(
            pl.BlockSpec((tm, 4 * RET_W), lambda i: (i, 0)),
            pl.BlockSpec((tm, 3 * MOBA_W), lambda i: (i, 0)),
            pl.BlockSpec((tm, 4 * GDN_W), lambda i: (i, 0)),
            pl.BlockSpec((tm, LANES), lambda i: (i, 0)),
            pl.BlockSpec((BF16_ROWS, tm), lambda i: (0, i)),
        ),
        compiler_params=_params("parallel"),
        name="inproj",
    )(x, norm_w.reshape(1, d), w_main, w_small, w_small_t)


def _head_consts(n_heads):
    w = n_heads * HEAD_DIM
    head = np.arange(w) // HEAD_DIM
    same = (head[:, None] == head[None, :]).astype(np.float32)
    hmask = (head[None, None, :] == np.arange(n_heads)[:, None, None]).astype(np.float32)
    return same, hmask


def _ret_kernel(in_ref, cos_ref, sin_ref, gn_ref, rot_ref, eye_ref, dmat_ref, xi_ref, zeta_ref, gt_ref,
                hmask_ref, same_ref, out_ref, s_ref, *, n_heads):
    w = n_heads * HEAD_DIM
    t_len = in_ref.shape[0]

    @pl.when(pl.program_id(1) == 0)
    def _():
        s_ref[...] = jnp.zeros_like(s_ref)

    xin = in_ref[...]
    q = xin[:, :w]
    k = xin[:, w:2 * w]
    v = xin[:, 2 * w:3 * w]
    g = xin[:, 3 * w:].astype(F32)
    cos = cos_ref[...]
    sin = sin_ref[...]
    rot = rot_ref[...]
    qr = q.astype(F32) * cos + jnp.dot(q, rot, preferred_element_type=F32) * sin
    kr = k.astype(F32) * cos + jnp.dot(k, rot, preferred_element_type=F32) * sin
    same = same_ref[...]
    k_t = _mm_nt(eye_ref[...], kr)
    state = s_ref[...]
    o = _mm(qr, state) * xi_ref[...]
    vf = v.astype(F32)
    for h in range(n_heads):
        hm = hmask_ref[h]
        s = _mm_nt(qr * hm, kr) * dmat_ref[h]
        o = o + _mm(s, vf * hm)
    zr = jnp.concatenate(
        [jnp.broadcast_to(zeta_ref[h:h + 1, :], (HEAD_DIM, t_len)) for h in range(n_heads)], axis=0)
    s_ref[...] = state * gt_ref[...] + _mm(k_t * zr, v) * same
    ms = _mm2_l(o * o, _bf(same)) * (1.0 / HEAD_DIM)
    y = o * lax.rsqrt(ms + EPS) * gn_ref[...] * _silu(g)
    out_ref[...] = y.astype(out_ref.dtype)


def _retention(ret_in, ret_norm, batch, seq, t_len):
    n_heads = RET_HEADS
    w = n_heads * HEAD_DIM
    m = batch * seq
    nt = seq // t_len
    half = HEAD_DIM // 2
    pos = np.arange(seq, dtype=np.float32)
    inv_freq = (ROPE_BASE ** (-jnp.arange(half, dtype=F32) / half))
    ang = jnp.asarray(pos)[:, None] * inv_freq[None, :]
    cos = jnp.tile(jnp.concatenate([jnp.cos(ang)] * 2, axis=-1), (1, n_heads))
    sin = jnp.tile(jnp.concatenate([jnp.sin(ang)] * 2, axis=-1), (1, n_heads))
    j = np.arange(w)
    rot = np.zeros((w, w), np.float32)
    first = (j % HEAD_DIM) < half
    rot[j[first] + half, j[first]] = -1.0
    rot[j[~first] - half, j[~first]] = 1.0
    same, hmask = _head_consts(n_heads)
    log_gamma = jnp.log1p(-jnp.exp2(-5.0 - jnp.arange(n_heads, dtype=F32)))
    idx = jnp.arange(t_len, dtype=F32)
    diff = idx[:, None] - idx[None, :]
    scale = HEAD_DIM ** -0.5
    dmat = jnp.where(diff >= 0, jnp.exp(log_gamma[:, None, None] * jnp.maximum(diff, 0.0)), 0.0) * scale
    xi = jnp.repeat(jnp.exp(log_gamma[:, None] * (idx + 1.0)).T, HEAD_DIM, axis=1)
    zeta = jnp.exp(log_gamma[:, None] * (t_len - 1.0 - idx)) * scale
    zeta = jnp.zeros((SUBLANES, t_len), F32).at[:n_heads].set(zeta)
    g_chunk = jnp.repeat(jnp.exp(log_gamma * t_len), HEAD_DIM)[None, :]
    gn = jnp.tile(ret_norm.astype(F32), n_heads)[None, :]
    kern = functools.partial(_ret_kernel, n_heads=n_heads)
    return pl.pallas_call(
        kern,
        out_shape=jax.ShapeDtypeStruct((m, w), BF16),
        grid=(batch, nt),
        in_specs=[
            pl.BlockSpec((t_len, 4 * w), lambda b, t: (b * nt + t, 0)),
            pl.BlockSpec((t_len, w), lambda b, t: (t, 0)),
            pl.BlockSpec((t_len, w), lambda b, t: (t, 0)),
            _const_spec((1, w)),
            _const_spec((w, w)),
            _const_spec((w, w)),
            _const_spec((n_heads, t_len, t_len)),
            _const_spec((t_len, w)),
            _const_spec((SUBLANES, t_len)),
            _const_spec((1, w)),
            _const_spec((n_heads, 1, w)),
            _const_spec((w, w)),
        ],
        out_specs=pl.BlockSpec((t_len, w), lambda b, t: (b * nt + t, 0)),
        scratch_shapes=[pltpu.VMEM((w, w), F32)],
        compiler_params=_params("parallel", "arbitrary"),
        name="retention",
    )(ret_in, cos, sin, gn, jnp.asarray(rot, BF16), jnp.eye(w, dtype=BF16), dmat, xi, zeta, g_chunk,
      jnp.asarray(hmask), jnp.asarray(same))


def _t5_bucket(rel):
    n = np.maximum(rel, 0)
    exact = REL_BUCKETS // 2
    nf = np.maximum(n, exact).astype(np.float32)
    ratio = np.log(nf / np.float32(exact)) / np.float32(math.log(REL_MAX_DIST / exact))
    large = exact + (ratio * np.float32(REL_BUCKETS - exact)).astype(np.int32)
    return np.where(n < exact, n, np.minimum(large, REL_BUCKETS - 1)).astype(np.int32)


MOBA_MASK = -1e30
MOBA_EXP_LIMIT = 1e37


def _moba_kernel(q_ref, k_ref, v_ref, idown_ref, idprev_ref, rb_ref, qg_ref, kg_ref, same_ref, eye_ref,
                 out_ref, kaug_ref, vaug_ref, kmean_ref, bias_ref, *, n_blocks, far_bucket):
    blk = MOBA_BLOCK
    hd = HEAD_DIM
    qi = pl.program_id(2)
    head0 = 2 * pl.program_id(1)
    same = same_ref[...]
    inv_d = 1.0 / hd
    scale = hd ** -0.5
    lane = lax.broadcasted_iota(jnp.int32, (1, 2 * hd), 1)
    in_h0 = lane < hd

    @pl.when(qi == 0)
    def _():
        def body(j, c):
            r0 = pl.multiple_of(j * blk, blk)
            kb = k_ref[pl.ds(r0, blk), :].astype(F32)
            kn = kb * lax.rsqrt(_mm2_l(kb * kb, same) * inv_d + EPS) * kg_ref[...]
            kmean_ref[pl.ds(j, 1), :] = jnp.mean(kn, axis=0, keepdims=True)
            vb = v_ref[pl.ds(r0, blk), :].astype(F32)
            for e in range(2):
                xo = (1 - e) * hd
                ones = (lane >= xo + n_blocks) & (lane < xo + n_blocks + 3)
                ext = jnp.where((lane == xo + j) | ones, 1.0, 0.0)
                mine = in_h0 if e == 0 else jnp.logical_not(in_h0)
                kaug_ref[e, pl.ds(r0, blk), :] = jnp.where(mine, kn, ext).astype(BF16)
                vaug_ref[e, pl.ds(r0, blk), :] = jnp.where(mine, vb, 1.0).astype(BF16)
            return c
        lax.fori_loop(0, n_blocks, body, 0)
        for e in range(2):
            for tab, ids_ref in enumerate((idown_ref, idprev_ref)):
                ids = ids_ref[...]
                tile = jnp.zeros((blk, blk), F32)
                for b in range(REL_BUCKETS):
                    tile = jnp.where(ids == b, rb_ref[head0 + e, b], tile)
                bias_ref[tab, e] = tile - rb_ref[head0 + e, far_bucket]

    qb = q_ref[...].astype(F32)
    qn = qb * lax.rsqrt(_mm2_l(qb * qb, same) * inv_d + EPS) * qg_ref[...]
    km = kmean_ref[...]
    blk_row = lax.broadcasted_iota(jnp.int32, (n_blocks, blk), 0)
    row = lax.broadcasted_iota(jnp.int32, (blk, blk), 0)
    col = lax.broadcasted_iota(jnp.int32, (blk, blk), 1)
    eye = eye_ref[...]
    q0 = pl.multiple_of(qi * blk, blk)
    qp = pl.multiple_of(jnp.maximum(qi - 1, 0) * blk, blk)

    qbase, bown, bprev = [], [], []
    for e in range(2):
        mine = in_h0 if e == 0 else jnp.logical_not(in_h0)
        gate = _mm3_nt(jnp.where(mine, km, 0.0), qn)
        gate = jnp.where(blk_row < qi, gate, -jnp.inf)
        rank = jnp.zeros_like(gate)
        for i in range(n_blocks):
            gi = gate[i:i + 1, :]
            tie = jnp.where(blk_row > i, 1.0, 0.0)
            rank = rank + jnp.where(gi > gate, 1.0, jnp.where(gi == gate, tie, 0.0))
        chosen = ((blk_row < qi) & (rank < MOBA_TOPK)) | (blk_row == qi)
        mask_t = jnp.where(chosen, 0.0, MOBA_MASK)
        far = jnp.full((1, blk), rb_ref[head0 + e, far_bucket], F32)
        far_hi = far.astype(BF16).astype(F32)
        crow = jnp.concatenate([far_hi, far - far_hi, jnp.zeros((SUBLANES - 2, blk), F32)], axis=0)
        pad = jnp.zeros((hd - n_blocks - SUBLANES, blk), F32)
        zq = jnp.zeros((hd, blk), F32)
        parts = [zq, mask_t, crow, pad] if e == 0 else [mask_t, crow, pad, zq]
        extra = _mm_nt(eye, jnp.concatenate(parts, axis=0))
        qbase.append(jnp.where(mine, qn * scale, 0.0) + extra)
        bown.append(bias_ref[0, e])
        bprev.append(bias_ref[1, e])

    def scores(qa, e, r0, bias=None):
        s = _mm_nt(qa, kaug_ref[e, pl.ds(r0, blk), :])
        return s if bias is None else s + bias

    def finish(accs):
        o = [acc / pltpu.roll(acc, hd, axis=1) for acc in accs]
        return jnp.where(in_h0, o[0], o[1])

    def own_scores(e):
        s = jnp.where(col <= row, scores(_bf(qbase[e]), e, q0, bown[e]), -jnp.inf)
        return s, jnp.max(s, axis=-1, keepdims=True)

    q_shift, acc_fast = [], []
    for e in range(2):
        s, m0 = own_scores(e)
        shift = m0.astype(BF16).astype(F32)
        acc_fast.append(_mm(jnp.exp(s - shift), vaug_ref[e, pl.ds(q0, blk), :]))
        c_lane = (1 - e) * hd + n_blocks + 2
        q_shift.append(_bf(qbase[e] + jnp.where(lane == c_lane, -shift, 0.0)))

    def fast_tile(accs, r0, biases=(None, None)):
        return tuple(acc + _mm(jnp.exp(scores(q_shift[e], e, r0, biases[e])), vaug_ref[e, pl.ds(r0, blk), :])
                     for e, acc in enumerate(accs))

    accs = lax.cond(qi >= 1, lambda a: fast_tile(a, qp, bprev), lambda a: a, tuple(acc_fast))
    n_far = jnp.maximum(qi - 1, 0)

    def far_pair(t, a):
        a = fast_tile(a, pl.multiple_of(2 * t * blk, blk))
        return fast_tile(a, pl.multiple_of((2 * t + 1) * blk, blk))

    accs = lax.fori_loop(0, n_far // 2, far_pair, accs)
    last = pl.multiple_of(jnp.maximum(n_far - 1, 0) * blk, blk)
    accs = lax.cond(n_far % 2 == 1, lambda a: fast_tile(a, last), lambda a: a, accs)

    unsafe = sum(jnp.sum(jnp.where(jnp.abs(acc) < MOBA_EXP_LIMIT, 0.0, 1.0)) for acc in accs)
    overflowed = unsafe > 0.0

    def safe_path(_):
        def update(carry, s, vaug):
            m_i, acc = carry
            m_n = jnp.maximum(m_i, jnp.max(s, axis=-1, keepdims=True))
            return m_n, jnp.exp(m_i - m_n) * acc + _mm(jnp.exp(s - m_n), vaug)

        def block(j, c):
            r0 = pl.multiple_of(j * blk, blk)
            out = []
            for e in range(2):
                s = scores(_bf(qbase[e]), e, r0) + jnp.where(j == qi - 1, bprev[e], 0.0)
                out.append(update(c[e], s, vaug_ref[e, pl.ds(r0, blk), :]))
            return tuple(out)

        init = []
        for e in range(2):
            s, m0 = own_scores(e)
            init.append((m0, _mm(jnp.exp(s - m0), vaug_ref[e, pl.ds(q0, blk), :])))
        c = lax.fori_loop(0, qi, block, tuple(init))
        return finish([c[0][1], c[1][1]])

    out = lax.cond(overflowed, safe_path, lambda _: finish(accs), 0)
    out_ref[...] = out.astype(out_ref.dtype)


def _moba_bucket_tables(seq):
    blk = MOBA_BLOCK
    d = np.arange(blk, dtype=np.int32)[:, None] - np.arange(blk, dtype=np.int32)[None, :]
    far = _t5_bucket(np.arange(blk + 1, max(seq, blk + 2), dtype=np.int32))
    assert (far == far[0]).all()
    return _t5_bucket(d), _t5_bucket(d + blk), int(far[0])


def _moba(moba_in, q_norm, k_norm, rel_bias, batch, seq):
    m = batch * seq
    blk = MOBA_BLOCK
    nb = seq // blk
    assert nb % SUBLANES == 0 and nb + SUBLANES <= HEAD_DIM
    pairs = MOBA_HEADS // 2
    pw = 2 * HEAD_DIM
    ids_own, ids_prev, far_bucket = _moba_bucket_tables(seq)
    same, _ = _head_consts(2)
    qg = jnp.tile(q_norm.astype(F32), 2)[None, :]
    kg = jnp.tile(k_norm.astype(F32), 2)[None, :]
    kern = functools.partial(_moba_kernel, n_blocks=nb, far_bucket=far_bucket)
    return pl.pallas_call(
        kern,
        out_shape=jax.ShapeDtypeStruct((m, MOBA_W), BF16),
        grid=(batch, pairs, nb),
        in_specs=[
            pl.BlockSpec((blk, pw), lambda b, p, i: (b * nb + i, p)),
            pl.BlockSpec((seq, pw), lambda b, p, i: (b, pairs + p)),
            pl.BlockSpec((seq, pw), lambda b, p, i: (b, 2 * pairs + p)),
            _const_spec((blk, blk)),
            _const_spec((blk, blk)),
            pl.BlockSpec(memory_space=pltpu.SMEM),
            _const_spec((1, pw)),
            _const_spec((1, pw)),
            _const_spec((pw, pw)),
            _const_spec((blk, blk)),
        ],
        out_specs=pl.BlockSpec((blk, pw), lambda b, p, i: (b * nb + i, p)),
        scratch_shapes=[pltpu.VMEM((2, seq, pw), BF16), pltpu.VMEM((2, seq, pw), BF16),
                        pltpu.VMEM((nb, pw), F32), pltpu.VMEM((2, 2, blk, blk), F32)],
        compiler_params=_params("parallel", "parallel", "arbitrary"),
        name="moba",
    )(moba_in, moba_in, moba_in, jnp.asarray(ids_own), jnp.asarray(ids_prev), rel_bias.astype(F32),
      qg, kg, jnp.asarray(same, BF16), jnp.eye(blk, dtype=BF16))


def _tri_inverse(n, eye):
    x = eye - n
    p = n
    steps = int(math.log2(GDN_CHUNK)) - 1
    for _ in range(steps):
        p = _mm(p, p)
        x = x + _mm(x, p)
    r = eye - (x + _mm3(n, x))
    return x + _mm(x, r)


def _gdn_kernel(in_ref, scol_ref, srow_ref, cw_ref, alog_b_ref, dtb_b_ref, alog_r_ref, dtb_r_ref, gn_ref,
                lblk_ref, ublk_ref, mstrict_ref, mincl_ref, hmask_ref, same_ref, eye_ref, sela_ref, selb_ref,
                out_ref, s_ref, tail_ref, *, n_heads):
    w = n_heads * HEAD_DIM
    t_len = in_ref.shape[0]
    c_len = GDN_CHUNK
    neg_inf = -jnp.inf

    @pl.when(pl.program_id(1) == 0)
    def _():
        s_ref[...] = jnp.zeros_like(s_ref)
        tail_ref[...] = jnp.zeros_like(tail_ref)

    xin = in_ref[...].astype(F32)
    raw = xin[:, :3 * w]
    z = xin[:, 3 * w:]
    prev = tail_ref[...]
    cw = cw_ref[...]
    acc = raw * cw[GDN_CONV - 1:GDN_CONV, :]
    for kk in range(1, GDN_CONV):
        acc = acc + _shift_rows(raw, prev, kk) * cw[GDN_CONV - 1 - kk:GDN_CONV - kk, :]
    tail_ref[...] = raw[t_len - SUBLANES:, :]
    y = _silu(acc)
    q = y[:, :w]
    k = y[:, w:2 * w]
    v = y[:, 2 * w:]
    same = same_ref[...]
    same_b = _bf(same)
    qn = q * lax.rsqrt(_mm2_l(q * q, same_b) + EPS) * (HEAD_DIM ** -0.5)
    kn = k * lax.rsqrt(_mm2_l(k * k, same_b) + EPS)

    sc = scol_ref[...]
    beta_b = _sigmoid(_mm2_l(sc, selb_ref[...]))
    ld_b = -jnp.exp(alog_b_ref[...]) * _softplus(_mm2_l(sc, sela_ref[...]) + dtb_b_ref[...])
    g_b = _mm2_r(lblk_ref[...], ld_b)
    sr = srow_ref[...]
    ld_r = -jnp.exp(alog_r_ref[...]) * _softplus(sr + dtb_r_ref[...])
    g_r = _mm2_l(ld_r, ublk_ref[...])
    rem_r = _mm2_l(ld_r, _bf(mstrict_ref[...]))

    eye = eye_ref[...]
    k_t = _mm_nt(eye, kn)
    rem_rows = jnp.concatenate(
        [jnp.broadcast_to(rem_r[n_heads + h:n_heads + h + 1, :], (HEAD_DIM, t_len)) for h in range(n_heads)],
        axis=0)
    kdec_t = k_t * jnp.exp(rem_rows)
    eg = jnp.exp(g_b)
    qdec = qn * eg
    wrhs = kn * beta_b * eg
    urhs = v * beta_b
    mstrict = mstrict_ref[...]
    mincl = mincl_ref[...]
    eye_f = eye.astype(F32)
    u = jnp.zeros((t_len, w), F32)
    wv = jnp.zeros((t_len, w), F32)
    qks = []
    for h in range(n_heads):
        hm = hmask_ref[h]
        gcol = g_b[:, h * HEAD_DIM:h * HEAD_DIM + 1]
        grow = g_r[n_heads + h:n_heads + h + 1, :]
        dec = jnp.exp(jnp.where(mincl > 0.0, gcol - grow, neg_inf))
        kk_h = _mm_nt(kn * hm, kn)
        bcol = beta_b[:, h * HEAD_DIM:h * HEAD_DIM + 1]
        n_h = jnp.where(mstrict > 0.0, bcol * kk_h * dec, 0.0)
        t_inv = _tri_inverse(n_h, eye_f)
        t_hi, t_lo = _split(t_inv)
        for part in (t_hi, t_lo):
            u = u + _mm(part, urhs * hm)
            wv = wv + _mm(part, wrhs * hm)
        qks.append(_mm_nt(qn * hm, kn) * dec)

    state = s_ref[...]
    vnews = []
    ointer = []
    for c in range(t_len // c_len):
        rc = slice(c * c_len, (c + 1) * c_len)
        vnew = u[rc] - _mm(wv[rc], state)
        vnews.append(vnew)
        ointer.append(_mm(qdec[rc], state))
        gl = eg[(c + 1) * c_len - 1:(c + 1) * c_len, :]
        state = state * gl + _mm(kdec_t[:, rc], vnew) * same
    s_ref[...] = state
    vn = jnp.concatenate(vnews, axis=0)
    o = jnp.concatenate(ointer, axis=0)
    for h in range(n_heads):
        o = o + _mm(qks[h], vn * hmask_ref[h])
    ms = _mm2_l(o * o, same_b) * (1.0 / HEAD_DIM)
    yo = o * lax.rsqrt(ms + EPS) * gn_ref[...] * _silu(z)
    out_ref[...] = yo.astype(out_ref.dtype)


def _gdn(gdn_in, scol, srow, conv_w, a_log, dt_bias, gdn_norm, batch, seq, t_len):
    n_heads = GDN_HEADS
    w = n_heads * HEAD_DIM
    m = batch * seq
    nt = seq // t_len
    chunk = np.arange(t_len) // GDN_CHUNK
    pos = np.arange(t_len)
    samec = chunk[:, None] == chunk[None, :]
    mincl = (samec & (pos[None, :] <= pos[:, None])).astype(np.float32)
    mstrict = (samec & (pos[None, :] < pos[:, None])).astype(np.float32)
    same, hmask = _head_consts(n_heads)
    head = np.arange(w) // HEAD_DIM
    selb = np.zeros((LANES, w), np.float32)
    sela = np.zeros((LANES, w), np.float32)
    selb[head, np.arange(w)] = 1.0
    sela[n_heads + head, np.arange(w)] = 1.0
    alog_b = jnp.repeat(a_log.astype(F32), HEAD_DIM)[None, :]
    dtb_b = jnp.repeat(dt_bias.astype(F32), HEAD_DIM)[None, :]
    alog_r = jnp.zeros((BF16_ROWS, 1), F32).at[n_heads:2 * n_heads, 0].set(a_log.astype(F32))
    dtb_r = jnp.zeros((BF16_ROWS, 1), F32).at[n_heads:2 * n_heads, 0].set(dt_bias.astype(F32))
    gn = jnp.tile(gdn_norm.astype(F32), n_heads)[None, :]
    kern = functools.partial(_gdn_kernel, n_heads=n_heads)
    return pl.pallas_call(
        kern,
        out_shape=jax.ShapeDtypeStruct((m, w), BF16),
        grid=(batch, nt),
        in_specs=[
            pl.BlockSpec((t_len, 4 * w), lambda b, t: (b * nt + t, 0)),
            pl.BlockSpec((t_len, LANES), lambda b, t: (b * nt + t, 0)),
            pl.BlockSpec((BF16_ROWS, t_len), lambda b, t: (0, b * nt + t)),
            _const_spec((GDN_CONV, 3 * w)),
            _const_spec((1, w)),
            _const_spec((1, w)),
            _const_spec((BF16_ROWS, 1)),
            _const_spec((BF16_ROWS, 1)),
            _const_spec((1, w)),
            _const_spec((t_len, t_len)),
            _const_spec((t_len, t_len)),
            _const_spec((t_len, t_len)),
            _const_spec((t_len, t_len)),
            _const_spec((n_heads, 1, w)),
            _const_spec((w, w)),
            _const_spec((w, w)),
            _const_spec((LANES, w)),
            _const_spec((LANES, w)),
        ],
        out_specs=pl.BlockSpec((t_len, w), lambda b, t: (b * nt + t, 0)),
        scratch_shapes=[pltpu.VMEM((w, w), F32), pltpu.VMEM((SUBLANES, 3 * w), F32)],
        compiler_params=_params("parallel", "arbitrary"),
        name="gdn",
    )(gdn_in, scol, srow, conv_w.astype(F32), alog_b, dtb_b, alog_r, dtb_r, gn,
      jnp.asarray(mincl, BF16), jnp.asarray(mincl.T, BF16), jnp.asarray(mstrict), jnp.asarray(mincl),
      jnp.asarray(hmask), jnp.asarray(same), jnp.eye(w, dtype=BF16), jnp.asarray(sela, BF16),
      jnp.asarray(selb, BF16))


def _outproj_kernel(yr_ref, ym_ref, yg_ref, wr_ref, wm_ref, wg_ref, x_ref, out_ref):
    d = functools.partial(jnp.dot, preferred_element_type=F32)
    out_ref[...] = x_ref[...] + (d(yr_ref[...], wr_ref[...]) + d(ym_ref[...], wm_ref[...])
                                 + d(yg_ref[...], wg_ref[...]))


def _outproj(y_ret, y_moba, y_gdn, w_out, x, tm):
    m, d = x.shape
    wb = _bf(w_out)
    wr = wb[:RET_W]
    wm = wb[RET_W:RET_W + MOBA_W]
    wg = wb[RET_W + MOBA_W:]
    return pl.pallas_call(
        _outproj_kernel,
        out_shape=jax.ShapeDtypeStruct((m, d), F32),
        grid=(m // tm,),
        in_specs=[
            pl.BlockSpec((tm, RET_W), lambda i: (i, 0)),
            pl.BlockSpec((tm, MOBA_W), lambda i: (i, 0)),
            pl.BlockSpec((tm, GDN_W), lambda i: (i, 0)),
            _const_spec((RET_W, d)),
            _const_spec((MOBA_W, d)),
            _const_spec((GDN_W, d)),
            pl.BlockSpec((tm, d), lambda i: (i, 0)),
        ],
        out_specs=pl.BlockSpec((tm, d), lambda i: (i, 0)),
        compiler_params=_params("parallel"),
        name="outproj",
    )(y_ret, y_moba, y_gdn, wr, wm, wg, x)


def _memkv_kernel(mem_ref, nw_ref, wkv_ref, kg_ref, k_ref, v_ref):
    cw = CROSS_HEADS * CROSS_HEAD_DIM
    h = _bf(_rms_rows(mem_ref[...], nw_ref[...]))
    kv = jnp.dot(h, wkv_ref[...], preferred_element_type=F32)
    for hd in range(CROSS_HEADS):
        sl = slice(hd * CROSS_HEAD_DIM, (hd + 1) * CROSS_HEAD_DIM)
        k_ref[:, sl] = _rms_rows(kv[:, sl], kg_ref[...]).astype(k_ref.dtype)
    v_ref[...] = kv[:, cw:].astype(v_ref.dtype)


def _memkv(mem, norm_mem, wkv, k_norm, tm):
    m, d = mem.shape
    cw = CROSS_HEADS * CROSS_HEAD_DIM
    return pl.pallas_call(
        _memkv_kernel,
        out_shape=(jax.ShapeDtypeStruct((m, cw), BF16), jax.ShapeDtypeStruct((m, cw), BF16)),
        grid=(m // tm,),
        in_specs=[
            pl.BlockSpec((tm, d), lambda i: (i, 0)),
            _const_spec((1, d)),
            _const_spec((d, 2 * cw)),
            _const_spec((1, CROSS_HEAD_DIM)),
        ],
        out_specs=(pl.BlockSpec((tm, cw), lambda i: (i, 0)), pl.BlockSpec((tm, cw), lambda i: (i, 0))),
        compiler_params=_params("parallel"),
        name="memkv",
    )(mem, norm_mem.reshape(1, d), _bf(wkv), k_norm.reshape(1, CROSS_HEAD_DIM).astype(F32))


def _cross_kernel(x_ref, nw_ref, wq_ref, qg_ref, k_ref, v_ref, wo_ref, out_ref):
    x = x_ref[...]
    h = _bf(_rms_rows(x, nw_ref[...]))
    q = jnp.dot(h, wq_ref[...], preferred_element_type=F32)
    scale = CROSS_HEAD_DIM ** -0.5
    outs = []
    for hd in range(CROSS_HEADS):
        sl = slice(hd * CROSS_HEAD_DIM, (hd + 1) * CROSS_HEAD_DIM)
        qh = _rms_rows(q[:, sl], qg_ref[...])
        s = _mm_nt(qh, k_ref[:, sl]) * scale
        p = jnp.exp(s - jnp.max(s, axis=-1, keepdims=True))
        l = jnp.sum(p, axis=-1, keepdims=True)
        outs.append(_mm(p, v_ref[:, sl]) / l)
    o = _bf(jnp.concatenate(outs, axis=-1))
    out_ref[...] = x + jnp.dot(o, wo_ref[...], preferred_element_type=F32)


def _cross(x, norm_w, wq, q_norm, kn, v, wo, batch, seq, mem_len, tm):
    m, d = x.shape
    cw = CROSS_HEADS * CROSS_HEAD_DIM
    nt = seq // tm
    return pl.pallas_call(
        _cross_kernel,
        out_shape=jax.ShapeDtypeStruct((m, d), F32),
        grid=(batch, nt),
        in_specs=[
            pl.BlockSpec((tm, d), lambda b, t: (b * nt + t, 0)),
            _const_spec((1, d)),
            _const_spec((d, cw)),
            _const_spec((1, CROSS_HEAD_DIM)),
            pl.BlockSpec((mem_len, cw), lambda b, t: (b, 0)),
            pl.BlockSpec((mem_len, cw), lambda b, t: (b, 0)),
            _const_spec((cw, d)),
        ],
        out_specs=pl.BlockSpec((tm, d), lambda b, t: (b * nt + t, 0)),
        compiler_params=_params("parallel", "parallel"),
        name="cross",
    )(x, norm_w.reshape(1, d), _bf(wq), q_norm.reshape(1, CROSS_HEAD_DIM).astype(F32), kn, v, _bf(wo))


def _ffn_kernel(x_ref, xp_ref, nw_ref, wg_ref, wv_ref, cg_ref, cv_ref, bg_ref, bv_ref, wd_ref, out_ref,
                *, tiles_per_seq, n_chunks):
    x = x_ref[...]
    nw = nw_ref[...]
    h = _bf(_rms_rows(x, nw))
    keep = jnp.where(pl.program_id(0) % tiles_per_seq == 0, 0.0, 1.0)
    hp = _bf(_rms_rows(xp_ref[...], nw))
    out_ref[...] = x

    def conv(u, up, cw, b):
        y = u * cw[FFN_CONV - 1:FFN_CONV, :] + b
        for kk in range(1, FFN_CONV):
            y = y + _shift_rows(u, up, kk) * cw[FFN_CONV - 1 - kk:FFN_CONV - kk, :]
        return y

    for c in range(n_chunks):
        wg = wg_ref[c]
        wv = wv_ref[c]
        d = functools.partial(jnp.dot, preferred_element_type=F32)
        gate = conv(d(h, wg), d(hp, wg) * keep, cg_ref[c], bg_ref[c])
        val = conv(d(h, wv), d(hp, wv) * keep, cv_ref[c], bv_ref[c])
        act = _bf(_silu(gate) * val)
        out_ref[...] += d(act, wd_ref[c])


def _ffn(x, norm_w, w_up, conv_w, conv_b, w_down, seq, tm, fc):
    m, d = x.shape
    d_ff = w_down.shape[0]
    nc = d_ff // fc
    halo = BF16_ROWS

    def chunks(a):
        return a.reshape(a.shape[0], nc, fc).transpose(1, 0, 2)

    wg = chunks(_bf(w_up[:, :d_ff]))
    wv = chunks(_bf(w_up[:, d_ff:]))
    cg = chunks(conv_w[:, :d_ff].astype(F32))
    cv = chunks(conv_w[:, d_ff:].astype(F32))
    bg = chunks(conv_b[None, :d_ff].astype(F32))
    bv = chunks(conv_b[None, d_ff:].astype(F32))
    wd = _bf(w_down).reshape(nc, fc, d)
    kern = functools.partial(_ffn_kernel, tiles_per_seq=seq // tm, n_chunks=nc)
    return pl.pallas_call(
        kern,
        out_shape=jax.ShapeDtypeStruct((m, d), F32),
        grid=(m // tm,),
        in_specs=[
            pl.BlockSpec((tm, d), lambda i: (i, 0)),
            pl.BlockSpec((halo, d), lambda i: (jnp.maximum(i * (tm // halo) - 1, 0), 0)),
            _const_spec((1, d)),
            _const_spec((nc, d, fc)),
            _const_spec((nc, d, fc)),
            _const_spec((nc, FFN_CONV, fc)),
            _const_spec((nc, FFN_CONV, fc)),
            _const_spec((nc, 1, fc)),
            _const_spec((nc, 1, fc)),
            _const_spec((nc, fc, d)),
        ],
        out_specs=pl.BlockSpec((tm, d), lambda i: (i, 0)),
        compiler_params=_params("parallel"),
        name="ffn",
    )(x, x, norm_w.reshape(1, d), wg, wv, cg, cv, bg, bv, wd)


def _tiles(seq):
    tm = min(512, seq)
    t_len = min(256, seq)
    return tm, t_len


def kernel(x, mem, norm_mix, w_in, ret_norm, moba_q_norm, moba_k_norm, gdn_conv, gdn_a_log, gdn_dt_bias,
           gdn_norm, w_out, norm_cross, norm_mem, cross_wq, cross_wkv, cross_q_norm, cross_k_norm, cross_wo,
           norm_ffn, ffn_up, ffn_conv, ffn_conv_b, ffn_down, rel_bias):
    batch, seq, d = x.shape
    mem_len = mem.shape[1]
    depth = w_in.shape[0]
    tm, t_len = _tiles(seq)
    xf = x.reshape(batch * seq, d)
    memf = mem.reshape(batch * mem_len, d)
    for l in range(depth):
        ret_in, moba_in, gdn_in, scol, srow = _inproj(xf, norm_mix[l], w_in[l], tm)
        y_ret = _retention(ret_in, ret_norm[l], batch, seq, t_len)
        y_moba = _moba(moba_in, moba_q_norm[l], moba_k_norm[l], rel_bias, batch, seq)
        y_gdn = _gdn(gdn_in, scol, srow, gdn_conv[l], gdn_a_log[l], gdn_dt_bias[l], gdn_norm[l],
                     batch, seq, t_len)
        xf = _outproj(y_ret, y_moba, y_gdn, w_out[l], xf, tm)
        kn, v = _memkv(memf, norm_mem[l], cross_wkv[l], cross_k_norm[l], mem_len)
        xf = _cross(xf, norm_cross[l], cross_wq[l], cross_q_norm[l], kn, v, cross_wo[l],
                    batch, seq, mem_len, tm)
        xf = _ffn(xf, norm_ffn[l], ffn_up[l], ffn_conv[l], ffn_conv_b[l], ffn_down[l], seq, tm, 256)
    return xf.reshape(batch, seq, d)
```

```python
import functools
import math

import numpy as np
import jax
import jax.numpy as jnp
from jax import lax
from jax.experimental import pallas as pl
from jax.experimental.pallas import tpu as pltpu

F32 = jnp.float32
BF16 = jnp.bfloat16

HEAD_DIM = 64
RET_HEADS = 4
MOBA_HEADS = 8
GDN_HEADS = 4
RET_W = RET_HEADS * HEAD_DIM
MOBA_W = MOBA_HEADS * HEAD_DIM
GDN_W = GDN_HEADS * HEAD_DIM
ROPE_BASE = 10000.0
MOBA_BLOCK = 256
MOBA_TOPK = 3
REL_BUCKETS = 32
REL_MAX_DIST = 128
GDN_CHUNK = 64
GDN_CONV = 4
CROSS_HEADS = 4
CROSS_HEAD_DIM = 128
FFN_CONV = 3
EPS = 1e-6

LANES = 128
SUBLANES = 8
BF16_ROWS = 16
VMEM_LIMIT = 56 * 1024 * 1024


def _bf(a):
    return a.astype(BF16)


def _mm(a, b):
    return jnp.dot(_bf(a), _bf(b), preferred_element_type=F32)


def _mm_nt(a, b):
    return lax.dot_general(_bf(a), _bf(b), (((1,), (1,)), ((), ())), preferred_element_type=F32)


def _split(a):
    hi = a.astype(BF16)
    lo = (a - hi.astype(F32)).astype(BF16)
    return hi, lo


def _mm3(a, b):
    ah, al = _split(a)
    bh, bl = _split(b)
    d = functools.partial(jnp.dot, preferred_element_type=F32)
    return d(ah, bh) + (d(ah, bl) + d(al, bh))


def _mm3_nt(a, b):
    ah, al = _split(a)
    bh, bl = _split(b)
    d = functools.partial(lax.dot_general, dimension_numbers=(((1,), (1,)), ((), ())),
                          preferred_element_type=F32)
    return d(ah, bh) + (d(ah, bl) + d(al, bh))


def _mm2_l(a, b_exact):
    ah, al = _split(a)
    d = functools.partial(jnp.dot, preferred_element_type=F32)
    return d(ah, b_exact) + d(al, b_exact)


def _mm2_r(a_exact, b):
    bh, bl = _split(b)
    d = functools.partial(jnp.dot, preferred_element_type=F32)
    return d(a_exact, bh) + d(a_exact, bl)


def _sigmoid(x):
    return 1.0 / (1.0 + jnp.exp(-x))


def _silu(x):
    return x * _sigmoid(x)


def _softplus(x):
    return jnp.maximum(x, 0.0) + jnp.log1p(jnp.exp(-jnp.abs(x)))


def _rms_rows(x, w):
    return x * lax.rsqrt(jnp.mean(x * x, axis=-1, keepdims=True) + EPS) * w


def _shift_rows(u, prev, k):
    r = pltpu.roll(u, k, axis=0)
    rp = pltpu.roll(prev, k, axis=0)[:SUBLANES]
    row = lax.broadcasted_iota(jnp.int32, (SUBLANES, u.shape[1]), 0)
    top = jnp.where(row < k, rp, r[:SUBLANES])
    return jnp.concatenate([top, r[SUBLANES:]], axis=0)


def _const_spec(shape):
    nd = len(shape)
    return pl.BlockSpec(shape, lambda *_: (0,) * nd)


def _params(*sem):
    return pltpu.CompilerParams(dimension_semantics=sem, vmem_limit_bytes=VMEM_LIMIT)


def _inproj_kernel(x_ref, nw_ref, w_ref, ws_ref, wst_ref, ret_ref, moba_ref, gdn_ref, scol_ref, srow_ref):
    x = x_ref[...]
    hf = _rms_rows(x, nw_ref[...])
    h = _bf(hf)
    off = 0
    for ref in (ret_ref, moba_ref, gdn_ref):
        width = ref.shape[1]
        for c in range(0, width, 512):
            ref[:, c:c + 512] = jnp.dot(h, w_ref[:, off + c:off + c + 512],
                                        preferred_element_type=F32).astype(ref.dtype)
        off += width
    scol_ref[...] = _mm3(hf, ws_ref[...])
    srow_ref[...] = _mm3_nt(wst_ref[...], hf)


def _inproj(x, norm_w, w_in, tm):
    m, d = x.shape
    main = 4 * RET_W + 3 * MOBA_W + 4 * GDN_W
    w_main = _bf(w_in[:, :main])
    w_small = jnp.zeros((d, LANES), F32).at[:, :2 * GDN_HEADS].set(w_in[:, main:])
    w_small_t = jnp.zeros((BF16_ROWS, d), F32).at[:2 * GDN_HEADS, :].set(w_in[:, main:].T)
    out_shape = (
        jax.ShapeDtypeStruct((m, 4 * RET_W), BF16),
        jax.ShapeDtypeStruct((m, 3 * MOBA_W), BF16),
        jax.ShapeDtypeStruct((m, 4 * GDN_W), BF16),
        jax.ShapeDtypeStruct((m, LANES), F32),
        jax.ShapeDtypeStruct((BF16_ROWS, m), F32),
    )
    return pl.pallas_call(
        _inproj_kernel,
        out_shape=out_shape,
        grid=(m // tm,),
        in_specs=[
            pl.BlockSpec((tm, d), lambda i: (i, 0)),
            _const_spec((1, d)),
            _const_spec((d, main)),
            _const_spec((d, LANES)),
            _const_spec((BF16_ROWS, d)),
        ],
        out_specs=(
            pl.BlockSpec((tm, 4 * RET_W), lambda i: (i, 0)),
            pl.BlockSpec((tm, 3 * MOBA_W), lambda i: (i, 0)),
            pl.BlockSpec((tm, 4 * GDN_W), lambda i: (i, 0)),
            pl.BlockSpec((tm, LANES), lambda i: (i, 0)),
            pl.BlockSpec((BF16_ROWS, tm), lambda i: (0, i)),
        ),
        compiler_params=_params("parallel"),
        name="inproj",
    )(x, norm_w.reshape(1, d), w_main, w_small, w_small_t)


def _head_consts(n_heads):
    w = n_heads * HEAD_DIM
    head = np.arange(w) // HEAD_DIM
    same = (head[:, None] == head[None, :]).astype(np.float32)
    hmask = (head[None, None, :] == np.arange(n_heads)[:, None, None]).astype(np.float32)
    return same, hmask


def _ret_kernel(in_ref, cos_ref, sin_ref, gn_ref, rot_ref, eye_ref, dmat_ref, xi_ref, zeta_ref, gt_ref,
                hmask_ref, same_ref, out_ref, s_ref, *, n_heads):
    w = n_heads * HEAD_DIM
    t_len = in_ref.shape[0]

    @pl.when(pl.program_id(1) == 0)
    def _():
        s_ref[...] = jnp.zeros_like(s_ref)

    xin = in_ref[...]
    q = xin[:, :w]
    k = xin[:, w:2 * w]
    v = xin[:, 2 * w:3 * w]
    g = xin[:, 3 * w:].astype(F32)
    cos = cos_ref[...]
    sin = sin_ref[...]
    rot = rot_ref[...]
    qr = q.astype(F32) * cos + jnp.dot(q, rot, preferred_element_type=F32) * sin
    kr = k.astype(F32) * cos + jnp.dot(k, rot, preferred_element_type=F32) * sin
    same = same_ref[...]
    k_t = _mm_nt(eye_ref[...], kr)
    state = s_ref[...]
    o = _mm(qr, state) * xi_ref[...]
    vf = v.astype(F32)
    hms = [hmask_ref[h] for h in range(n_heads)]
    ss = [_mm_nt(qr * hm, kr) * dmat_ref[h] for h, hm in enumerate(hms)]
    for s, hm in zip(ss, hms):
        o = o + _mm(s, vf * hm)
    zr = jnp.concatenate(
        [jnp.broadcast_to(zeta_ref[h:h + 1, :], (HEAD_DIM, t_len)) for h in range(n_heads)], axis=0)
    s_ref[...] = state * gt_ref[...] + _mm(k_t * zr, v) * same
    ms = _mm2_l(o * o, _bf(same)) * (1.0 / HEAD_DIM)
    y = o * lax.rsqrt(ms + EPS) * gn_ref[...] * _silu(g)
    out_ref[...] = y.astype(out_ref.dtype)


def _retention(ret_in, ret_norm, batch, seq, t_len):
    n_heads = RET_HEADS
    w = n_heads * HEAD_DIM
    m = batch * seq
    nt = seq // t_len
    half = HEAD_DIM // 2
    pos = np.arange(seq, dtype=np.float32)
    inv_freq = (ROPE_BASE ** (-jnp.arange(half, dtype=F32) / half))
    ang = jnp.asarray(pos)[:, None] * inv_freq[None, :]
    cos = jnp.tile(jnp.concatenate([jnp.cos(ang)] * 2, axis=-1), (1, n_heads))
    sin = jnp.tile(jnp.concatenate([jnp.sin(ang)] * 2, axis=-1), (1, n_heads))
    j = np.arange(w)
    rot = np.zeros((w, w), np.float32)
    first = (j % HEAD_DIM) < half
    rot[j[first] + half, j[first]] = -1.0
    rot[j[~first] - half, j[~first]] = 1.0
    same, hmask = _head_consts(n_heads)
    log_gamma = jnp.log1p(-jnp.exp2(-5.0 - jnp.arange(n_heads, dtype=F32)))
    idx = jnp.arange(t_len, dtype=F32)
    diff = idx[:, None] - idx[None, :]
    scale = HEAD_DIM ** -0.5
    dmat = jnp.where(diff >= 0, jnp.exp(log_gamma[:, None, None] * jnp.maximum(diff, 0.0)), 0.0) * scale
    xi = jnp.repeat(jnp.exp(log_gamma[:, None] * (idx + 1.0)).T, HEAD_DIM, axis=1)
    zeta = jnp.exp(log_gamma[:, None] * (t_len - 1.0 - idx)) * scale
    zeta = jnp.zeros((SUBLANES, t_len), F32).at[:n_heads].set(zeta)
    g_chunk = jnp.repeat(jnp.exp(log_gamma * t_len), HEAD_DIM)[None, :]
    gn = jnp.tile(ret_norm.astype(F32), n_heads)[None, :]
    kern = functools.partial(_ret_kernel, n_heads=n_heads)
    return pl.pallas_call(
        kern,
        out_shape=jax.ShapeDtypeStruct((m, w), BF16),
        grid=(batch, nt),
        in_specs=[
            pl.BlockSpec((t_len, 4 * w), lambda b, t: (b * nt + t, 0)),
            pl.BlockSpec((t_len, w), lambda b, t: (t, 0)),
            pl.BlockSpec((t_len, w), lambda b, t: (t, 0)),
            _const_spec((1, w)),
            _const_spec((w, w)),
            _const_spec((w, w)),
            _const_spec((n_heads, t_len, t_len)),
            _const_spec((t_len, w)),
            _const_spec((SUBLANES, t_len)),
            _const_spec((1, w)),
            _const_spec((n_heads, 1, w)),
            _const_spec((w, w)),
        ],
        out_specs=pl.BlockSpec((t_len, w), lambda b, t: (b * nt + t, 0)),
        scratch_shapes=[pltpu.VMEM((w, w), F32)],
        compiler_params=_params("parallel", "arbitrary"),
        name="retention",
    )(ret_in, cos, sin, gn, jnp.asarray(rot, BF16), jnp.eye(w, dtype=BF16), dmat, xi, zeta, g_chunk,
      jnp.asarray(hmask), jnp.asarray(same))


def _t5_bucket(rel):
    n = np.maximum(rel, 0)
    exact = REL_BUCKETS // 2
    nf = np.maximum(n, exact).astype(np.float32)
    ratio = np.log(nf / np.float32(exact)) / np.float32(math.log(REL_MAX_DIST / exact))
    large = exact + (ratio * np.float32(REL_BUCKETS - exact)).astype(np.int32)
    return np.where(n < exact, n, np.minimum(large, REL_BUCKETS - 1)).astype(np.int32)


MOBA_MASK = -1e30
MOBA_EXP_LIMIT = 1e37


def _moba_kernel(q_ref, k_ref, v_ref, idown_ref, idprev_ref, rb_ref, qg_ref, kg_ref, same_ref, eye_ref,
                 out_ref, kaug_ref, vaug_ref, kmean_ref, bias_ref, *, n_blocks, far_bucket):
    blk = MOBA_BLOCK
    hd = HEAD_DIM
    qi = pl.program_id(2)
    head0 = 2 * pl.program_id(1)
    same = same_ref[...]
    inv_d = 1.0 / hd
    scale = hd ** -0.5
    lane = lax.broadcasted_iota(jnp.int32, (1, 2 * hd), 1)
    in_h0 = lane < hd

    @pl.when(qi == 0)
    def _():
        def body(j, c):
            r0 = pl.multiple_of(j * blk, blk)
            kb = k_ref[pl.ds(r0, blk), :].astype(F32)
            kn = kb * lax.rsqrt(_mm2_l(kb * kb, same) * inv_d + EPS) * kg_ref[...]
            kmean_ref[pl.ds(j, 1), :] = jnp.mean(kn, axis=0, keepdims=True)
            vb = v_ref[pl.ds(r0, blk), :].astype(F32)
            for e in range(2):
                xo = (1 - e) * hd
                ones = (lane >= xo + n_blocks) & (lane < xo + n_blocks + 3)
                ext = jnp.where((lane == xo + j) | ones, 1.0, 0.0)
                mine = in_h0 if e == 0 else jnp.logical_not(in_h0)
                kaug_ref[e, pl.ds(r0, blk), :] = jnp.where(mine, kn, ext).astype(BF16)
                vaug_ref[e, pl.ds(r0, blk), :] = jnp.where(mine, vb, 1.0).astype(BF16)
            return c
        lax.fori_loop(0, n_blocks, body, 0)
        for e in range(2):
            for tab, ids_ref in enumerate((idown_ref, idprev_ref)):
                ids = ids_ref[...]
                tile = jnp.zeros((blk, blk), F32)
                for b in range(REL_BUCKETS):
                    tile = jnp.where(ids == b, rb_ref[head0 + e, b], tile)
                bias_ref[tab, e] = tile - rb_ref[head0 + e, far_bucket]

    qb = q_ref[...].astype(F32)
    qn = qb * lax.rsqrt(_mm2_l(qb * qb, same) * inv_d + EPS) * qg_ref[...]
    km = kmean_ref[...]
    blk_row = lax.broadcasted_iota(jnp.int32, (n_blocks, blk), 0)
    row = lax.broadcasted_iota(jnp.int32, (blk, blk), 0)
    col = lax.broadcasted_iota(jnp.int32, (blk, blk), 1)
    eye = eye_ref[...]
    q0 = pl.multiple_of(qi * blk, blk)
    qp = pl.multiple_of(jnp.maximum(qi - 1, 0) * blk, blk)

    both = range(2)
    mine = [in_h0, jnp.logical_not(in_h0)]
    gates = [jnp.where(blk_row < qi, _mm3_nt(jnp.where(mine[e], km, 0.0), qn), -jnp.inf) for e in both]
    ranks = [jnp.zeros_like(g) for g in gates]
    for i in range(n_blocks):
        tie = jnp.where(blk_row > i, 1.0, 0.0)
        ranks = [r + jnp.where(g[i:i + 1, :] > g, 1.0, jnp.where(g[i:i + 1, :] == g, tie, 0.0))
                 for r, g in zip(ranks, gates)]
    mask_t = [jnp.where(((blk_row < qi) & (r < MOBA_TOPK)) | (blk_row == qi), 0.0, MOBA_MASK)
              for r in ranks]
    pad = jnp.zeros((hd - n_blocks - SUBLANES, blk), F32)
    zq = jnp.zeros((hd, blk), F32)
    extras = []
    for e in both:
        far = jnp.full((1, blk), rb_ref[head0 + e, far_bucket], F32)
        far_hi = far.astype(BF16).astype(F32)
        crow = jnp.concatenate([far_hi, far - far_hi, jnp.zeros((SUBLANES - 2, blk), F32)], axis=0)
        extras.append([zq, mask_t[e], crow, pad] if e == 0 else [mask_t[e], crow, pad, zq])
    extras = [_mm_nt(eye, jnp.concatenate(parts, axis=0)) for parts in extras]
    qbase = [jnp.where(mine[e], qn * scale, 0.0) + extras[e] for e in both]
    bown = [bias_ref[0, e] for e in both]
    bprev = [bias_ref[1, e] for e in both]

    def scores(qa, e, r0, bias=None):
        s = _mm_nt(qa, kaug_ref[e, pl.ds(r0, blk), :])
        return s if bias is None else s + bias

    def finish(accs):
        o = [acc / pltpu.roll(acc, hd, axis=1) for acc in accs]
        return jnp.where(in_h0, o[0], o[1])

    def own_scores(e):
        s = jnp.where(col <= row, scores(_bf(qbase[e]), e, q0, bown[e]), -jnp.inf)
        return s, jnp.max(s, axis=-1, keepdims=True)

    own = [own_scores(e) for e in both]
    shifts = [m0.astype(BF16).astype(F32) for (_, m0) in own]
    acc_fast = [_mm(jnp.exp(own[e][0] - shifts[e]), vaug_ref[e, pl.ds(q0, blk), :]) for e in both]
    q_shift = [_bf(qbase[e] + jnp.where(lane == (1 - e) * hd + n_blocks + 2, -shifts[e], 0.0)) for e in both]

    def fast_tiles(accs, starts, biases=(None, None)):
        ss = [[scores(q_shift[e], e, r0, biases[e]) for e in both] for r0 in starts]
        for s, r0 in zip(ss, starts):
            accs = tuple(accs[e] + _mm(jnp.exp(s[e]), vaug_ref[e, pl.ds(r0, blk), :]) for e in both)
        return accs

    def fast_tile(accs, r0, biases=(None, None)):
        return fast_tiles(accs, [r0], biases)

    accs = lax.cond(qi >= 1, lambda a: fast_tile(a, qp, bprev), lambda a: a, tuple(acc_fast))
    n_far = jnp.maximum(qi - 1, 0)

    def far_pair(t, a):
        return fast_tiles(a, [pl.multiple_of(2 * t * blk, blk), pl.multiple_of((2 * t + 1) * blk, blk)])

    accs = lax.fori_loop(0, n_far // 2, far_pair, accs)
    last = pl.multiple_of(jnp.maximum(n_far - 1, 0) * blk, blk)
    accs = lax.cond(n_far % 2 == 1, lambda a: fast_tile(a, last), lambda a: a, accs)

    unsafe = sum(jnp.sum(jnp.where(jnp.abs(acc) < MOBA_EXP_LIMIT, 0.0, 1.0)) for acc in accs)
    overflowed = unsafe > 0.0

    def safe_path(_):
        def update(carry, s, vaug):
            m_i, acc = carry
            m_n = jnp.maximum(m_i, jnp.max(s, axis=-1, keepdims=True))
            return m_n, jnp.exp(m_i - m_n) * acc + _mm(jnp.exp(s - m_n), vaug)

        def block(j, c):
            r0 = pl.multiple_of(j * blk, blk)
            out = []
            for e in range(2):
                s = scores(_bf(qbase[e]), e, r0) + jnp.where(j == qi - 1, bprev[e], 0.0)
                out.append(update(c[e], s, vaug_ref[e, pl.ds(r0, blk), :]))
            return tuple(out)

        init = []
        for e in range(2):
            s, m0 = own_scores(e)
            init.append((m0, _mm(jnp.exp(s - m0), vaug_ref[e, pl.ds(q0, blk), :])))
        c = lax.fori_loop(0, qi, block, tuple(init))
        return finish([c[0][1], c[1][1]])

    out = lax.cond(overflowed, safe_path, lambda _: finish(accs), 0)
    out_ref[...] = out.astype(out_ref.dtype)


def _moba_bucket_tables(seq):
    blk = MOBA_BLOCK
    d = np.arange(blk, dtype=np.int32)[:, None] - np.arange(blk, dtype=np.int32)[None, :]
    far = _t5_bucket(np.arange(blk + 1, max(seq, blk + 2), dtype=np.int32))
    assert (far == far[0]).all()
    return _t5_bucket(d), _t5_bucket(d + blk), int(far[0])


def _moba(moba_in, q_norm, k_norm, rel_bias, batch, seq):
    m = batch * seq
    blk = MOBA_BLOCK
    nb = seq // blk
    assert nb % SUBLANES == 0 and nb + SUBLANES <= HEAD_DIM
    pairs = MOBA_HEADS // 2
    pw = 2 * HEAD_DIM
    ids_own, ids_prev, far_bucket = _moba_bucket_tables(seq)
    same, _ = _head_consts(2)
    qg = jnp.tile(q_norm.astype(F32), 2)[None, :]
    kg = jnp.tile(k_norm.astype(F32), 2)[None, :]
    kern = functools.partial(_moba_kernel, n_blocks=nb, far_bucket=far_bucket)
    return pl.pallas_call(
        kern,
        out_shape=jax.ShapeDtypeStruct((m, MOBA_W), BF16),
        grid=(batch, pairs, nb),
        in_specs=[
            pl.BlockSpec((blk, pw), lambda b, p, i: (b * nb + i, p)),
            pl.BlockSpec((seq, pw), lambda b, p, i: (b, pairs + p)),
            pl.BlockSpec((seq, pw), lambda b, p, i: (b, 2 * pairs + p)),
            _const_spec((blk, blk)),
            _const_spec((blk, blk)),
            pl.BlockSpec(memory_space=pltpu.SMEM),
            _const_spec((1, pw)),
            _const_spec((1, pw)),
            _const_spec((pw, pw)),
            _const_spec((blk, blk)),
        ],
        out_specs=pl.BlockSpec((blk, pw), lambda b, p, i: (b * nb + i, p)),
        scratch_shapes=[pltpu.VMEM((2, seq, pw), BF16), pltpu.VMEM((2, seq, pw), BF16),
                        pltpu.VMEM((nb, pw), F32), pltpu.VMEM((2, 2, blk, blk), F32)],
        compiler_params=_params("parallel", "parallel", "arbitrary"),
        name="moba",
    )(moba_in, moba_in, moba_in, jnp.asarray(ids_own), jnp.asarray(ids_prev), rel_bias.astype(F32),
      qg, kg, jnp.asarray(same, BF16), jnp.eye(blk, dtype=BF16))


def _tri_inverse(ns, eye):
    xs = [eye - n for n in ns]
    ps = list(ns)
    steps = int(math.log2(GDN_CHUNK)) - 1
    for _ in range(steps):
        ps = [_mm(p, p) for p in ps]
        xs = [x + _mm(x, p) for x, p in zip(xs, ps)]
    rs = [eye - (x + _mm3(n, x)) for n, x in zip(ns, xs)]
    return [x + _mm(x, r) for x, r in zip(xs, rs)]


def _gdn_kernel(in_ref, scol_ref, srow_ref, *rest, n_heads):
    consts, (out_ref, s_ref, tail_ref) = rest[:-3], rest[-3:]

    @pl.when(pl.program_id(1) == 0)
    def _():
        s_ref[...] = jnp.zeros_like(s_ref)
        tail_ref[...] = jnp.zeros_like(tail_ref)

    for i in range(in_ref.shape[0]):
        _gdn_block(in_ref.at[i], scol_ref.at[i], srow_ref.at[i], *consts,
                   out_ref.at[i], s_ref.at[i], tail_ref.at[i], n_heads=n_heads)


def _gdn_block(in_ref, scol_ref, srow_ref, cw_ref, alog_b_ref, dtb_b_ref, alog_r_ref, dtb_r_ref, gn_ref,
               lblk_ref, ublk_ref, mstrict_ref, mincl_ref, hmask_ref, same_ref, eye_ref, sela_ref, selb_ref,
               out_ref, s_ref, tail_ref, *, n_heads):
    w = n_heads * HEAD_DIM
    t_len = in_ref.shape[0]
    c_len = GDN_CHUNK
    neg_inf = -jnp.inf

    xin = in_ref[...].astype(F32)
    raw = xin[:, :3 * w]
    z = xin[:, 3 * w:]
    prev = tail_ref[...]
    cw = cw_ref[...]
    acc = raw * cw[GDN_CONV - 1:GDN_CONV, :]
    for kk in range(1, GDN_CONV):
        acc = acc + _shift_rows(raw, prev, kk) * cw[GDN_CONV - 1 - kk:GDN_CONV - kk, :]
    tail_ref[...] = raw[t_len - SUBLANES:, :]
    y = _silu(acc)
    q = y[:, :w]
    k = y[:, w:2 * w]
    v = y[:, 2 * w:]
    same = same_ref[...]
    same_b = _bf(same)
    qn = q * lax.rsqrt(_mm2_l(q * q, same_b) + EPS) * (HEAD_DIM ** -0.5)
    kn = k * lax.rsqrt(_mm2_l(k * k, same_b) + EPS)

    sc = scol_ref[...]
    beta_b = _sigmoid(_mm2_l(sc, selb_ref[...]))
    ld_b = -jnp.exp(alog_b_ref[...]) * _softplus(_mm2_l(sc, sela_ref[...]) + dtb_b_ref[...])
    g_b = _mm2_r(lblk_ref[...], ld_b)
    sr = srow_ref[...]
    ld_r = -jnp.exp(alog_r_ref[...]) * _softplus(sr + dtb_r_ref[...])
    g_r = _mm2_l(ld_r, ublk_ref[...])
    rem_r = _mm2_l(ld_r, _bf(mstrict_ref[...]))

    eye = eye_ref[...]
    k_t = _mm_nt(eye, kn)
    rem_rows = jnp.concatenate(
        [jnp.broadcast_to(rem_r[n_heads + h:n_heads + h + 1, :], (HEAD_DIM, t_len)) for h in range(n_heads)],
        axis=0)
    kdec_t = k_t * jnp.exp(rem_rows)
    eg = jnp.exp(g_b)
    qdec = qn * eg
    wrhs = kn * beta_b * eg
    urhs = v * beta_b
    mstrict = mstrict_ref[...]
    mincl = mincl_ref[...]
    eye_f = eye.astype(F32)
    u = jnp.zeros((t_len, w), F32)
    wv = jnp.zeros((t_len, w), F32)
    heads = range(n_heads)
    hms = [hmask_ref[h] for h in heads]
    decs = [jnp.exp(jnp.where(mincl > 0.0, g_b[:, h * HEAD_DIM:h * HEAD_DIM + 1]
                              - g_r[n_heads + h:n_heads + h + 1, :], neg_inf)) for h in heads]
    kks = [_mm_nt(kn * hms[h], kn) for h in heads]
    ns = [jnp.where(mstrict > 0.0, beta_b[:, h * HEAD_DIM:h * HEAD_DIM + 1] * kks[h] * decs[h], 0.0)
          for h in heads]
    t_invs = _tri_inverse(ns, eye_f)
    for h in heads:
        for part in _split(t_invs[h]):
            u = u + _mm(part, urhs * hms[h])
            wv = wv + _mm(part, wrhs * hms[h])
    qks = [_mm_nt(qn * hms[h], kn) * decs[h] for h in heads]

    state = s_ref[...]
    vnews = []
    ointer = []
    for c in range(t_len // c_len):
        rc = slice(c * c_len, (c + 1) * c_len)
        vnew = u[rc] - _mm(wv[rc], state)
        vnews.append(vnew)
        ointer.append(_mm(qdec[rc], state))
        gl = eg[(c + 1) * c_len - 1:(c + 1) * c_len, :]
        state = state * gl + _mm(kdec_t[:, rc], vnew) * same
    s_ref[...] = state
    vn = jnp.concatenate(vnews, axis=0)
    o = jnp.concatenate(ointer, axis=0)
    for h in range(n_heads):
        o = o + _mm(qks[h], vn * hmask_ref[h])
    ms = _mm2_l(o * o, same_b) * (1.0 / HEAD_DIM)
    yo = o * lax.rsqrt(ms + EPS) * gn_ref[...] * _silu(z)
    out_ref[...] = yo.astype(out_ref.dtype)


def _gdn(gdn_in, scol, srow, conv_w, a_log, dt_bias, gdn_norm, batch, seq, t_len):
    n_heads = GDN_HEADS
    w = n_heads * HEAD_DIM
    m = batch * seq
    nt = seq // t_len
    chunk = np.arange(t_len) // GDN_CHUNK
    pos = np.arange(t_len)
    samec = chunk[:, None] == chunk[None, :]
    mincl = (samec & (pos[None, :] <= pos[:, None])).astype(np.float32)
    mstrict = (samec & (pos[None, :] < pos[:, None])).astype(np.float32)
    same, hmask = _head_consts(n_heads)
    head = np.arange(w) // HEAD_DIM
    selb = np.zeros((LANES, w), np.float32)
    sela = np.zeros((LANES, w), np.float32)
    selb[head, np.arange(w)] = 1.0
    sela[n_heads + head, np.arange(w)] = 1.0
    alog_b = jnp.repeat(a_log.astype(F32), HEAD_DIM)[None, :]
    dtb_b = jnp.repeat(dt_bias.astype(F32), HEAD_DIM)[None, :]
    alog_r = jnp.zeros((BF16_ROWS, 1), F32).at[n_heads:2 * n_heads, 0].set(a_log.astype(F32))
    dtb_r = jnp.zeros((BF16_ROWS, 1), F32).at[n_heads:2 * n_heads, 0].set(dt_bias.astype(F32))
    gn = jnp.tile(gdn_norm.astype(F32), n_heads)[None, :]
    kern = functools.partial(_gdn_kernel, n_heads=n_heads)
    g = 1
    srow_b = srow.reshape(BF16_ROWS, batch, seq).transpose(1, 0, 2)
    out = pl.pallas_call(
        kern,
        out_shape=jax.ShapeDtypeStruct((batch, seq, w), BF16),
        grid=(batch // g, nt),
        in_specs=[
            pl.BlockSpec((g, t_len, 4 * w), lambda b, t: (b, t, 0)),
            pl.BlockSpec((g, t_len, LANES), lambda b, t: (b, t, 0)),
            pl.BlockSpec((g, BF16_ROWS, t_len), lambda b, t: (b, 0, t)),
            _const_spec((GDN_CONV, 3 * w)),
            _const_spec((1, w)),
            _const_spec((1, w)),
            _const_spec((BF16_ROWS, 1)),
            _const_spec((BF16_ROWS, 1)),
            _const_spec((1, w)),
            _const_spec((t_len, t_len)),
            _const_spec((t_len, t_len)),
            _const_spec((t_len, t_len)),
            _const_spec((t_len, t_len)),
            _const_spec((n_heads, 1, w)),
            _const_spec((w, w)),
            _const_spec((w, w)),
            _const_spec((LANES, w)),
            _const_spec((LANES, w)),
        ],
        out_specs=pl.BlockSpec((g, t_len, w), lambda b, t: (b, t, 0)),
        scratch_shapes=[pltpu.VMEM((g, w, w), F32), pltpu.VMEM((g, SUBLANES, 3 * w), F32)],
        compiler_params=_params("parallel", "arbitrary"),
        name="gdn",
    )(gdn_in.reshape(batch, seq, 4 * w), scol.reshape(batch, seq, LANES), srow_b, conv_w.astype(F32),
      alog_b, dtb_b, alog_r, dtb_r, gn,
      jnp.asarray(mincl, BF16), jnp.asarray(mincl.T, BF16), jnp.asarray(mstrict), jnp.asarray(mincl),
      jnp.asarray(hmask), jnp.asarray(same), jnp.eye(w, dtype=BF16), jnp.asarray(sela, BF16),
      jnp.asarray(selb, BF16))
    return out.reshape(m, w)


def _outproj_kernel(yr_ref, ym_ref, yg_ref, wr_ref, wm_ref, wg_ref, x_ref, out_ref):
    d = functools.partial(jnp.dot, preferred_element_type=F32)
    out_ref[...] = x_ref[...] + (d(yr_ref[...], wr_ref[...]) + d(ym_ref[...], wm_ref[...])
                                 + d(yg_ref[...], wg_ref[...]))


def _outproj(y_ret, y_moba, y_gdn, w_out, x, tm):
    m, d = x.shape
    wb = _bf(w_out)
    wr = wb[:RET_W]
    wm = wb[RET_W:RET_W + MOBA_W]
    wg = wb[RET_W + MOBA_W:]
    return pl.pallas_call(
        _outproj_kernel,
        out_shape=jax.ShapeDtypeStruct((m, d), F32),
        grid=(m // tm,),
        in_specs=[
            pl.BlockSpec((tm, RET_W), lambda i: (i, 0)),
            pl.BlockSpec((tm, MOBA_W), lambda i: (i, 0)),
            pl.BlockSpec((tm, GDN_W), lambda i: (i, 0)),
            _const_spec((RET_W, d)),
            _const_spec((MOBA_W, d)),
            _const_spec((GDN_W, d)),
            pl.BlockSpec((tm, d), lambda i: (i, 0)),
        ],
        out_specs=pl.BlockSpec((tm, d), lambda i: (i, 0)),
        compiler_params=_params("parallel"),
        name="outproj",
    )(y_ret, y_moba, y_gdn, wr, wm, wg, x)


def _memkv_kernel(mem_ref, nw_ref, wkv_ref, kg_ref, k_ref, v_ref):
    cw = CROSS_HEADS * CROSS_HEAD_DIM
    h = _bf(_rms_rows(mem_ref[...], nw_ref[...]))
    kv = jnp.dot(h, wkv_ref[...], preferred_element_type=F32)
    for hd in range(CROSS_HEADS):
        sl = slice(hd * CROSS_HEAD_DIM, (hd + 1) * CROSS_HEAD_DIM)
        k_ref[:, sl] = _rms_rows(kv[:, sl], kg_ref[...]).astype(k_ref.dtype)
    v_ref[...] = kv[:, cw:].astype(v_ref.dtype)


def _memkv(mem, norm_mem, wkv, k_norm, tm):
    m, d = mem.shape
    cw = CROSS_HEADS * CROSS_HEAD_DIM
    return pl.pallas_call(
        _memkv_kernel,
        out_shape=(jax.ShapeDtypeStruct((m, cw), BF16), jax.ShapeDtypeStruct((m, cw), BF16)),
        grid=(m // tm,),
        in_specs=[
            pl.BlockSpec((tm, d), lambda i: (i, 0)),
            _const_spec((1, d)),
            _const_spec((d, 2 * cw)),
            _const_spec((1, CROSS_HEAD_DIM)),
        ],
        out_specs=(pl.BlockSpec((tm, cw), lambda i: (i, 0)), pl.BlockSpec((tm, cw), lambda i: (i, 0))),
        compiler_params=_params("parallel"),
        name="memkv",
    )(mem, norm_mem.reshape(1, d), _bf(wkv), k_norm.reshape(1, CROSS_HEAD_DIM).astype(F32))


def _cross_kernel(x_ref, nw_ref, wq_ref, qg_ref, k_ref, v_ref, wo_ref, out_ref):
    x = x_ref[...]
    h = _bf(_rms_rows(x, nw_ref[...]))
    q = jnp.dot(h, wq_ref[...], preferred_element_type=F32)
    scale = CROSS_HEAD_DIM ** -0.5
    sls = [slice(hd * CROSS_HEAD_DIM, (hd + 1) * CROSS_HEAD_DIM) for hd in range(CROSS_HEADS)]
    qhs = [_rms_rows(q[:, sl], qg_ref[...]) for sl in sls]
    ss = [_mm_nt(qh, k_ref[:, sl]) * scale for qh, sl in zip(qhs, sls)]
    ps = [jnp.exp(s - jnp.max(s, axis=-1, keepdims=True)) for s in ss]
    ls = [jnp.sum(p, axis=-1, keepdims=True) for p in ps]
    outs = [_mm(p, v_ref[:, sl]) / l for p, sl, l in zip(ps, sls, ls)]
    o = _bf(jnp.concatenate(outs, axis=-1))
    out_ref[...] = x + jnp.dot(o, wo_ref[...], preferred_element_type=F32)


def _cross(x, norm_w, wq, q_norm, kn, v, wo, batch, seq, mem_len, tm):
    m, d = x.shape
    cw = CROSS_HEADS * CROSS_HEAD_DIM
    nt = seq // tm
    return pl.pallas_call(
        _cross_kernel,
        out_shape=jax.ShapeDtypeStruct((m, d), F32),
        grid=(batch, nt),
        in_specs=[
            pl.BlockSpec((tm, d), lambda b, t: (b * nt + t, 0)),
            _const_spec((1, d)),
            _const_spec((d, cw)),
            _const_spec((1, CROSS_HEAD_DIM)),
            pl.BlockSpec((mem_len, cw), lambda b, t: (b, 0)),
            pl.BlockSpec((mem_len, cw), lambda b, t: (b, 0)),
            _const_spec((cw, d)),
        ],
        out_specs=pl.BlockSpec((tm, d), lambda b, t: (b * nt + t, 0)),
        compiler_params=_params("parallel", "parallel"),
        name="cross",
    )(x, norm_w.reshape(1, d), _bf(wq), q_norm.reshape(1, CROSS_HEAD_DIM).astype(F32), kn, v, _bf(wo))


def _ffn_kernel(x_ref, xp_ref, nw_ref, wg_ref, wv_ref, cg_ref, cv_ref, bg_ref, bv_ref, wd_ref, out_ref,
                *, tiles_per_seq, n_chunks):
    x = x_ref[...]
    nw = nw_ref[...]
    h = _bf(_rms_rows(x, nw))
    keep = jnp.where(pl.program_id(0) % tiles_per_seq == 0, 0.0, 1.0)
    hp = _bf(_rms_rows(xp_ref[...], nw))
    out_ref[...] = x

    def conv(u, up, cw, b):
        y = u * cw[FFN_CONV - 1:FFN_CONV, :] + b
        for kk in range(1, FFN_CONV):
            y = y + _shift_rows(u, up, kk) * cw[FFN_CONV - 1 - kk:FFN_CONV - kk, :]
        return y

    d = functools.partial(jnp.dot, preferred_element_type=F32)

    def up(c):
        wg = wg_ref[c]
        wv = wv_ref[c]
        return d(h, wg), d(hp, wg) * keep, d(h, wv), d(hp, wv) * keep

    nxt = up(0)
    for c in range(n_chunks):
        ug, ugp, uv, uvp = nxt
        if c + 1 < n_chunks:
            nxt = up(c + 1)
        gate = conv(ug, ugp, cg_ref[c], bg_ref[c])
        val = conv(uv, uvp, cv_ref[c], bv_ref[c])
        out_ref[...] += d(_bf(_silu(gate) * val), wd_ref[c])


def _ffn(x, norm_w, w_up, conv_w, conv_b, w_down, seq, tm, fc):
    m, d = x.shape
    d_ff = w_down.shape[0]
    nc = d_ff // fc
    halo = BF16_ROWS

    def chunks(a):
        return a.reshape(a.shape[0], nc, fc).transpose(1, 0, 2)

    wg = chunks(_bf(w_up[:, :d_ff]))
    wv = chunks(_bf(w_up[:, d_ff:]))
    cg = chunks(conv_w[:, :d_ff].astype(F32))
    cv = chunks(conv_w[:, d_ff:].astype(F32))
    bg = chunks(conv_b[None, :d_ff].astype(F32))
    bv = chunks(conv_b[None, d_ff:].astype(F32))
    wd = _bf(w_down).reshape(nc, fc, d)
    kern = functools.partial(_ffn_kernel, tiles_per_seq=seq // tm, n_chunks=nc)
    return pl.pallas_call(
        kern,
        out_shape=jax.ShapeDtypeStruct((m, d), F32),
        grid=(m // tm,),
        in_specs=[
            pl.BlockSpec((tm, d), lambda i: (i, 0)),
            pl.BlockSpec((halo, d), lambda i: (jnp.maximum(i * (tm // halo) - 1, 0), 0)),
            _const_spec((1, d)),
            _const_spec((nc, d, fc)),
            _const_spec((nc, d, fc)),
            _const_spec((nc, FFN_CONV, fc)),
            _const_spec((nc, FFN_CONV, fc)),
            _const_spec((nc, 1, fc)),
            _const_spec((nc, 1, fc)),
            _const_spec((nc, fc, d)),
        ],
        out_specs=pl.BlockSpec((tm, d), lambda i: (i, 0)),
        compiler_params=_params("parallel"),
        name="ffn",
    )(x, x, norm_w.reshape(1, d), wg, wv, cg, cv, bg, bv, wd)


def _tiles(seq):
    tm = min(512, seq)
    t_len = min(256, seq)
    return tm, t_len


def kernel(x, mem, norm_mix, w_in, ret_norm, moba_q_norm, moba_k_norm, gdn_conv, gdn_a_log, gdn_dt_bias,
           gdn_norm, w_out, norm_cross, norm_mem, cross_wq, cross_wkv, cross_q_norm, cross_k_norm, cross_wo,
           norm_ffn, ffn_up, ffn_conv, ffn_conv_b, ffn_down, rel_bias):
    batch, seq, d = x.shape
    mem_len = mem.shape[1]
    depth = w_in.shape[0]
    tm, t_len = _tiles(seq)
    xf = x.reshape(batch * seq, d)
    memf = mem.reshape(batch * mem_len, d)
    for l in range(depth):
        ret_in, moba_in, gdn_in, scol, srow = _inproj(xf, norm_mix[l], w_in[l], tm)
        y_ret = _retention(ret_in, ret_norm[l], batch, seq, t_len)
        y_moba = _moba(moba_in, moba_q_norm[l], moba_k_norm[l], rel_bias, batch, seq)
        y_gdn = _gdn(gdn_in, scol, srow, gdn_conv[l], gdn_a_log[l], gdn_dt_bias[l], gdn_norm[l],
                     batch, seq, t_len)
        xf = _outproj(y_ret, y_moba, y_gdn, w_out[l], xf, tm)
        kn, v = _memkv(memf, norm_mem[l], cross_wkv[l], cross_k_norm[l], mem_len)
        xf = _cross(xf, norm_cross[l], cross_wq[l], cross_q_norm[l], kn, v, cross_wo[l],
                    batch, seq, mem_len, tm)
        xf = _ffn(xf, norm_ffn[l], ffn_up[l], ffn_conv[l], ffn_conv_b[l], ffn_down[l], seq, tm, 256)
    return xf.reshape(batch, seq, d)
```

```python
import functools
import math

import numpy as np
import jax
import jax.numpy as jnp
from jax import lax
from jax.experimental import pallas as pl
from jax.experimental.pallas import tpu as pltpu

F32 = jnp.float32
BF16 = jnp.bfloat16

HEAD_DIM = 64
RET_HEADS = 4
MOBA_HEADS = 8
GDN_HEADS = 4
RET_W = RET_HEADS * HEAD_DIM
MOBA_W = MOBA_HEADS * HEAD_DIM
GDN_W = GDN_HEADS * HEAD_DIM
ROPE_BASE = 10000.0
MOBA_BLOCK = 256
MOBA_TOPK = 3
REL_BUCKETS = 32
REL_MAX_DIST = 128
GDN_CHUNK = 64
GDN_CONV = 4
CROSS_HEADS = 4
CROSS_HEAD_DIM = 128
FFN_CONV = 3
EPS = 1e-6

LANES = 128
SUBLANES = 8
BF16_ROWS = 16
VMEM_LIMIT = 56 * 1024 * 1024


def _bf(a):
    return a.astype(BF16)


def _mm(a, b):
    return jnp.dot(_bf(a), _bf(b), preferred_element_type=F32)


def _mm_nt(a, b):
    return lax.dot_general(_bf(a), _bf(b), (((1,), (1,)), ((), ())), preferred_element_type=F32)


def _split(a):
    hi = a.astype(BF16)
    lo = (a - hi.astype(F32)).astype(BF16)
    return hi, lo


def _mm3(a, b):
    ah, al = _split(a)
    bh, bl = _split(b)
    d = functools.partial(jnp.dot, preferred_element_type=F32)
    return d(ah, bh) + (d(ah, bl) + d(al, bh))


def _mm3_nt(a, b):
    ah, al = _split(a)
    bh, bl = _split(b)
    d = functools.partial(lax.dot_general, dimension_numbers=(((1,), (1,)), ((), ())),
                          preferred_element_type=F32)
    return d(ah, bh) + (d(ah, bl) + d(al, bh))


def _mm2_l(a, b_exact):
    ah, al = _split(a)
    d = functools.partial(jnp.dot, preferred_element_type=F32)
    return d(ah, b_exact) + d(al, b_exact)


def _mm2_r(a_exact, b):
    bh, bl = _split(b)
    d = functools.partial(jnp.dot, preferred_element_type=F32)
    return d(a_exact, bh) + d(a_exact, bl)


def _sigmoid(x):
    return 1.0 / (1.0 + jnp.exp(-x))


def _silu(x):
    return x * _sigmoid(x)


def _softplus(x):
    return jnp.maximum(x, 0.0) + jnp.log1p(jnp.exp(-jnp.abs(x)))


def _rms_rows(x, w):
    return x * lax.rsqrt(jnp.mean(x * x, axis=-1, keepdims=True) + EPS) * w


def _shift_rows(u, prev, k):
    r = pltpu.roll(u, k, axis=0)
    rp = pltpu.roll(prev, k, axis=0)[:SUBLANES]
    row = lax.broadcasted_iota(jnp.int32, (SUBLANES, u.shape[1]), 0)
    top = jnp.where(row < k, rp, r[:SUBLANES])
    return jnp.concatenate([top, r[SUBLANES:]], axis=0)


def _const_spec(shape):
    nd = len(shape)
    return pl.BlockSpec(shape, lambda *_: (0,) * nd)


def _params(*sem):
    return pltpu.CompilerParams(dimension_semantics=sem, vmem_limit_bytes=VMEM_LIMIT)


def _inproj_kernel(x_ref, nw_ref, w_ref, ws_ref, wst_ref, ret_ref, moba_ref, gdn_ref, scol_ref, srow_ref):
    x = x_ref[...]
    hf = _rms_rows(x, nw_ref[...])
    h = _bf(hf)
    off = 0
    for ref in (ret_ref, moba_ref, gdn_ref):
        width = ref.shape[1]
        for c in range(0, width, 512):
            ref[:, c:c + 512] = jnp.dot(h, w_ref[:, off + c:off + c + 512],
                                        preferred_element_type=F32).astype(ref.dtype)
        off += width
    scol_ref[...] = _mm3(hf, ws_ref[...])
    srow_ref[...] = _mm3_nt(wst_ref[...], hf)


def _inproj(x, norm_w, w_in, tm):
    m, d = x.shape
    main = 4 * RET_W + 3 * MOBA_W + 4 * GDN_W
    w_main = _bf(w_in[:, :main])
    w_small = jnp.zeros((d, LANES), F32).at[:, :2 * GDN_HEADS].set(w_in[:, main:])
    w_small_t = jnp.zeros((BF16_ROWS, d), F32).at[:2 * GDN_HEADS, :].set(w_in[:, main:].T)
    out_shape = (
        jax.ShapeDtypeStruct((m, 4 * RET_W), BF16),
        jax.ShapeDtypeStruct((m, 3 * MOBA_W), BF16),
        jax.ShapeDtypeStruct((m, 4 * GDN_W), BF16),
        jax.ShapeDtypeStruct((m, LANES), F32),
        jax.ShapeDtypeStruct((BF16_ROWS, m), F32),
    )
    return pl.pallas_call(
        _inproj_kernel,
        out_shape=out_shape,
        grid=(m // tm,),
        in_specs=[
            pl.BlockSpec((tm, d), lambda i: (i, 0)),
            _const_spec((1, d)),
            _const_spec((d, main)),
            _const_spec((d, LANES)),
            _const_spec((BF16_ROWS, d)),
        ],
        out_specs=(
            pl.BlockSpec((tm, 4 * RET_W), lambda i: (i, 0)),
            pl.BlockSpec((tm, 3 * MOBA_W), lambda i: (i, 0)),
            pl.BlockSpec((tm, 4 * GDN_W), lambda i: (i, 0)),
            pl.BlockSpec((tm, LANES), lambda i: (i, 0)),
            pl.BlockSpec((BF16_ROWS, tm), lambda i: (0, i)),
        ),
        compiler_params=_params("parallel"),
        name="inproj",
    )(x, norm_w.reshape(1, d), w_main, w_small, w_small_t)


def _head_consts(n_heads):
    w = n_heads * HEAD_DIM
    head = np.arange(w) // HEAD_DIM
    same = (head[:, None] == head[None, :]).astype(np.float32)
    hmask = (head[None, None, :] == np.arange(n_heads)[:, None, None]).astype(np.float32)
    return same, hmask


def _ret_kernel(in_ref, cos_ref, sin_ref, gn_ref, rot_ref, eye_ref, dmat_ref, xi_ref, zeta_ref, gt_ref,
                hmask_ref, same_ref, out_ref, s_ref, *, n_heads):
    w = n_heads * HEAD_DIM
    t_len = in_ref.shape[0]

    @pl.when(pl.program_id(1) == 0)
    def _():
        s_ref[...] = jnp.zeros_like(s_ref)

    xin = in_ref[...]
    q = xin[:, :w]
    k = xin[:, w:2 * w]
    v = xin[:, 2 * w:3 * w]
    g = xin[:, 3 * w:].astype(F32)
    cos = cos_ref[...]
    sin = sin_ref[...]
    rot = rot_ref[...]
    qr = q.astype(F32) * cos + jnp.dot(q, rot, preferred_element_type=F32) * sin
    kr = k.astype(F32) * cos + jnp.dot(k, rot, preferred_element_type=F32) * sin
    same = same_ref[...]
    k_t = _mm_nt(eye_ref[...], kr)
    state = s_ref[...]
    o = _mm(qr, state) * xi_ref[...]
    vf = v.astype(F32)
    hms = [hmask_ref[h] for h in range(n_heads)]
    ss = [_mm_nt(qr * hm, kr) * dmat_ref[h] for h, hm in enumerate(hms)]
    for s, hm in zip(ss, hms):
        o = o + _mm(s, vf * hm)
    zr = jnp.concatenate(
        [jnp.broadcast_to(zeta_ref[h:h + 1, :], (HEAD_DIM, t_len)) for h in range(n_heads)], axis=0)
    s_ref[...] = state * gt_ref[...] + _mm(k_t * zr, v) * same
    ms = _mm2_l(o * o, _bf(same)) * (1.0 / HEAD_DIM)
    y = o * lax.rsqrt(ms + EPS) * gn_ref[...] * _silu(g)
    out_ref[...] = y.astype(out_ref.dtype)


def _retention(ret_in, ret_norm, batch, seq, t_len):
    n_heads = RET_HEADS
    w = n_heads * HEAD_DIM
    m = batch * seq
    nt = seq // t_len
    half = HEAD_DIM // 2
    pos = np.arange(seq, dtype=np.float32)
    inv_freq = (ROPE_BASE ** (-jnp.arange(half, dtype=F32) / half))
    ang = jnp.asarray(pos)[:, None] * inv_freq[None, :]
    cos = jnp.tile(jnp.concatenate([jnp.cos(ang)] * 2, axis=-1), (1, n_heads))
    sin = jnp.tile(jnp.concatenate([jnp.sin(ang)] * 2, axis=-1), (1, n_heads))
    j = np.arange(w)
    rot = np.zeros((w, w), np.float32)
    first = (j % HEAD_DIM) < half
    rot[j[first] + half, j[first]] = -1.0
    rot[j[~first] - half, j[~first]] = 1.0
    same, hmask = _head_consts(n_heads)
    log_gamma = jnp.log1p(-jnp.exp2(-5.0 - jnp.arange(n_heads, dtype=F32)))
    idx = jnp.arange(t_len, dtype=F32)
    diff = idx[:, None] - idx[None, :]
    scale = HEAD_DIM ** -0.5
    dmat = jnp.where(diff >= 0, jnp.exp(log_gamma[:, None, None] * jnp.maximum(diff, 0.0)), 0.0) * scale
    xi = jnp.repeat(jnp.exp(log_gamma[:, None] * (idx + 1.0)).T, HEAD_DIM, axis=1)
    zeta = jnp.exp(log_gamma[:, None] * (t_len - 1.0 - idx)) * scale
    zeta = jnp.zeros((SUBLANES, t_len), F32).at[:n_heads].set(zeta)
    g_chunk = jnp.repeat(jnp.exp(log_gamma * t_len), HEAD_DIM)[None, :]
    gn = jnp.tile(ret_norm.astype(F32), n_heads)[None, :]
    kern = functools.partial(_ret_kernel, n_heads=n_heads)
    return pl.pallas_call(
        kern,
        out_shape=jax.ShapeDtypeStruct((m, w), BF16),
        grid=(batch, nt),
        in_specs=[
            pl.BlockSpec((t_len, 4 * w), lambda b, t: (b * nt + t, 0)),
            pl.BlockSpec((t_len, w), lambda b, t: (t, 0)),
            pl.BlockSpec((t_len, w), lambda b, t: (t, 0)),
            _const_spec((1, w)),
            _const_spec((w, w)),
            _const_spec((w, w)),
            _const_spec((n_heads, t_len, t_len)),
            _const_spec((t_len, w)),
            _const_spec((SUBLANES, t_len)),
            _const_spec((1, w)),
            _const_spec((n_heads, 1, w)),
            _const_spec((w, w)),
        ],
        out_specs=pl.BlockSpec((t_len, w), lambda b, t: (b * nt + t, 0)),
        scratch_shapes=[pltpu.VMEM((w, w), F32)],
        compiler_params=_params("parallel", "arbitrary"),
        name="retention",
    )(ret_in, cos, sin, gn, jnp.asarray(rot, BF16), jnp.eye(w, dtype=BF16), dmat, xi, zeta, g_chunk,
      jnp.asarray(hmask), jnp.asarray(same))


def _t5_bucket(rel):
    n = np.maximum(rel, 0)
    exact = REL_BUCKETS // 2
    nf = np.maximum(n, exact).astype(np.float32)
    ratio = np.log(nf / np.float32(exact)) / np.float32(math.log(REL_MAX_DIST / exact))
    large = exact + (ratio * np.float32(REL_BUCKETS - exact)).astype(np.int32)
    return np.where(n < exact, n, np.minimum(large, REL_BUCKETS - 1)).astype(np.int32)


MOBA_MASK = -1e30
MOBA_EXP_LIMIT = 1e37


def _moba_kernel(q_ref, k_ref, v_ref, idown_ref, idprev_ref, rb_ref, qg_ref, kg_ref, same_ref,
                 eye_ref, out_ref, kaug_ref, vaug_ref, kmean_ref, bias_ref, *, n_blocks, far_bucket):
    blk = MOBA_BLOCK
    hd = HEAD_DIM
    qi = pl.program_id(2)
    head0 = 2 * pl.program_id(0)
    same = same_ref[...]
    inv_d = 1.0 / hd
    scale = hd ** -0.5
    lane = lax.broadcasted_iota(jnp.int32, (1, 2 * hd), 1)
    in_h0 = lane < hd
    both = range(2)
    mine = [in_h0, jnp.logical_not(in_h0)]
    blk_row = lax.broadcasted_iota(jnp.int32, (n_blocks, blk), 0)

    def augmented_queries(q_blk, j):
        qf = q_blk.astype(F32)
        qn = qf * lax.rsqrt(_mm2_l(qf * qf, same) * inv_d + EPS) * qg_ref[...]
        km = kmean_ref[...]
        gates = [jnp.where(blk_row < j, _mm3_nt(jnp.where(mine[e], km, 0.0), qn), -jnp.inf) for e in both]
        ranks = [jnp.zeros_like(g) for g in gates]
        for i in range(n_blocks):
            tie = jnp.where(blk_row > i, 1.0, 0.0)
            ranks = [r + jnp.where(g[i:i + 1, :] > g, 1.0, jnp.where(g[i:i + 1, :] == g, tie, 0.0))
                     for r, g in zip(ranks, gates)]
        mask_t = [jnp.where(((blk_row < j) & (r < MOBA_TOPK)) | (blk_row == j), 0.0, MOBA_MASK)
                  for r in ranks]
        pad = jnp.zeros((hd - n_blocks - SUBLANES, blk), F32)
        zq = jnp.zeros((hd, blk), F32)
        extras = []
        for e in both:
            far = jnp.full((1, blk), rb_ref[head0 + e, far_bucket], F32)
            far_hi = far.astype(BF16).astype(F32)
            crow = jnp.concatenate([far_hi, far - far_hi, jnp.zeros((SUBLANES - 2, blk), F32)], axis=0)
            extras.append([zq, mask_t[e], crow, pad] if e == 0 else [mask_t[e], crow, pad, zq])
        extras = [_mm_nt(eye_ref[...], jnp.concatenate(parts, axis=0)) for parts in extras]
        return [jnp.where(mine[e], qn * scale, 0.0) + extras[e] for e in both]

    @pl.when((qi == 0) & (pl.program_id(1) == 0))
    def _():
        for e in both:
            for tab, ids_ref in enumerate((idown_ref, idprev_ref)):
                ids = ids_ref[...]
                tile = jnp.zeros((blk, blk), F32)
                for b in range(REL_BUCKETS):
                    tile = jnp.where(ids == b, rb_ref[head0 + e, b], tile)
                bias_ref[tab, e] = tile - rb_ref[head0 + e, far_bucket]

    @pl.when(qi == 0)
    def _():
        def body(j, c):
            r0 = pl.multiple_of(j * blk, blk)
            kb = k_ref[pl.ds(r0, blk), :].astype(F32)
            kn = kb * lax.rsqrt(_mm2_l(kb * kb, same) * inv_d + EPS) * kg_ref[...]
            kmean_ref[pl.ds(j, 1), :] = jnp.mean(kn, axis=0, keepdims=True)
            vb = v_ref[pl.ds(r0, blk), :].astype(F32)
            for e in both:
                xo = (1 - e) * hd
                ones = (lane >= xo + n_blocks) & (lane < xo + n_blocks + 3)
                ext = jnp.where((lane == xo + j) | ones, 1.0, 0.0)
                kaug_ref[e, pl.ds(r0, blk), :] = jnp.where(mine[e], kn, ext).astype(BF16)
                vaug_ref[e, pl.ds(r0, blk), :] = jnp.where(mine[e], vb, 1.0).astype(BF16)
            return c
        lax.fori_loop(0, n_blocks, body, 0)

    row = lax.broadcasted_iota(jnp.int32, (blk, blk), 0)
    col = lax.broadcasted_iota(jnp.int32, (blk, blk), 1)
    q0 = pl.multiple_of(qi * blk, blk)
    qp = pl.multiple_of(jnp.maximum(qi - 1, 0) * blk, blk)
    qbase = augmented_queries(q_ref[...], qi)
    bown = [bias_ref[0, e] for e in both]
    bprev = [bias_ref[1, e] for e in both]

    def scores(qa, e, r0, bias=None):
        s = _mm_nt(qa, kaug_ref[e, pl.ds(r0, blk), :])
        return s if bias is None else s + bias

    def finish(accs):
        o = [acc / pltpu.roll(acc, hd, axis=1) for acc in accs]
        return jnp.where(in_h0, o[0], o[1])

    k_own = [kaug_ref[e, pl.ds(q0, blk), :] for e in both]
    shifts = [jnp.sum(jnp.where(mine[e], qbase[e] * k_own[e].astype(F32), 0.0), axis=-1, keepdims=True)
              .astype(BF16).astype(F32) for e in both]
    q_shift = [_bf(qbase[e] + jnp.where(lane == (1 - e) * hd + n_blocks + 2, -shifts[e], 0.0)) for e in both]

    def fast_tiles(accs, starts):
        ss = [[scores(q_shift[e], e, r0) for e in both] for r0 in starts]
        for s, r0 in zip(ss, starts):
            accs = tuple(accs[e] + _mm(jnp.exp(s[e]), vaug_ref[e, pl.ds(r0, blk), :]) for e in both)
        return accs

    def fast_tile(accs, r0):
        return fast_tiles(accs, [r0])

    no_prev = jnp.where(qi >= 1, 0.0, MOBA_MASK)
    s_own = [jnp.where(col <= row, scores(q_shift[e], e, q0, bown[e]), -jnp.inf) for e in both]
    s_prev = [scores(q_shift[e], e, qp, bprev[e]) + no_prev for e in both]
    accs = tuple(_mm(jnp.exp(s_own[e]), vaug_ref[e, pl.ds(q0, blk), :])
                 + _mm(jnp.exp(s_prev[e]), vaug_ref[e, pl.ds(qp, blk), :]) for e in both)
    n_far = jnp.maximum(qi - 1, 0)

    def far_pair(t, a):
        return fast_tiles(a, [pl.multiple_of(2 * t * blk, blk), pl.multiple_of((2 * t + 1) * blk, blk)])

    accs = lax.fori_loop(0, n_far // 2, far_pair, accs)
    last = pl.multiple_of(jnp.maximum(n_far - 1, 0) * blk, blk)
    accs = lax.cond(n_far % 2 == 1, lambda a: fast_tile(a, last), lambda a: a, accs)

    unsafe = sum(jnp.sum(jnp.where(jnp.abs(acc) < MOBA_EXP_LIMIT, 0.0, 1.0)) for acc in accs)
    overflowed = unsafe > 0.0

    def safe_path(_):
        def update(carry, s, vaug):
            m_i, acc = carry
            m_n = jnp.maximum(m_i, jnp.max(s, axis=-1, keepdims=True))
            return m_n, jnp.exp(m_i - m_n) * acc + _mm(jnp.exp(s - m_n), vaug)

        def block(j, c):
            r0 = pl.multiple_of(j * blk, blk)
            out = []
            for e in range(2):
                s = scores(_bf(qbase[e]), e, r0) + jnp.where(j == qi - 1, bprev[e], 0.0)
                out.append(update(c[e], s, vaug_ref[e, pl.ds(r0, blk), :]))
            return tuple(out)

        init = []
        for e in range(2):
            s = jnp.where(col <= row, scores(_bf(qbase[e]), e, q0, bown[e]), -jnp.inf)
            m0 = jnp.max(s, axis=-1, keepdims=True)
            init.append((m0, _mm(jnp.exp(s - m0), vaug_ref[e, pl.ds(q0, blk), :])))
        c = lax.fori_loop(0, qi, block, tuple(init))
        return finish([c[0][1], c[1][1]])

    out = lax.cond(overflowed, safe_path, lambda _: finish(accs), 0)
    out_ref[...] = out.astype(out_ref.dtype)


def _moba_bucket_tables(seq):
    blk = MOBA_BLOCK
    d = np.arange(blk, dtype=np.int32)[:, None] - np.arange(blk, dtype=np.int32)[None, :]
    far = _t5_bucket(np.arange(blk + 1, max(seq, blk + 2), dtype=np.int32))
    assert (far == far[0]).all()
    return _t5_bucket(d), _t5_bucket(d + blk), int(far[0])


def _moba(moba_in, q_norm, k_norm, rel_bias, batch, seq):
    m = batch * seq
    blk = MOBA_BLOCK
    nb = seq // blk
    assert nb % SUBLANES == 0 and nb + SUBLANES <= HEAD_DIM
    pairs = MOBA_HEADS // 2
    pw = 2 * HEAD_DIM
    ids_own, ids_prev, far_bucket = _moba_bucket_tables(seq)
    same, _ = _head_consts(2)
    qg = jnp.tile(q_norm.astype(F32), 2)[None, :]
    kg = jnp.tile(k_norm.astype(F32), 2)[None, :]
    kern = functools.partial(_moba_kernel, n_blocks=nb, far_bucket=far_bucket)
    return pl.pallas_call(
        kern,
        out_shape=jax.ShapeDtypeStruct((m, MOBA_W), BF16),
        grid=(pairs, batch, nb),
        in_specs=[
            pl.BlockSpec((blk, pw), lambda p, b, i: (b * nb + i, p)),
            pl.BlockSpec((seq, pw), lambda p, b, i: (b, pairs + p)),
            pl.BlockSpec((seq, pw), lambda p, b, i: (b, 2 * pairs + p)),
            _const_spec((blk, blk)),
            _const_spec((blk, blk)),
            pl.BlockSpec(memory_space=pltpu.SMEM),
            _const_spec((1, pw)),
            _const_spec((1, pw)),
            _const_spec((pw, pw)),
            _const_spec((blk, blk)),
        ],
        out_specs=pl.BlockSpec((blk, pw), lambda p, b, i: (b * nb + i, p)),
        scratch_shapes=[pltpu.VMEM((2, seq, pw), BF16), pltpu.VMEM((2, seq, pw), BF16),
                        pltpu.VMEM((nb, pw), F32), pltpu.VMEM((2, 2, blk, blk), F32)],
        compiler_params=_params("arbitrary", "arbitrary", "arbitrary"),
        name="moba",
    )(moba_in, moba_in, moba_in, jnp.asarray(ids_own), jnp.asarray(ids_prev), rel_bias.astype(F32),
      qg, kg, jnp.asarray(same, BF16), jnp.eye(blk, dtype=BF16))


def _tri_inverse(ns, eye):
    xs = [eye - n for n in ns]
    ps = list(ns)
    steps = int(math.log2(GDN_CHUNK)) - 1
    for _ in range(steps):
        ps = [_mm(p, p) for p in ps]
        xs = [x + _mm(x, p) for x, p in zip(xs, ps)]
    rs = [eye - (x + _mm3(n, x)) for n, x in zip(ns, xs)]
    return [x + _mm(x, r) for x, r in zip(xs, rs)]


def _gdn_kernel(in_ref, scol_ref, srow_ref, *rest, n_heads):
    consts, (out_ref, s_ref, tail_ref) = rest[:-3], rest[-3:]

    @pl.when(pl.program_id(1) == 0)
    def _():
        s_ref[...] = jnp.zeros_like(s_ref)
        tail_ref[...] = jnp.zeros_like(tail_ref)

    for i in range(in_ref.shape[0]):
        _gdn_block(in_ref.at[i], scol_ref.at[i], srow_ref.at[i], *consts,
                   out_ref.at[i], s_ref.at[i], tail_ref.at[i], n_heads=n_heads)


def _gdn_block(in_ref, scol_ref, srow_ref, cw_ref, alog_b_ref, dtb_b_ref, alog_r_ref, dtb_r_ref, gn_ref,
               lblk_ref, ublk_ref, mstrict_ref, mincl_ref, hmask_ref, same_ref, eye_ref, sela_ref, selb_ref,
               out_ref, s_ref, tail_ref, *, n_heads):
    w = n_heads * HEAD_DIM
    t_len = in_ref.shape[0]
    c_len = GDN_CHUNK
    neg_inf = -jnp.inf

    xin = in_ref[...].astype(F32)
    raw = xin[:, :3 * w]
    z = xin[:, 3 * w:]
    prev = tail_ref[...]
    cw = cw_ref[...]
    acc = raw * cw[GDN_CONV - 1:GDN_CONV, :]
    for kk in range(1, GDN_CONV):
        acc = acc + _shift_rows(raw, prev, kk) * cw[GDN_CONV - 1 - kk:GDN_CONV - kk, :]
    tail_ref[...] = raw[t_len - SUBLANES:, :]
    y = _silu(acc)
    q = y[:, :w]
    k = y[:, w:2 * w]
    v = y[:, 2 * w:]
    same = same_ref[...]
    same_b = _bf(same)
    qn = q * lax.rsqrt(_mm2_l(q * q, same_b) + EPS) * (HEAD_DIM ** -0.5)
    kn = k * lax.rsqrt(_mm2_l(k * k, same_b) + EPS)

    sc = scol_ref[...]
    beta_b = _sigmoid(_mm2_l(sc, selb_ref[...]))
    ld_b = -jnp.exp(alog_b_ref[...]) * _softplus(_mm2_l(sc, sela_ref[...]) + dtb_b_ref[...])
    g_b = _mm2_r(lblk_ref[...], ld_b)
    sr = srow_ref[...]
    ld_r = -jnp.exp(alog_r_ref[...]) * _softplus(sr + dtb_r_ref[...])
    g_r = _mm2_l(ld_r, ublk_ref[...])
    rem_r = _mm2_l(ld_r, _bf(mstrict_ref[...]))

    eye = eye_ref[...]
    k_t = _mm_nt(eye, kn)
    rem_rows = jnp.concatenate(
        [jnp.broadcast_to(rem_r[n_heads + h:n_heads + h + 1, :], (HEAD_DIM, t_len)) for h in range(n_heads)],
        axis=0)
    kdec_t = k_t * jnp.exp(rem_rows)
    eg = jnp.exp(g_b)
    qdec = qn * eg
    wrhs = kn * beta_b * eg
    urhs = v * beta_b
    mstrict = mstrict_ref[...]
    mincl = mincl_ref[...]
    eye_f = eye.astype(F32)
    u = jnp.zeros((t_len, w), F32)
    wv = jnp.zeros((t_len, w), F32)
    heads = range(n_heads)
    hms = [hmask_ref[h] for h in heads]
    decs = [jnp.exp(jnp.where(mincl > 0.0, g_b[:, h * HEAD_DIM:h * HEAD_DIM + 1]
                              - g_r[n_heads + h:n_heads + h + 1, :], neg_inf)) for h in heads]
    kks = [_mm_nt(kn * hms[h], kn) for h in heads]
    ns = [jnp.where(mstrict > 0.0, beta_b[:, h * HEAD_DIM:h * HEAD_DIM + 1] * kks[h] * decs[h], 0.0)
          for h in heads]
    t_invs = _tri_inverse(ns, eye_f)
    for h in heads:
        for part in _split(t_invs[h]):
            u = u + _mm(part, urhs * hms[h])
            wv = wv + _mm(part, wrhs * hms[h])
    qks = [_mm_nt(qn * hms[h], kn) * decs[h] for h in heads]

    state = s_ref[...]
    vnews = []
    ointer = []
    for c in range(t_len // c_len):
        rc = slice(c * c_len, (c + 1) * c_len)
        vnew = u[rc] - _mm(wv[rc], state)
        vnews.append(vnew)
        ointer.append(_mm(qdec[rc], state))
        gl = eg[(c + 1) * c_len - 1:(c + 1) * c_len, :]
        state = state * gl + _mm(kdec_t[:, rc], vnew) * same
    s_ref[...] = state
    vn = jnp.concatenate(vnews, axis=0)
    o = jnp.concatenate(ointer, axis=0)
    for h in range(n_heads):
        o = o + _mm(qks[h], vn * hmask_ref[h])
    ms = _mm2_l(o * o, same_b) * (1.0 / HEAD_DIM)
    yo = o * lax.rsqrt(ms + EPS) * gn_ref[...] * _silu(z)
    out_ref[...] = yo.astype(out_ref.dtype)


def _gdn(gdn_in, scol, srow, conv_w, a_log, dt_bias, gdn_norm, batch, seq, t_len):
    n_heads = GDN_HEADS
    w = n_heads * HEAD_DIM
    m = batch * seq
    nt = seq // t_len
    chunk = np.arange(t_len) // GDN_CHUNK
    pos = np.arange(t_len)
    samec = chunk[:, None] == chunk[None, :]
    mincl = (samec & (pos[None, :] <= pos[:, None])).astype(np.float32)
    mstrict = (samec & (pos[None, :] < pos[:, None])).astype(np.float32)
    same, hmask = _head_consts(n_heads)
    head = np.arange(w) // HEAD_DIM
    selb = np.zeros((LANES, w), np.float32)
    sela = np.zeros((LANES, w), np.float32)
    selb[head, np.arange(w)] = 1.0
    sela[n_heads + head, np.arange(w)] = 1.0
    alog_b = jnp.repeat(a_log.astype(F32), HEAD_DIM)[None, :]
    dtb_b = jnp.repeat(dt_bias.astype(F32), HEAD_DIM)[None, :]
    alog_r = jnp.zeros((BF16_ROWS, 1), F32).at[n_heads:2 * n_heads, 0].set(a_log.astype(F32))
    dtb_r = jnp.zeros((BF16_ROWS, 1), F32).at[n_heads:2 * n_heads, 0].set(dt_bias.astype(F32))
    gn = jnp.tile(gdn_norm.astype(F32), n_heads)[None, :]
    kern = functools.partial(_gdn_kernel, n_heads=n_heads)
    g = 1
    srow_b = srow.reshape(BF16_ROWS, batch, seq).transpose(1, 0, 2)
    out = pl.pallas_call(
        kern,
        out_shape=jax.ShapeDtypeStruct((batch, seq, w), BF16),
        grid=(batch // g, nt),
        in_specs=[
            pl.BlockSpec((g, t_len, 4 * w), lambda b, t: (b, t, 0)),
            pl.BlockSpec((g, t_len, LANES), lambda b, t: (b, t, 0)),
            pl.BlockSpec((g, BF16_ROWS, t_len), lambda b, t: (b, 0, t)),
            _const_spec((GDN_CONV, 3 * w)),
            _const_spec((1, w)),
            _const_spec((1, w)),
            _const_spec((BF16_ROWS, 1)),
            _const_spec((BF16_ROWS, 1)),
            _const_spec((1, w)),
            _const_spec((t_len, t_len)),
            _const_spec((t_len, t_len)),
            _const_spec((t_len, t_len)),
            _const_spec((t_len, t_len)),
            _const_spec((n_heads, 1, w)),
            _const_spec((w, w)),
            _const_spec((w, w)),
            _const_spec((LANES, w)),
            _const_spec((LANES, w)),
        ],
        out_specs=pl.BlockSpec((g, t_len, w), lambda b, t: (b, t, 0)),
        scratch_shapes=[pltpu.VMEM((g, w, w), F32), pltpu.VMEM((g, SUBLANES, 3 * w), F32)],
        compiler_params=_params("parallel", "arbitrary"),
        name="gdn",
    )(gdn_in.reshape(batch, seq, 4 * w), scol.reshape(batch, seq, LANES), srow_b, conv_w.astype(F32),
      alog_b, dtb_b, alog_r, dtb_r, gn,
      jnp.asarray(mincl, BF16), jnp.asarray(mincl.T, BF16), jnp.asarray(mstrict), jnp.asarray(mincl),
      jnp.asarray(hmask), jnp.asarray(same), jnp.eye(w, dtype=BF16), jnp.asarray(sela, BF16),
      jnp.asarray(selb, BF16))
    return out.reshape(m, w)


def _outproj_kernel(yr_ref, ym_ref, yg_ref, wr_ref, wm_ref, wg_ref, x_ref, out_ref):
    d = functools.partial(jnp.dot, preferred_element_type=F32)
    out_ref[...] = x_ref[...] + (d(yr_ref[...], wr_ref[...]) + d(ym_ref[...], wm_ref[...])
                                 + d(yg_ref[...], wg_ref[...]))


def _outproj(y_ret, y_moba, y_gdn, w_out, x, tm):
    m, d = x.shape
    wb = _bf(w_out)
    wr = wb[:RET_W]
    wm = wb[RET_W:RET_W + MOBA_W]
    wg = wb[RET_W + MOBA_W:]
    return pl.pallas_call(
        _outproj_kernel,
        out_shape=jax.ShapeDtypeStruct((m, d), F32),
        grid=(m // tm,),
        in_specs=[
            pl.BlockSpec((tm, RET_W), lambda i: (i, 0)),
            pl.BlockSpec((tm, MOBA_W), lambda i: (i, 0)),
            pl.BlockSpec((tm, GDN_W), lambda i: (i, 0)),
            _const_spec((RET_W, d)),
            _const_spec((MOBA_W, d)),
            _const_spec((GDN_W, d)),
            pl.BlockSpec((tm, d), lambda i: (i, 0)),
        ],
        out_specs=pl.BlockSpec((tm, d), lambda i: (i, 0)),
        compiler_params=_params("parallel"),
        name="outproj",
    )(y_ret, y_moba, y_gdn, wr, wm, wg, x)


def _memkv_kernel(mem_ref, nw_ref, wkv_ref, kg_ref, k_ref, v_ref):
    cw = CROSS_HEADS * CROSS_HEAD_DIM
    h = _bf(_rms_rows(mem_ref[...], nw_ref[...]))
    kv = jnp.dot(h, wkv_ref[...], preferred_element_type=F32)
    for hd in range(CROSS_HEADS):
        sl = slice(hd * CROSS_HEAD_DIM, (hd + 1) * CROSS_HEAD_DIM)
        k_ref[:, sl] = _rms_rows(kv[:, sl], kg_ref[...]).astype(k_ref.dtype)
    v_ref[...] = kv[:, cw:].astype(v_ref.dtype)


def _memkv(mem, norm_mem, wkv, k_norm, tm):
    m, d = mem.shape
    cw = CROSS_HEADS * CROSS_HEAD_DIM
    return pl.pallas_call(
        _memkv_kernel,
        out_shape=(jax.ShapeDtypeStruct((m, cw), BF16), jax.ShapeDtypeStruct((m, cw), BF16)),
        grid=(m // tm,),
        in_specs=[
            pl.BlockSpec((tm, d), lambda i: (i, 0)),
            _const_spec((1, d)),
            _const_spec((d, 2 * cw)),
            _const_spec((1, CROSS_HEAD_DIM)),
        ],
        out_specs=(pl.BlockSpec((tm, cw), lambda i: (i, 0)), pl.BlockSpec((tm, cw), lambda i: (i, 0))),
        compiler_params=_params("parallel"),
        name="memkv",
    )(mem, norm_mem.reshape(1, d), _bf(wkv), k_norm.reshape(1, CROSS_HEAD_DIM).astype(F32))


def _cross_kernel(x_ref, nw_ref, wq_ref, qg_ref, k_ref, v_ref, wo_ref, out_ref):
    x = x_ref[...]
    h = _bf(_rms_rows(x, nw_ref[...]))
    q = jnp.dot(h, wq_ref[...], preferred_element_type=F32)
    scale = CROSS_HEAD_DIM ** -0.5
    sls = [slice(hd * CROSS_HEAD_DIM, (hd + 1) * CROSS_HEAD_DIM) for hd in range(CROSS_HEADS)]
    qhs = [_rms_rows(q[:, sl], qg_ref[...]) for sl in sls]
    ss = [_mm_nt(qh, k_ref[:, sl]) * scale for qh, sl in zip(qhs, sls)]
    ps = [jnp.exp(s - jnp.max(s, axis=-1, keepdims=True)) for s in ss]
    ls = [jnp.sum(p, axis=-1, keepdims=True) for p in ps]
    outs = [_mm(p, v_ref[:, sl]) / l for p, sl, l in zip(ps, sls, ls)]
    o = _bf(jnp.concatenate(outs, axis=-1))
    out_ref[...] = x + jnp.dot(o, wo_ref[...], preferred_element_type=F32)


def _cross(x, norm_w, wq, q_norm, kn, v, wo, batch, seq, mem_len, tm):
    m, d = x.shape
    cw = CROSS_HEADS * CROSS_HEAD_DIM
    nt = seq // tm
    return pl.pallas_call(
        _cross_kernel,
        out_shape=jax.ShapeDtypeStruct((m, d), F32),
        grid=(batch, nt),
        in_specs=[
            pl.BlockSpec((tm, d), lambda b, t: (b * nt + t, 0)),
            _const_spec((1, d)),
            _const_spec((d, cw)),
            _const_spec((1, CROSS_HEAD_DIM)),
            pl.BlockSpec((mem_len, cw), lambda b, t: (b, 0)),
            pl.BlockSpec((mem_len, cw), lambda b, t: (b, 0)),
            _const_spec((cw, d)),
        ],
        out_specs=pl.BlockSpec((tm, d), lambda b, t: (b * nt + t, 0)),
        compiler_params=_params("parallel", "parallel"),
        name="cross",
    )(x, norm_w.reshape(1, d), _bf(wq), q_norm.reshape(1, CROSS_HEAD_DIM).astype(F32), kn, v, _bf(wo))


def _ffn_kernel(x_ref, xp_ref, nw_ref, wg_ref, wv_ref, cg_ref, cv_ref, bg_ref, bv_ref, wd_ref, out_ref,
                *, tiles_per_seq, n_chunks):
    x = x_ref[...]
    nw = nw_ref[...]
    h = _bf(_rms_rows(x, nw))
    keep = jnp.where(pl.program_id(0) % tiles_per_seq == 0, 0.0, 1.0)
    hp = _bf(_rms_rows(xp_ref[...], nw))
    out_ref[...] = x

    def conv(u, up, cw, b):
        y = u * cw[FFN_CONV - 1:FFN_CONV, :] + b
        for kk in range(1, FFN_CONV):
            y = y + _shift_rows(u, up, kk) * cw[FFN_CONV - 1 - kk:FFN_CONV - kk, :]
        return y

    d = functools.partial(jnp.dot, preferred_element_type=F32)

    def up(c):
        wg = wg_ref[c]
        wv = wv_ref[c]
        return d(h, wg), d(hp, wg) * keep, d(h, wv), d(hp, wv) * keep

    nxt = up(0)
    for c in range(n_chunks):
        ug, ugp, uv, uvp = nxt
        if c + 1 < n_chunks:
            nxt = up(c + 1)
        gate = conv(ug, ugp, cg_ref[c], bg_ref[c])
        val = conv(uv, uvp, cv_ref[c], bv_ref[c])
        out_ref[...] += d(_bf(_silu(gate) * val), wd_ref[c])


def _ffn(x, norm_w, w_up, conv_w, conv_b, w_down, seq, tm, fc):
    m, d = x.shape
    d_ff = w_down.shape[0]
    nc = d_ff // fc
    halo = BF16_ROWS

    def chunks(a):
        return a.reshape(a.shape[0], nc, fc).transpose(1, 0, 2)

    wg = chunks(_bf(w_up[:, :d_ff]))
    wv = chunks(_bf(w_up[:, d_ff:]))
    cg = chunks(conv_w[:, :d_ff].astype(F32))
    cv = chunks(conv_w[:, d_ff:].astype(F32))
    bg = chunks(conv_b[None, :d_ff].astype(F32))
    bv = chunks(conv_b[None, d_ff:].astype(F32))
    wd = _bf(w_down).reshape(nc, fc, d)
    kern = functools.partial(_ffn_kernel, tiles_per_seq=seq // tm, n_chunks=nc)
    return pl.pallas_call(
        kern,
        out_shape=jax.ShapeDtypeStruct((m, d), F32),
        grid=(m // tm,),
        in_specs=[
            pl.BlockSpec((tm, d), lambda i: (i, 0)),
            pl.BlockSpec((halo, d), lambda i: (jnp.maximum(i * (tm // halo) - 1, 0), 0)),
            _const_spec((1, d)),
            _const_spec((nc, d, fc)),
            _const_spec((nc, d, fc)),
            _const_spec((nc, FFN_CONV, fc)),
            _const_spec((nc, FFN_CONV, fc)),
            _const_spec((nc, 1, fc)),
            _const_spec((nc, 1, fc)),
            _const_spec((nc, fc, d)),
        ],
        out_specs=pl.BlockSpec((tm, d), lambda i: (i, 0)),
        compiler_params=_params("parallel"),
        name="ffn",
    )(x, x, norm_w.reshape(1, d), wg, wv, cg, cv, bg, bv, wd)


def _tiles(seq):
    tm = min(512, seq)
    t_len = min(256, seq)
    return tm, t_len


def kernel(x, mem, norm_mix, w_in, ret_norm, moba_q_norm, moba_k_norm, gdn_conv, gdn_a_log, gdn_dt_bias,
           gdn_norm, w_out, norm_cross, norm_mem, cross_wq, cross_wkv, cross_q_norm, cross_k_norm, cross_wo,
           norm_ffn, ffn_up, ffn_conv, ffn_conv_b, ffn_down, rel_bias):
    batch, seq, d = x.shape
    mem_len = mem.shape[1]
    depth = w_in.shape[0]
    tm, t_len = _tiles(seq)
    xf = x.reshape(batch * seq, d)
    memf = mem.reshape(batch * mem_len, d)
    for l in range(depth):
        ret_in, moba_in, gdn_in, scol, srow = _inproj(xf, norm_mix[l], w_in[l], tm)
        y_ret = _retention(ret_in, ret_norm[l], batch, seq, t_len)
        y_moba = _moba(moba_in, moba_q_norm[l], moba_k_norm[l], rel_bias, batch, seq)
        y_gdn = _gdn(gdn_in, scol, srow, gdn_conv[l], gdn_a_log[l], gdn_dt_bias[l], gdn_norm[l],
                     batch, seq, t_len)
        xf = _outproj(y_ret, y_moba, y_gdn, w_out[l], xf, tm)
        kn, v = _memkv(memf, norm_mem[l], cross_wkv[l], cross_k_norm[l], mem_len)
        xf = _cross(xf, norm_cross[l], cross_wq[l], cross_q_norm[l], kn, v, cross_wo[l],
                    batch, seq, mem_len, tm)
        xf = _ffn(xf, norm_ffn[l], ffn_up[l], ffn_conv[l], ffn_conv_b[l], ffn_down[l], seq, tm, 256)
    return xf.reshape(batch, seq, d)
```

```python
import functools
import math

import numpy as np
import jax
import jax.numpy as jnp
from jax import lax
from jax.experimental import pallas as pl
from jax.experimental.pallas import tpu as pltpu

F32 = jnp.float32
BF16 = jnp.bfloat16

HEAD_DIM = 64
RET_HEADS = 4
MOBA_HEADS = 8
GDN_HEADS = 4
RET_W = RET_HEADS * HEAD_DIM
MOBA_W = MOBA_HEADS * HEAD_DIM
GDN_W = GDN_HEADS * HEAD_DIM
ROPE_BASE = 10000.0
MOBA_BLOCK = 256
MOBA_TOPK = 3
REL_BUCKETS = 32
REL_MAX_DIST = 128
GDN_CHUNK = 64
GDN_CONV = 4
CROSS_HEADS = 4
CROSS_HEAD_DIM = 128
FFN_CONV = 3
EPS = 1e-6

LANES = 128
SUBLANES = 8
BF16_ROWS = 16
VMEM_LIMIT = 56 * 1024 * 1024


def _bf(a):
    return a.astype(BF16)


def _mm(a, b):
    return jnp.dot(_bf(a), _bf(b), preferred_element_type=F32)


def _mm_nt(a, b):
    return lax.dot_general(_bf(a), _bf(b), (((1,), (1,)), ((), ())), preferred_element_type=F32)


def _split(a):
    hi = a.astype(BF16)
    lo = (a - hi.astype(F32)).astype(BF16)
    return hi, lo


def _mm3(a, b):
    ah, al = _split(a)
    bh, bl = _split(b)
    d = functools.partial(jnp.dot, preferred_element_type=F32)
    return d(ah, bh) + (d(ah, bl) + d(al, bh))


def _mm3_nt(a, b):
    ah, al = _split(a)
    bh, bl = _split(b)
    d = functools.partial(lax.dot_general, dimension_numbers=(((1,), (1,)), ((), ())),
                          preferred_element_type=F32)
    return d(ah, bh) + (d(ah, bl) + d(al, bh))


def _mm2_l(a, b_exact):
    ah, al = _split(a)
    d = functools.partial(jnp.dot, preferred_element_type=F32)
    return d(ah, b_exact) + d(al, b_exact)


def _mm2_r(a_exact, b):
    bh, bl = _split(b)
    d = functools.partial(jnp.dot, preferred_element_type=F32)
    return d(a_exact, bh) + d(a_exact, bl)


def _sigmoid(x):
    return 1.0 / (1.0 + jnp.exp(-x))


def _silu(x):
    return x * _sigmoid(x)


def _softplus(x):
    return jnp.maximum(x, 0.0) + jnp.log1p(jnp.exp(-jnp.abs(x)))


def _rms_rows(x, w):
    return x * lax.rsqrt(jnp.mean(x * x, axis=-1, keepdims=True) + EPS) * w


def _shift_rows(u, prev, k):
    r = pltpu.roll(u, k, axis=0)
    rp = pltpu.roll(prev, k, axis=0)[:SUBLANES]
    row = lax.broadcasted_iota(jnp.int32, (SUBLANES, u.shape[1]), 0)
    top = jnp.where(row < k, rp, r[:SUBLANES])
    return jnp.concatenate([top, r[SUBLANES:]], axis=0)


def _const_spec(shape):
    nd = len(shape)
    return pl.BlockSpec(shape, lambda *_: (0,) * nd)


def _params(*sem):
    return pltpu.CompilerParams(dimension_semantics=sem, vmem_limit_bytes=VMEM_LIMIT)


def _inproj_kernel(x_ref, nw_ref, w_ref, ws_ref, ret_ref, moba_ref, gdn_ref, scol_ref):
    x = x_ref[...]
    hf = _rms_rows(x, nw_ref[...])
    h = _bf(hf)
    off = 0
    for ref in (ret_ref, moba_ref, gdn_ref):
        width = ref.shape[1]
        for c in range(0, width, 512):
            ref[:, c:c + 512] = jnp.dot(h, w_ref[:, off + c:off + c + 512],
                                        preferred_element_type=F32).astype(ref.dtype)
        off += width
    scol_ref[...] = _mm3(hf, ws_ref[...])


def _inproj(x, norm_w, w_in, tm):
    m, d = x.shape
    main = 4 * RET_W + 3 * MOBA_W + 4 * GDN_W
    w_main = _bf(w_in[:, :main])
    w_small = jnp.zeros((d, LANES), F32).at[:, :2 * GDN_HEADS].set(w_in[:, main:])
    out_shape = (
        jax.ShapeDtypeStruct((m, 4 * RET_W), BF16),
        jax.ShapeDtypeStruct((m, 3 * MOBA_W), BF16),
        jax.ShapeDtypeStruct((m, 4 * GDN_W), BF16),
        jax.ShapeDtypeStruct((m, LANES), F32),
    )
    return pl.pallas_call(
        _inproj_kernel,
        out_shape=out_shape,
        grid=(m // tm,),
        in_specs=[
            pl.BlockSpec((tm, d), lambda i: (i, 0)),
            _const_spec((1, d)),
            _const_spec((d, main)),
            _const_spec((d, LANES)),
        ],
        out_specs=(
            pl.BlockSpec((tm, 4 * RET_W), lambda i: (i, 0)),
            pl.BlockSpec((tm, 3 * MOBA_W), lambda i: (i, 0)),
            pl.BlockSpec((tm, 4 * GDN_W), lambda i: (i, 0)),
            pl.BlockSpec((tm, LANES), lambda i: (i, 0)),
        ),
        compiler_params=_params("parallel"),
        name="inproj",
    )(x, norm_w.reshape(1, d), w_main, w_small)


def _head_consts(n_heads):
    w = n_heads * HEAD_DIM
    head = np.arange(w) // HEAD_DIM
    same = (head[:, None] == head[None, :]).astype(np.float32)
    hmask = (head[None, None, :] == np.arange(n_heads)[:, None, None]).astype(np.float32)
    return same, hmask


def _ret_kernel(in_ref, cos_ref, sin_ref, gn_ref, rot_ref, eye_ref, dmat_ref, xi_ref, zeta_ref, gt_ref,
                hmask_ref, same_ref, out_ref, s_ref, *, n_heads):
    w = n_heads * HEAD_DIM
    t_len = in_ref.shape[0]

    @pl.when(pl.program_id(1) == 0)
    def _():
        s_ref[...] = jnp.zeros_like(s_ref)

    xin = in_ref[...]
    q = xin[:, :w]
    k = xin[:, w:2 * w]
    v = xin[:, 2 * w:3 * w]
    g = xin[:, 3 * w:].astype(F32)
    cos = cos_ref[...]
    sin = sin_ref[...]
    rot = rot_ref[...]
    qr = q.astype(F32) * cos + jnp.dot(q, rot, preferred_element_type=F32) * sin
    kr = k.astype(F32) * cos + jnp.dot(k, rot, preferred_element_type=F32) * sin
    same = same_ref[...]
    k_t = _mm_nt(eye_ref[...], kr)
    state = s_ref[...]
    o = _mm(qr, state) * xi_ref[...]
    vf = v.astype(F32)
    hms = [hmask_ref[h] for h in range(n_heads)]
    ss = [_mm_nt(qr * hm, kr) * dmat_ref[h] for h, hm in enumerate(hms)]
    for s, hm in zip(ss, hms):
        o = o + _mm(s, vf * hm)
    zr = jnp.concatenate(
        [jnp.broadcast_to(zeta_ref[h:h + 1, :], (HEAD_DIM, t_len)) for h in range(n_heads)], axis=0)
    s_ref[...] = state * gt_ref[...] + _mm(k_t * zr, v) * same
    ms = _mm2_l(o * o, _bf(same)) * (1.0 / HEAD_DIM)
    y = o * lax.rsqrt(ms + EPS) * gn_ref[...] * _silu(g)
    out_ref[...] = y.astype(out_ref.dtype)


def _retention(ret_in, ret_norm, batch, seq, t_len):
    n_heads = RET_HEADS
    w = n_heads * HEAD_DIM
    m = batch * seq
    nt = seq // t_len
    half = HEAD_DIM // 2
    pos = np.arange(seq, dtype=np.float32)
    inv_freq = (ROPE_BASE ** (-jnp.arange(half, dtype=F32) / half))
    ang = jnp.asarray(pos)[:, None] * inv_freq[None, :]
    cos = jnp.tile(jnp.concatenate([jnp.cos(ang)] * 2, axis=-1), (1, n_heads))
    sin = jnp.tile(jnp.concatenate([jnp.sin(ang)] * 2, axis=-1), (1, n_heads))
    j = np.arange(w)
    rot = np.zeros((w, w), np.float32)
    first = (j % HEAD_DIM) < half
    rot[j[first] + half, j[first]] = -1.0
    rot[j[~first] - half, j[~first]] = 1.0
    same, hmask = _head_consts(n_heads)
    log_gamma = jnp.log1p(-jnp.exp2(-5.0 - jnp.arange(n_heads, dtype=F32)))
    idx = jnp.arange(t_len, dtype=F32)
    diff = idx[:, None] - idx[None, :]
    scale = HEAD_DIM ** -0.5
    dmat = jnp.where(diff >= 0, jnp.exp(log_gamma[:, None, None] * jnp.maximum(diff, 0.0)), 0.0) * scale
    xi = jnp.repeat(jnp.exp(log_gamma[:, None] * (idx + 1.0)).T, HEAD_DIM, axis=1)
    zeta = jnp.exp(log_gamma[:, None] * (t_len - 1.0 - idx)) * scale
    zeta = jnp.zeros((SUBLANES, t_len), F32).at[:n_heads].set(zeta)
    g_chunk = jnp.repeat(jnp.exp(log_gamma * t_len), HEAD_DIM)[None, :]
    gn = jnp.tile(ret_norm.astype(F32), n_heads)[None, :]
    kern = functools.partial(_ret_kernel, n_heads=n_heads)
    return pl.pallas_call(
        kern,
        out_shape=jax.ShapeDtypeStruct((m, w), BF16),
        grid=(batch, nt),
        in_specs=[
            pl.BlockSpec((t_len, 4 * w), lambda b, t: (b * nt + t, 0)),
            pl.BlockSpec((t_len, w), lambda b, t: (t, 0)),
            pl.BlockSpec((t_len, w), lambda b, t: (t, 0)),
            _const_spec((1, w)),
            _const_spec((w, w)),
            _const_spec((w, w)),
            _const_spec((n_heads, t_len, t_len)),
            _const_spec((t_len, w)),
            _const_spec((SUBLANES, t_len)),
            _const_spec((1, w)),
            _const_spec((n_heads, 1, w)),
            _const_spec((w, w)),
        ],
        out_specs=pl.BlockSpec((t_len, w), lambda b, t: (b * nt + t, 0)),
        scratch_shapes=[pltpu.VMEM((w, w), F32)],
        compiler_params=_params("parallel", "arbitrary"),
        name="retention",
    )(ret_in, cos, sin, gn, jnp.asarray(rot, BF16), jnp.eye(w, dtype=BF16), dmat, xi, zeta, g_chunk,
      jnp.asarray(hmask), jnp.asarray(same))


def _t5_bucket(rel):
    n = np.maximum(rel, 0)
    exact = REL_BUCKETS // 2
    nf = np.maximum(n, exact).astype(np.float32)
    ratio = np.log(nf / np.float32(exact)) / np.float32(math.log(REL_MAX_DIST / exact))
    large = exact + (ratio * np.float32(REL_BUCKETS - exact)).astype(np.int32)
    return np.where(n < exact, n, np.minimum(large, REL_BUCKETS - 1)).astype(np.int32)


MOBA_MASK = -1e30
MOBA_EXP_LIMIT = 1e37


def _moba_kernel(q_ref, k_ref, v_ref, idown_ref, idprev_ref, rb_ref, qg_ref, kg_ref, same_ref,
                 eye_ref, out_ref, kaug_ref, vaug_ref, kmean_ref, bias_ref, *, n_blocks, far_bucket):
    blk = MOBA_BLOCK
    hd = HEAD_DIM
    qi = pl.program_id(2)
    head0 = 2 * pl.program_id(0)
    same = same_ref[...]
    inv_d = 1.0 / hd
    scale = hd ** -0.5
    lane = lax.broadcasted_iota(jnp.int32, (1, 2 * hd), 1)
    in_h0 = lane < hd
    both = range(2)
    mine = [in_h0, jnp.logical_not(in_h0)]
    blk_row = lax.broadcasted_iota(jnp.int32, (n_blocks, blk), 0)

    def augmented_queries(q_blk, j):
        qf = q_blk.astype(F32)
        qn = qf * lax.rsqrt(_mm2_l(qf * qf, same) * inv_d + EPS) * qg_ref[...]
        km = kmean_ref[...]
        gates = [jnp.where(blk_row < j, _mm3_nt(jnp.where(mine[e], km, 0.0), qn), -jnp.inf) for e in both]
        ranks = [jnp.zeros_like(g) for g in gates]
        for i in range(n_blocks):
            tie = jnp.where(blk_row > i, 1.0, 0.0)
            ranks = [r + jnp.where(g[i:i + 1, :] > g, 1.0, jnp.where(g[i:i + 1, :] == g, tie, 0.0))
                     for r, g in zip(ranks, gates)]
        mask_t = [jnp.where(((blk_row < j) & (r < MOBA_TOPK)) | (blk_row == j), 0.0, MOBA_MASK)
                  for r in ranks]
        pad = jnp.zeros((hd - n_blocks - SUBLANES, blk), F32)
        zq = jnp.zeros((hd, blk), F32)
        extras = []
        for e in both:
            far = jnp.full((1, blk), rb_ref[head0 + e, far_bucket], F32)
            far_hi = far.astype(BF16).astype(F32)
            crow = jnp.concatenate([far_hi, far - far_hi, jnp.zeros((SUBLANES - 2, blk), F32)], axis=0)
            extras.append([zq, mask_t[e], crow, pad] if e == 0 else [mask_t[e], crow, pad, zq])
        extras = [_mm_nt(eye_ref[...], jnp.concatenate(parts, axis=0)) for parts in extras]
        return [jnp.where(mine[e], qn * scale, 0.0) + extras[e] for e in both]

    @pl.when((qi == 0) & (pl.program_id(1) == 0))
    def _():
        for e in both:
            for tab, ids_ref in enumerate((idown_ref, idprev_ref)):
                ids = ids_ref[...]
                tile = jnp.zeros((blk, blk), F32)
                for b in range(REL_BUCKETS):
                    tile = jnp.where(ids == b, rb_ref[head0 + e, b], tile)
                bias_ref[tab, e] = tile - rb_ref[head0 + e, far_bucket]

    @pl.when(qi == 0)
    def _():
        def body(j, c):
            r0 = pl.multiple_of(j * blk, blk)
            kb = k_ref[pl.ds(r0, blk), :].astype(F32)
            kn = kb * lax.rsqrt(_mm2_l(kb * kb, same) * inv_d + EPS) * kg_ref[...]
            kmean_ref[pl.ds(j, 1), :] = jnp.mean(kn, axis=0, keepdims=True)
            vb = v_ref[pl.ds(r0, blk), :].astype(F32)
            for e in both:
                xo = (1 - e) * hd
                ones = (lane >= xo + n_blocks) & (lane < xo + n_blocks + 3)
                ext = jnp.where((lane == xo + j) | ones, 1.0, 0.0)
                kaug_ref[e, pl.ds(r0, blk), :] = jnp.where(mine[e], kn, ext).astype(BF16)
                vaug_ref[e, pl.ds(r0, blk), :] = jnp.where(mine[e], vb, 1.0).astype(BF16)
            return c
        lax.fori_loop(0, n_blocks, body, 0)

    row = lax.broadcasted_iota(jnp.int32, (blk, blk), 0)
    col = lax.broadcasted_iota(jnp.int32, (blk, blk), 1)
    q0 = pl.multiple_of(qi * blk, blk)
    qp = pl.multiple_of(jnp.maximum(qi - 1, 0) * blk, blk)
    qbase = augmented_queries(q_ref[...], qi)
    bown = [bias_ref[0, e] for e in both]
    bprev = [bias_ref[1, e] for e in both]

    def scores(qa, e, r0, bias=None):
        s = _mm_nt(qa, kaug_ref[e, pl.ds(r0, blk), :])
        return s if bias is None else s + bias

    def finish(accs):
        o = [acc / pltpu.roll(acc, hd, axis=1) for acc in accs]
        return jnp.where(in_h0, o[0], o[1])

    k_own = [kaug_ref[e, pl.ds(q0, blk), :] for e in both]
    shifts = [jnp.sum(jnp.where(mine[e], qbase[e] * k_own[e].astype(F32), 0.0), axis=-1, keepdims=True)
              .astype(BF16).astype(F32) for e in both]
    q_shift = [_bf(qbase[e] + jnp.where(lane == (1 - e) * hd + n_blocks + 2, -shifts[e], 0.0)) for e in both]

    def fast_tiles(accs, starts):
        ss = [[scores(q_shift[e], e, r0) for e in both] for r0 in starts]
        for s, r0 in zip(ss, starts):
            accs = tuple(accs[e] + _mm(jnp.exp(s[e]), vaug_ref[e, pl.ds(r0, blk), :]) for e in both)
        return accs

    def fast_tile(accs, r0):
        return fast_tiles(accs, [r0])

    no_prev = jnp.where(qi >= 1, 0.0, MOBA_MASK)
    s_own = [jnp.where(col <= row, scores(q_shift[e], e, q0, bown[e]), -jnp.inf) for e in both]
    s_prev = [scores(q_shift[e], e, qp, bprev[e]) + no_prev for e in both]
    accs = tuple(_mm(jnp.exp(s_own[e]), vaug_ref[e, pl.ds(q0, blk), :])
                 + _mm(jnp.exp(s_prev[e]), vaug_ref[e, pl.ds(qp, blk), :]) for e in both)
    n_far = jnp.maximum(qi - 1, 0)

    def starts(first, count):
        return [pl.multiple_of((first + u) * blk, blk) for u in range(count)]

    group = 4
    accs = lax.fori_loop(0, n_far // group, lambda t, a: fast_tiles(a, starts(group * t, group)), accs)
    done = (n_far // group) * group
    left = n_far - done
    accs = lax.cond(left >= 2, lambda a: fast_tiles(a, starts(done, 2)), lambda a: a, accs)
    last = jnp.maximum(n_far - 1, 0)
    accs = lax.cond(left % 2 == 1, lambda a: fast_tile(a, starts(last, 1)[0]), lambda a: a, accs)

    unsafe = sum(jnp.sum(jnp.where(jnp.abs(acc) < MOBA_EXP_LIMIT, 0.0, 1.0)) for acc in accs)
    overflowed = unsafe > 0.0

    def safe_path(_):
        def update(carry, s, vaug):
            m_i, acc = carry
            m_n = jnp.maximum(m_i, jnp.max(s, axis=-1, keepdims=True))
            return m_n, jnp.exp(m_i - m_n) * acc + _mm(jnp.exp(s - m_n), vaug)

        def block(j, c):
            r0 = pl.multiple_of(j * blk, blk)
            out = []
            for e in range(2):
                s = scores(_bf(qbase[e]), e, r0) + jnp.where(j == qi - 1, bprev[e], 0.0)
                out.append(update(c[e], s, vaug_ref[e, pl.ds(r0, blk), :]))
            return tuple(out)

        init = []
        for e in range(2):
            s = jnp.where(col <= row, scores(_bf(qbase[e]), e, q0, bown[e]), -jnp.inf)
            m0 = jnp.max(s, axis=-1, keepdims=True)
            init.append((m0, _mm(jnp.exp(s - m0), vaug_ref[e, pl.ds(q0, blk), :])))
        c = lax.fori_loop(0, qi, block, tuple(init))
        return finish([c[0][1], c[1][1]])

    out = lax.cond(overflowed, safe_path, lambda _: finish(accs), 0)
    out_ref[...] = out.astype(out_ref.dtype)


def _moba_bucket_tables(seq):
    blk = MOBA_BLOCK
    d = np.arange(blk, dtype=np.int32)[:, None] - np.arange(blk, dtype=np.int32)[None, :]
    far = _t5_bucket(np.arange(blk + 1, max(seq, blk + 2), dtype=np.int32))
    assert (far == far[0]).all()
    return _t5_bucket(d), _t5_bucket(d + blk), int(far[0])


def _moba(moba_in, q_norm, k_norm, rel_bias, batch, seq):
    m = batch * seq
    blk = MOBA_BLOCK
    nb = seq // blk
    assert nb % SUBLANES == 0 and nb + SUBLANES <= HEAD_DIM
    pairs = MOBA_HEADS // 2
    pw = 2 * HEAD_DIM
    ids_own, ids_prev, far_bucket = _moba_bucket_tables(seq)
    same, _ = _head_consts(2)
    qg = jnp.tile(q_norm.astype(F32), 2)[None, :]
    kg = jnp.tile(k_norm.astype(F32), 2)[None, :]
    kern = functools.partial(_moba_kernel, n_blocks=nb, far_bucket=far_bucket)
    return pl.pallas_call(
        kern,
        out_shape=jax.ShapeDtypeStruct((m, MOBA_W), BF16),
        grid=(pairs, batch, nb),
        in_specs=[
            pl.BlockSpec((blk, pw), lambda p, b, i: (b * nb + i, p)),
            pl.BlockSpec((seq, pw), lambda p, b, i: (b, pairs + p)),
            pl.BlockSpec((seq, pw), lambda p, b, i: (b, 2 * pairs + p)),
            _const_spec((blk, blk)),
            _const_spec((blk, blk)),
            pl.BlockSpec(memory_space=pltpu.SMEM),
            _const_spec((1, pw)),
            _const_spec((1, pw)),
            _const_spec((pw, pw)),
            _const_spec((blk, blk)),
        ],
        out_specs=pl.BlockSpec((blk, pw), lambda p, b, i: (b * nb + i, p)),
        scratch_shapes=[pltpu.VMEM((2, seq, pw), BF16), pltpu.VMEM((2, seq, pw), BF16),
                        pltpu.VMEM((nb, pw), F32), pltpu.VMEM((2, 2, blk, blk), F32)],
        compiler_params=_params("arbitrary", "arbitrary", "arbitrary"),
        name="moba",
    )(moba_in, moba_in, moba_in, jnp.asarray(ids_own), jnp.asarray(ids_prev), rel_bias.astype(F32),
      qg, kg, jnp.asarray(same, BF16), jnp.eye(blk, dtype=BF16))


def _tri_inverse(ns, eye):
    xs = [eye - n for n in ns]
    ps = list(ns)
    steps = int(math.log2(GDN_CHUNK)) - 1
    for _ in range(steps):
        ps = [_mm(p, p) for p in ps]
        xs = [x + _mm(x, p) for x, p in zip(xs, ps)]
    rs = [eye - (x + _mm3(n, x)) for n, x in zip(ns, xs)]
    return [x + _mm(x, r) for x, r in zip(xs, rs)]


def _gdn_kernel(in_ref, scol_ref, *rest, n_heads):
    consts, (out_ref, s_ref, tail_ref) = rest[:-3], rest[-3:]

    @pl.when(pl.program_id(1) == 0)
    def _():
        s_ref[...] = jnp.zeros_like(s_ref)
        tail_ref[...] = jnp.zeros_like(tail_ref)

    for i in range(in_ref.shape[0]):
        _gdn_block(in_ref.at[i], scol_ref.at[i], *consts,
                   out_ref.at[i], s_ref.at[i], tail_ref.at[i], n_heads=n_heads)


def _gdn_block(in_ref, scol_ref, cw_ref, alog_b_ref, dtb_b_ref, alog_r_ref, dtb_r_ref, gn_ref,
               lblk_ref, ublk_ref, mstrict_ref, mincl_ref, hmask_ref, same_ref, eye_ref, sela_ref, selb_ref,
               selt_ref, out_ref, s_ref, tail_ref, *, n_heads):
    w = n_heads * HEAD_DIM
    t_len = in_ref.shape[0]
    c_len = GDN_CHUNK
    neg_inf = -jnp.inf

    xin = in_ref[...].astype(F32)
    raw = xin[:, :3 * w]
    z = xin[:, 3 * w:]
    prev = tail_ref[...]
    cw = cw_ref[...]
    acc = raw * cw[GDN_CONV - 1:GDN_CONV, :]
    for kk in range(1, GDN_CONV):
        acc = acc + _shift_rows(raw, prev, kk) * cw[GDN_CONV - 1 - kk:GDN_CONV - kk, :]
    tail_ref[...] = raw[t_len - SUBLANES:, :]
    y = _silu(acc)
    q = y[:, :w]
    k = y[:, w:2 * w]
    v = y[:, 2 * w:]
    same = same_ref[...]
    same_b = _bf(same)
    qn = q * lax.rsqrt(_mm2_l(q * q, same_b) + EPS) * (HEAD_DIM ** -0.5)
    kn = k * lax.rsqrt(_mm2_l(k * k, same_b) + EPS)

    sc = scol_ref[...]
    beta_b = _sigmoid(_mm2_l(sc, selb_ref[...]))
    ld_b = -jnp.exp(alog_b_ref[...]) * _softplus(_mm2_l(sc, sela_ref[...]) + dtb_b_ref[...])
    g_b = _mm2_r(lblk_ref[...], ld_b)
    sc_hi = sc.astype(BF16)
    sc_r = sc - sc_hi.astype(F32)
    sc_mid = sc_r.astype(BF16)
    sc_lo = (sc_r - sc_mid.astype(F32)).astype(BF16)
    sr = sum(_mm_nt(selt_ref[...], part) for part in (sc_hi, sc_mid, sc_lo))
    ld_r = -jnp.exp(alog_r_ref[...]) * _softplus(sr + dtb_r_ref[...])
    g_r = _mm2_l(ld_r, ublk_ref[...])
    rem_r = _mm2_l(ld_r, _bf(mstrict_ref[...]))

    eye = eye_ref[...]
    k_t = _mm_nt(eye, kn)
    rem_rows = jnp.concatenate(
        [jnp.broadcast_to(rem_r[n_heads + h:n_heads + h + 1, :], (HEAD_DIM, t_len)) for h in range(n_heads)],
        axis=0)
    kdec_t = k_t * jnp.exp(rem_rows)
    eg = jnp.exp(g_b)
    qdec = qn * eg
    wrhs = kn * beta_b * eg
    urhs = v * beta_b
    mstrict = mstrict_ref[...]
    mincl = mincl_ref[...]
    eye_f = eye.astype(F32)
    u = jnp.zeros((t_len, w), F32)
    wv = jnp.zeros((t_len, w), F32)
    heads = range(n_heads)
    hms = [hmask_ref[h] for h in heads]
    decs = [jnp.exp(jnp.where(mincl > 0.0, g_b[:, h * HEAD_DIM:h * HEAD_DIM + 1]
                              - g_r[n_heads + h:n_heads + h + 1, :], neg_inf)) for h in heads]
    kks = [_mm_nt(kn * hms[h], kn) for h in heads]
    ns = [jnp.where(mstrict > 0.0, beta_b[:, h * HEAD_DIM:h * HEAD_DIM + 1] * kks[h] * decs[h], 0.0)
          for h in heads]
    t_invs = _tri_inverse(ns, eye_f)
    for h in heads:
        for part in _split(t_invs[h]):
            u = u + _mm(part, urhs * hms[h])
            wv = wv + _mm(part, wrhs * hms[h])
    qks = [_mm_nt(qn * hms[h], kn) * decs[h] for h in heads]

    state = s_ref[...]
    vnews = []
    ointer = []
    for c in range(t_len // c_len):
        rc = slice(c * c_len, (c + 1) * c_len)
        from_state = _mm(jnp.concatenate([wv[rc], qdec[rc]], axis=0), state)
        vnew = u[rc] - from_state[:c_len]
        vnews.append(vnew)
        ointer.append(from_state[c_len:])
        gl = eg[(c + 1) * c_len - 1:(c + 1) * c_len, :]
        state = state * gl + _mm(kdec_t[:, rc], vnew) * same
    s_ref[...] = state
    vn = jnp.concatenate(vnews, axis=0)
    o = jnp.concatenate(ointer, axis=0)
    for h in range(n_heads):
        o = o + _mm(qks[h], vn * hmask_ref[h])
    ms = _mm2_l(o * o, same_b) * (1.0 / HEAD_DIM)
    yo = o * lax.rsqrt(ms + EPS) * gn_ref[...] * _silu(z)
    out_ref[...] = yo.astype(out_ref.dtype)


def _gdn(gdn_in, scol, conv_w, a_log, dt_bias, gdn_norm, batch, seq, t_len):
    n_heads = GDN_HEADS
    w = n_heads * HEAD_DIM
    m = batch * seq
    nt = seq // t_len
    chunk = np.arange(t_len) // GDN_CHUNK
    pos = np.arange(t_len)
    samec = chunk[:, None] == chunk[None, :]
    mincl = (samec & (pos[None, :] <= pos[:, None])).astype(np.float32)
    mstrict = (samec & (pos[None, :] < pos[:, None])).astype(np.float32)
    same, hmask = _head_consts(n_heads)
    head = np.arange(w) // HEAD_DIM
    selb = np.zeros((LANES, w), np.float32)
    sela = np.zeros((LANES, w), np.float32)
    selb[head, np.arange(w)] = 1.0
    sela[n_heads + head, np.arange(w)] = 1.0
    alog_b = jnp.repeat(a_log.astype(F32), HEAD_DIM)[None, :]
    dtb_b = jnp.repeat(dt_bias.astype(F32), HEAD_DIM)[None, :]
    alog_r = jnp.zeros((BF16_ROWS, 1), F32).at[n_heads:2 * n_heads, 0].set(a_log.astype(F32))
    dtb_r = jnp.zeros((BF16_ROWS, 1), F32).at[n_heads:2 * n_heads, 0].set(dt_bias.astype(F32))
    gn = jnp.tile(gdn_norm.astype(F32), n_heads)[None, :]
    kern = functools.partial(_gdn_kernel, n_heads=n_heads)
    g = 1
    selt = np.zeros((BF16_ROWS, LANES), np.float32)
    selt[np.arange(2 * n_heads), np.arange(2 * n_heads)] = 1.0
    out = pl.pallas_call(
        kern,
        out_shape=jax.ShapeDtypeStruct((batch, seq, w), BF16),
        grid=(batch // g, nt),
        in_specs=[
            pl.BlockSpec((g, t_len, 4 * w), lambda b, t: (b, t, 0)),
            pl.BlockSpec((g, t_len, LANES), lambda b, t: (b, t, 0)),
            _const_spec((GDN_CONV, 3 * w)),
            _const_spec((1, w)),
            _const_spec((1, w)),
            _const_spec((BF16_ROWS, 1)),
            _const_spec((BF16_ROWS, 1)),
            _const_spec((1, w)),
            _const_spec((t_len, t_len)),
            _const_spec((t_len, t_len)),
            _const_spec((t_len, t_len)),
            _const_spec((t_len, t_len)),
            _const_spec((n_heads, 1, w)),
            _const_spec((w, w)),
            _const_spec((w, w)),
            _const_spec((LANES, w)),
            _const_spec((LANES, w)),
            _const_spec((BF16_ROWS, LANES)),
        ],
        out_specs=pl.BlockSpec((g, t_len, w), lambda b, t: (b, t, 0)),
        scratch_shapes=[pltpu.VMEM((g, w, w), F32), pltpu.VMEM((g, SUBLANES, 3 * w), F32)],
        compiler_params=_params("parallel", "arbitrary"),
        name="gdn",
    )(gdn_in.reshape(batch, seq, 4 * w), scol.reshape(batch, seq, LANES), conv_w.astype(F32),
      alog_b, dtb_b, alog_r, dtb_r, gn,
      jnp.asarray(mincl, BF16), jnp.asarray(mincl.T, BF16), jnp.asarray(mstrict), jnp.asarray(mincl),
      jnp.asarray(hmask), jnp.asarray(same), jnp.eye(w, dtype=BF16), jnp.asarray(sela, BF16),
      jnp.asarray(selb, BF16), jnp.asarray(selt, BF16))
    return out.reshape(m, w)


def _outproj_kernel(yr_ref, ym_ref, yg_ref, wr_ref, wm_ref, wg_ref, x_ref, out_ref):
    d = functools.partial(jnp.dot, preferred_element_type=F32)
    out_ref[...] = x_ref[...] + (d(yr_ref[...], wr_ref[...]) + d(ym_ref[...], wm_ref[...])
                                 + d(yg_ref[...], wg_ref[...]))


def _outproj(y_ret, y_moba, y_gdn, w_out, x, tm):
    m, d = x.shape
    wb = _bf(w_out)
    wr = wb[:RET_W]
    wm = wb[RET_W:RET_W + MOBA_W]
    wg = wb[RET_W + MOBA_W:]
    return pl.pallas_call(
        _outproj_kernel,
        out_shape=jax.ShapeDtypeStruct((m, d), F32),
        grid=(m // tm,),
        in_specs=[
            pl.BlockSpec((tm, RET_W), lambda i: (i, 0)),
            pl.BlockSpec((tm, MOBA_W), lambda i: (i, 0)),
            pl.BlockSpec((tm, GDN_W), lambda i: (i, 0)),
            _const_spec((RET_W, d)),
            _const_spec((MOBA_W, d)),
            _const_spec((GDN_W, d)),
            pl.BlockSpec((tm, d), lambda i: (i, 0)),
        ],
        out_specs=pl.BlockSpec((tm, d), lambda i: (i, 0)),
        compiler_params=_params("parallel"),
        name="outproj",
    )(y_ret, y_moba, y_gdn, wr, wm, wg, x)


def _memkv_kernel(mem_ref, nw_ref, wkv_ref, kg_ref, k_ref, v_ref):
    cw = CROSS_HEADS * CROSS_HEAD_DIM
    h = _bf(_rms_rows(mem_ref[...], nw_ref[...]))
    kv = jnp.dot(h, wkv_ref[...], preferred_element_type=F32)
    for hd in range(CROSS_HEADS):
        sl = slice(hd * CROSS_HEAD_DIM, (hd + 1) * CROSS_HEAD_DIM)
        k_ref[:, sl] = _rms_rows(kv[:, sl], kg_ref[...]).astype(k_ref.dtype)
    v_ref[...] = kv[:, cw:].astype(v_ref.dtype)


def _memkv(mem, norm_mem, wkv, k_norm, tm):
    m, d = mem.shape
    cw = CROSS_HEADS * CROSS_HEAD_DIM
    return pl.pallas_call(
        _memkv_kernel,
        out_shape=(jax.ShapeDtypeStruct((m, cw), BF16), jax.ShapeDtypeStruct((m, cw), BF16)),
        grid=(m // tm,),
        in_specs=[
            pl.BlockSpec((tm, d), lambda i: (i, 0)),
            _const_spec((1, d)),
            _const_spec((d, 2 * cw)),
            _const_spec((1, CROSS_HEAD_DIM)),
        ],
        out_specs=(pl.BlockSpec((tm, cw), lambda i: (i, 0)), pl.BlockSpec((tm, cw), lambda i: (i, 0))),
        compiler_params=_params("parallel"),
        name="memkv",
    )(mem, norm_mem.reshape(1, d), _bf(wkv), k_norm.reshape(1, CROSS_HEAD_DIM).astype(F32))


def _cross_kernel(x_ref, nw_ref, wq_ref, qg_ref, k_ref, v_ref, wo_ref, out_ref):
    x = x_ref[...]
    h = _bf(_rms_rows(x, nw_ref[...]))
    q = jnp.dot(h, wq_ref[...], preferred_element_type=F32)
    scale = CROSS_HEAD_DIM ** -0.5
    sls = [slice(hd * CROSS_HEAD_DIM, (hd + 1) * CROSS_HEAD_DIM) for hd in range(CROSS_HEADS)]
    qhs = [_rms_rows(q[:, sl], qg_ref[...]) for sl in sls]
    ss = [_mm_nt(qh, k_ref[:, sl]) * scale for qh, sl in zip(qhs, sls)]
    ps = [jnp.exp(s - jnp.max(s, axis=-1, keepdims=True)) for s in ss]
    ls = [jnp.sum(p, axis=-1, keepdims=True) for p in ps]
    outs = [_mm(p, v_ref[:, sl]) / l for p, sl, l in zip(ps, sls, ls)]
    o = _bf(jnp.concatenate(outs, axis=-1))
    out_ref[...] = x + jnp.dot(o, wo_ref[...], preferred_element_type=F32)


def _cross(x, norm_w, wq, q_norm, kn, v, wo, batch, seq, mem_len, tm):
    m, d = x.shape
    cw = CROSS_HEADS * CROSS_HEAD_DIM
    nt = seq // tm
    return pl.pallas_call(
        _cross_kernel,
        out_shape=jax.ShapeDtypeStruct((m, d), F32),
        grid=(batch, nt),
        in_specs=[
            pl.BlockSpec((tm, d), lambda b, t: (b * nt + t, 0)),
            _const_spec((1, d)),
            _const_spec((d, cw)),
            _const_spec((1, CROSS_HEAD_DIM)),
            pl.BlockSpec((mem_len, cw), lambda b, t: (b, 0)),
            pl.BlockSpec((mem_len, cw), lambda b, t: (b, 0)),
            _const_spec((cw, d)),
        ],
        out_specs=pl.BlockSpec((tm, d), lambda b, t: (b * nt + t, 0)),
        compiler_params=_params("parallel", "parallel"),
        name="cross",
    )(x, norm_w.reshape(1, d), _bf(wq), q_norm.reshape(1, CROSS_HEAD_DIM).astype(F32), kn, v, _bf(wo))


def _ffn_kernel(x_ref, xp_ref, nw_ref, wg_ref, wv_ref, cg_ref, cv_ref, bg_ref, bv_ref, wd_ref, out_ref,
                *, tiles_per_seq, n_chunks):
    x = x_ref[...]
    nw = nw_ref[...]
    h = _bf(_rms_rows(x, nw))
    keep = jnp.where(pl.program_id(0) % tiles_per_seq == 0, 0.0, 1.0)
    hp = _bf(_rms_rows(xp_ref[...], nw))
    out_ref[...] = x

    def conv(u, up, cw, b):
        y = u * cw[FFN_CONV - 1:FFN_CONV, :] + b
        for kk in range(1, FFN_CONV):
            y = y + _shift_rows(u, up, kk) * cw[FFN_CONV - 1 - kk:FFN_CONV - kk, :]
        return y

    d = functools.partial(jnp.dot, preferred_element_type=F32)

    def up(c):
        wg = wg_ref[c]
        wv = wv_ref[c]
        return d(h, wg), d(hp, wg) * keep, d(h, wv), d(hp, wv) * keep

    nxt = up(0)
    for c in range(n_chunks):
        ug, ugp, uv, uvp = nxt
        if c + 1 < n_chunks:
            nxt = up(c + 1)
        gate = conv(ug, ugp, cg_ref[c], bg_ref[c])
        val = conv(uv, uvp, cv_ref[c], bv_ref[c])
        out_ref[...] += d(_bf(_silu(gate) * val), wd_ref[c])


def _ffn(x, norm_w, w_up, conv_w, conv_b, w_down, seq, tm, fc):
    m, d = x.shape
    d_ff = w_down.shape[0]
    nc = d_ff // fc
    halo = BF16_ROWS

    def chunks(a):
        return a.reshape(a.shape[0], nc, fc).transpose(1, 0, 2)

    wg = chunks(_bf(w_up[:, :d_ff]))
    wv = chunks(_bf(w_up[:, d_ff:]))
    cg = chunks(conv_w[:, :d_ff].astype(F32))
    cv = chunks(conv_w[:, d_ff:].astype(F32))
    bg = chunks(conv_b[None, :d_ff].astype(F32))
    bv = chunks(conv_b[None, d_ff:].astype(F32))
    wd = _bf(w_down).reshape(nc, fc, d)
    kern = functools.partial(_ffn_kernel, tiles_per_seq=seq // tm, n_chunks=nc)
    return pl.pallas_call(
        kern,
        out_shape=jax.ShapeDtypeStruct((m, d), F32),
        grid=(m // tm,),
        in_specs=[
            pl.BlockSpec((tm, d), lambda i: (i, 0)),
            pl.BlockSpec((halo, d), lambda i: (jnp.maximum(i * (tm // halo) - 1, 0), 0)),
            _const_spec((1, d)),
            _const_spec((nc, d, fc)),
            _const_spec((nc, d, fc)),
            _const_spec((nc, FFN_CONV, fc)),
            _const_spec((nc, FFN_CONV, fc)),
            _const_spec((nc, 1, fc)),
            _const_spec((nc, 1, fc)),
            _const_spec((nc, fc, d)),
        ],
        out_specs=pl.BlockSpec((tm, d), lambda i: (i, 0)),
        compiler_params=_params("parallel"),
        name="ffn",
    )(x, x, norm_w.reshape(1, d), wg, wv, cg, cv, bg, bv, wd)


def _tiles(seq):
    tm = min(512, seq)
    t_len = min(256, seq)
    return tm, t_len


def kernel(x, mem, norm_mix, w_in, ret_norm, moba_q_norm, moba_k_norm, gdn_conv, gdn_a_log, gdn_dt_bias,
           gdn_norm, w_out, norm_cross, norm_mem, cross_wq, cross_wkv, cross_q_norm, cross_k_norm, cross_wo,
           norm_ffn, ffn_up, ffn_conv, ffn_conv_b, ffn_down, rel_bias):
    batch, seq, d = x.shape
    mem_len = mem.shape[1]
    depth = w_in.shape[0]
    tm, t_len = _tiles(seq)
    xf = x.reshape(batch * seq, d)
    memf = mem.reshape(batch * mem_len, d)
    for l in range(depth):
        ret_in, moba_in, gdn_in, scol = _inproj(xf, norm_mix[l], w_in[l], tm)
        y_ret = _retention(ret_in, ret_norm[l], batch, seq, t_len)
        y_moba = _moba(moba_in, moba_q_norm[l], moba_k_norm[l], rel_bias, batch, seq)
        y_gdn = _gdn(gdn_in, scol, gdn_conv[l], gdn_a_log[l], gdn_dt_bias[l], gdn_norm[l],
                     batch, seq, t_len)
        xf = _outproj(y_ret, y_moba, y_gdn, w_out[l], xf, tm)
        kn, v = _memkv(memf, norm_mem[l], cross_wkv[l], cross_k_norm[l], mem_len)
        xf = _cross(xf, norm_cross[l], cross_wq[l], cross_q_norm[l], kn, v, cross_wo[l],
                    batch, seq, mem_len, tm)
        xf = _ffn(xf, norm_ffn[l], ffn_up[l], ffn_conv[l], ffn_conv_b[l], ffn_down[l], seq, tm, 256)
    return xf.reshape(batch, seq, d)
```

```python
import functools
import math

import numpy as np
import jax
import jax.numpy as jnp
from jax import lax
from jax.experimental import pallas as pl
from jax.experimental.pallas import tpu as pltpu

F32 = jnp.float32
BF16 = jnp.bfloat16

HEAD_DIM = 64
RET_HEADS = 4
MOBA_HEADS = 8
GDN_HEADS = 4
RET_W = RET_HEADS * HEAD_DIM
MOBA_W = MOBA_HEADS * HEAD_DIM
GDN_W = GDN_HEADS * HEAD_DIM
ROPE_BASE = 10000.0
MOBA_BLOCK = 256
MOBA_TOPK = 3
REL_BUCKETS = 32
REL_MAX_DIST = 128
GDN_CHUNK = 64
GDN_CONV = 4
CROSS_HEADS = 4
CROSS_HEAD_DIM = 128
FFN_CONV = 3
EPS = 1e-6

LANES = 128
SUBLANES = 8
BF16_ROWS = 16
VMEM_LIMIT = 56 * 1024 * 1024


def _bf(a):
    return a.astype(BF16)


def _mm(a, b):
    return jnp.dot(_bf(a), _bf(b), preferred_element_type=F32)


def _mm_nt(a, b):
    return lax.dot_general(_bf(a), _bf(b), (((1,), (1,)), ((), ())), preferred_element_type=F32)


def _split(a):
    hi = a.astype(BF16)
    lo = (a - hi.astype(F32)).astype(BF16)
    return hi, lo


def _mm3(a, b):
    ah, al = _split(a)
    bh, bl = _split(b)
    d = functools.partial(jnp.dot, preferred_element_type=F32)
    return d(ah, bh) + (d(ah, bl) + d(al, bh))


def _mm3_nt(a, b):
    ah, al = _split(a)
    bh, bl = _split(b)
    d = functools.partial(lax.dot_general, dimension_numbers=(((1,), (1,)), ((), ())),
                          preferred_element_type=F32)
    return d(ah, bh) + (d(ah, bl) + d(al, bh))


def _mm2_l(a, b_exact):
    ah, al = _split(a)
    d = functools.partial(jnp.dot, preferred_element_type=F32)
    return d(ah, b_exact) + d(al, b_exact)


def _mm2_r(a_exact, b):
    bh, bl = _split(b)
    d = functools.partial(jnp.dot, preferred_element_type=F32)
    return d(a_exact, bh) + d(a_exact, bl)


def _sigmoid(x):
    return 1.0 / (1.0 + jnp.exp(-x))


def _silu(x):
    return x * _sigmoid(x)


def _softplus(x):
    return jnp.maximum(x, 0.0) + jnp.log1p(jnp.exp(-jnp.abs(x)))


def _rms_rows(x, w):
    return x * lax.rsqrt(jnp.mean(x * x, axis=-1, keepdims=True) + EPS) * w


def _shift_rows(u, prev, k):
    r = pltpu.roll(u, k, axis=0)
    rp = pltpu.roll(prev, k, axis=0)[:SUBLANES]
    row = lax.broadcasted_iota(jnp.int32, (SUBLANES, u.shape[1]), 0)
    top = jnp.where(row < k, rp, r[:SUBLANES])
    return jnp.concatenate([top, r[SUBLANES:]], axis=0)


def _const_spec(shape):
    nd = len(shape)
    return pl.BlockSpec(shape, lambda *_: (0,) * nd)


def _params(*sem):
    return pltpu.CompilerParams(dimension_semantics=sem, vmem_limit_bytes=VMEM_LIMIT)


def _inproj_kernel(x_ref, nw_ref, w_ref, ws_ref, ret_ref, moba_ref, gdn_ref, scol_ref):
    x = x_ref[...]
    hf = _rms_rows(x, nw_ref[...])
    h = _bf(hf)
    off = 0
    for ref in (ret_ref, moba_ref, gdn_ref):
        width = ref.shape[1]
        for c in range(0, width, 512):
            ref[:, c:c + 512] = jnp.dot(h, w_ref[:, off + c:off + c + 512],
                                        preferred_element_type=F32).astype(ref.dtype)
        off += width
    scol_ref[...] = _mm3(hf, ws_ref[...])


def _inproj(x, norm_w, w_in, tm):
    m, d = x.shape
    main = 4 * RET_W + 3 * MOBA_W + 4 * GDN_W
    w_main = _bf(w_in[:, :main])
    w_small = jnp.zeros((d, LANES), F32).at[:, :2 * GDN_HEADS].set(w_in[:, main:])
    out_shape = (
        jax.ShapeDtypeStruct((m, 4 * RET_W), BF16),
        jax.ShapeDtypeStruct((m, 3 * MOBA_W), BF16),
        jax.ShapeDtypeStruct((m, 4 * GDN_W), BF16),
        jax.ShapeDtypeStruct((m, LANES), F32),
    )
    return pl.pallas_call(
        _inproj_kernel,
        out_shape=out_shape,
        grid=(m // tm,),
        in_specs=[
            pl.BlockSpec((tm, d), lambda i: (i, 0)),
            _const_spec((1, d)),
            _const_spec((d, main)),
            _const_spec((d, LANES)),
        ],
        out_specs=(
            pl.BlockSpec((tm, 4 * RET_W), lambda i: (i, 0)),
            pl.BlockSpec((tm, 3 * MOBA_W), lambda i: (i, 0)),
            pl.BlockSpec((tm, 4 * GDN_W), lambda i: (i, 0)),
            pl.BlockSpec((tm, LANES), lambda i: (i, 0)),
        ),
        compiler_params=_params("parallel"),
        name="inproj",
    )(x, norm_w.reshape(1, d), w_main, w_small)


def _head_consts(n_heads):
    w = n_heads * HEAD_DIM
    head = np.arange(w) // HEAD_DIM
    same = (head[:, None] == head[None, :]).astype(np.float32)
    hmask = (head[None, None, :] == np.arange(n_heads)[:, None, None]).astype(np.float32)
    return same, hmask


def _ret_kernel(in_ref, cos_ref, sin_ref, gn_ref, rot_ref, eye_ref, dmat_ref, xi_ref, zeta_ref, gt_ref,
                hmask_ref, same_ref, out_ref, s_ref, *, n_heads):
    w = n_heads * HEAD_DIM
    t_len = in_ref.shape[0]

    @pl.when(pl.program_id(1) == 0)
    def _():
        s_ref[...] = jnp.zeros_like(s_ref)

    xin = in_ref[...]
    q = xin[:, :w]
    k = xin[:, w:2 * w]
    v = xin[:, 2 * w:3 * w]
    g = xin[:, 3 * w:].astype(F32)
    cos = cos_ref[...]
    sin = sin_ref[...]
    rot = rot_ref[...]
    qr = q.astype(F32) * cos + jnp.dot(q, rot, preferred_element_type=F32) * sin
    kr = k.astype(F32) * cos + jnp.dot(k, rot, preferred_element_type=F32) * sin
    same = same_ref[...]
    k_t = _mm_nt(eye_ref[...], kr)
    state = s_ref[...]
    o = _mm(qr, state) * xi_ref[...]
    vf = v.astype(F32)
    hms = [hmask_ref[h] for h in range(n_heads)]
    ss = [_mm_nt(qr * hm, kr) * dmat_ref[h] for h, hm in enumerate(hms)]
    for s, hm in zip(ss, hms):
        o = o + _mm(s, vf * hm)
    zr = jnp.concatenate(
        [jnp.broadcast_to(zeta_ref[h:h + 1, :], (HEAD_DIM, t_len)) for h in range(n_heads)], axis=0)
    s_ref[...] = state * gt_ref[...] + _mm(k_t * zr, v) * same
    ms = _mm2_l(o * o, _bf(same)) * (1.0 / HEAD_DIM)
    y = o * lax.rsqrt(ms + EPS) * gn_ref[...] * _silu(g)
    out_ref[...] = y.astype(out_ref.dtype)


def _retention(ret_in, ret_norm, batch, seq, t_len):
    n_heads = RET_HEADS
    w = n_heads * HEAD_DIM
    m = batch * seq
    nt = seq // t_len
    half = HEAD_DIM // 2
    pos = np.arange(seq, dtype=np.float32)
    inv_freq = (ROPE_BASE ** (-jnp.arange(half, dtype=F32) / half))
    ang = jnp.asarray(pos)[:, None] * inv_freq[None, :]
    cos = jnp.tile(jnp.concatenate([jnp.cos(ang)] * 2, axis=-1), (1, n_heads))
    sin = jnp.tile(jnp.concatenate([jnp.sin(ang)] * 2, axis=-1), (1, n_heads))
    j = np.arange(w)
    rot = np.zeros((w, w), np.float32)
    first = (j % HEAD_DIM) < half
    rot[j[first] + half, j[first]] = -1.0
    rot[j[~first] - half, j[~first]] = 1.0
    same, hmask = _head_consts(n_heads)
    log_gamma = jnp.log1p(-jnp.exp2(-5.0 - jnp.arange(n_heads, dtype=F32)))
    idx = jnp.arange(t_len, dtype=F32)
    diff = idx[:, None] - idx[None, :]
    scale = HEAD_DIM ** -0.5
    dmat = jnp.where(diff >= 0, jnp.exp(log_gamma[:, None, None] * jnp.maximum(diff, 0.0)), 0.0) * scale
    xi = jnp.repeat(jnp.exp(log_gamma[:, None] * (idx + 1.0)).T, HEAD_DIM, axis=1)
    zeta = jnp.exp(log_gamma[:, None] * (t_len - 1.0 - idx)) * scale
    zeta = jnp.zeros((SUBLANES, t_len), F32).at[:n_heads].set(zeta)
    g_chunk = jnp.repeat(jnp.exp(log_gamma * t_len), HEAD_DIM)[None, :]
    gn = jnp.tile(ret_norm.astype(F32), n_heads)[None, :]
    kern = functools.partial(_ret_kernel, n_heads=n_heads)
    return pl.pallas_call(
        kern,
        out_shape=jax.ShapeDtypeStruct((m, w), BF16),
        grid=(batch, nt),
        in_specs=[
            pl.BlockSpec((t_len, 4 * w), lambda b, t: (b * nt + t, 0)),
            pl.BlockSpec((t_len, w), lambda b, t: (t, 0)),
            pl.BlockSpec((t_len, w), lambda b, t: (t, 0)),
            _const_spec((1, w)),
            _const_spec((w, w)),
            _const_spec((w, w)),
            _const_spec((n_heads, t_len, t_len)),
            _const_spec((t_len, w)),
            _const_spec((SUBLANES, t_len)),
            _const_spec((1, w)),
            _const_spec((n_heads, 1, w)),
            _const_spec((w, w)),
        ],
        out_specs=pl.BlockSpec((t_len, w), lambda b, t: (b * nt + t, 0)),
        scratch_shapes=[pltpu.VMEM((w, w), F32)],
        compiler_params=_params("parallel", "arbitrary"),
        name="retention",
    )(ret_in, cos, sin, gn, jnp.asarray(rot, BF16), jnp.eye(w, dtype=BF16), dmat, xi, zeta, g_chunk,
      jnp.asarray(hmask), jnp.asarray(same))


def _t5_bucket(rel):
    n = np.maximum(rel, 0)
    exact = REL_BUCKETS // 2
    nf = np.maximum(n, exact).astype(np.float32)
    ratio = np.log(nf / np.float32(exact)) / np.float32(math.log(REL_MAX_DIST / exact))
    large = exact + (ratio * np.float32(REL_BUCKETS - exact)).astype(np.int32)
    return np.where(n < exact, n, np.minimum(large, REL_BUCKETS - 1)).astype(np.int32)


MOBA_MASK = -1e30
MOBA_EXP_LIMIT = 1e37


def _moba_kernel(q_ref, k_ref, v_ref, idown_ref, idprev_ref, rb_ref, qg_ref, kg_ref, same_ref,
                 eye_ref, out_ref, kaug_ref, vaug_ref, kmean_ref, bias_ref, *, n_blocks, far_bucket):
    blk = MOBA_BLOCK
    hd = HEAD_DIM
    qi = pl.program_id(2)
    head0 = 2 * pl.program_id(0)
    same = same_ref[...]
    inv_d = 1.0 / hd
    scale = hd ** -0.5
    lane = lax.broadcasted_iota(jnp.int32, (1, 2 * hd), 1)
    in_h0 = lane < hd
    both = range(2)
    mine = [in_h0, jnp.logical_not(in_h0)]
    blk_row = lax.broadcasted_iota(jnp.int32, (n_blocks, blk), 0)

    def augmented_queries(q_blk, j):
        qf = q_blk.astype(F32)
        qn = qf * lax.rsqrt(_mm2_l(qf * qf, same) * inv_d + EPS) * qg_ref[...]
        km = kmean_ref[...]
        gates = [jnp.where(blk_row < j, _mm3_nt(jnp.where(mine[e], km, 0.0), qn), -jnp.inf) for e in both]
        ranks = [jnp.zeros_like(g) for g in gates]
        for i in range(n_blocks):
            tie = jnp.where(blk_row > i, 1.0, 0.0)
            ranks = [r + jnp.where(g[i:i + 1, :] > g, 1.0, jnp.where(g[i:i + 1, :] == g, tie, 0.0))
                     for r, g in zip(ranks, gates)]
        mask_t = [jnp.where(((blk_row < j) & (r < MOBA_TOPK)) | (blk_row == j), 0.0, MOBA_MASK)
                  for r in ranks]
        pad = jnp.zeros((hd - n_blocks - SUBLANES, blk), F32)
        zq = jnp.zeros((hd, blk), F32)
        extras = []
        for e in both:
            far = jnp.full((1, blk), rb_ref[head0 + e, far_bucket], F32)
            far_hi = far.astype(BF16).astype(F32)
            crow = jnp.concatenate([far_hi, far - far_hi, jnp.zeros((SUBLANES - 2, blk), F32)], axis=0)
            extras.append([zq, mask_t[e], crow, pad] if e == 0 else [mask_t[e], crow, pad, zq])
        extras = [_mm_nt(eye_ref[...], jnp.concatenate(parts, axis=0)) for parts in extras]
        return [jnp.where(mine[e], qn * scale, 0.0) + extras[e] for e in both]

    @pl.when((qi == 0) & (pl.program_id(1) == 0))
    def _():
        for e in both:
            for tab, ids_ref in enumerate((idown_ref, idprev_ref)):
                ids = ids_ref[...]
                tile = jnp.zeros((blk, blk), F32)
                for b in range(REL_BUCKETS):
                    tile = jnp.where(ids == b, rb_ref[head0 + e, b], tile)
                bias_ref[tab, e] = tile - rb_ref[head0 + e, far_bucket]

    @pl.when(qi == 0)
    def _():
        def body(j, c):
            r0 = pl.multiple_of(j * blk, blk)
            kb = k_ref[pl.ds(r0, blk), :].astype(F32)
            kn = kb * lax.rsqrt(_mm2_l(kb * kb, same) * inv_d + EPS) * kg_ref[...]
            kmean_ref[pl.ds(j, 1), :] = jnp.mean(kn, axis=0, keepdims=True)
            vb = v_ref[pl.ds(r0, blk), :].astype(F32)
            for e in both:
                xo = (1 - e) * hd
                ones = (lane >= xo + n_blocks) & (lane < xo + n_blocks + 3)
                ext = jnp.where((lane == xo + j) | ones, 1.0, 0.0)
                kaug_ref[e, pl.ds(r0, blk), :] = jnp.where(mine[e], kn, ext).astype(BF16)
                vaug_ref[e, pl.ds(r0, blk), :] = jnp.where(mine[e], vb, 1.0).astype(BF16)
            return c
        lax.fori_loop(0, n_blocks, body, 0)

    row = lax.broadcasted_iota(jnp.int32, (blk, blk), 0)
    col = lax.broadcasted_iota(jnp.int32, (blk, blk), 1)
    q0 = pl.multiple_of(qi * blk, blk)
    qp = pl.multiple_of(jnp.maximum(qi - 1, 0) * blk, blk)
    qbase = augmented_queries(q_ref[...], qi)
    bown = [bias_ref[0, e] for e in both]
    bprev = [bias_ref[1, e] for e in both]

    def scores(qa, e, r0, bias=None):
        s = _mm_nt(qa, kaug_ref[e, pl.ds(r0, blk), :])
        return s if bias is None else s + bias

    def finish(accs):
        o = [acc / pltpu.roll(acc, hd, axis=1) for acc in accs]
        return jnp.where(in_h0, o[0], o[1])

    k_own = [kaug_ref[e, pl.ds(q0, blk), :] for e in both]
    shifts = [jnp.sum(jnp.where(mine[e], qbase[e] * k_own[e].astype(F32), 0.0), axis=-1, keepdims=True)
              .astype(BF16).astype(F32) for e in both]
    q_shift = [_bf(qbase[e] + jnp.where(lane == (1 - e) * hd + n_blocks + 2, -shifts[e], 0.0)) for e in both]

    def fast_tiles(accs, starts):
        ss = [[scores(q_shift[e], e, r0) for e in both] for r0 in starts]
        for s, r0 in zip(ss, starts):
            accs = tuple(accs[e] + _mm(jnp.exp(s[e]), vaug_ref[e, pl.ds(r0, blk), :]) for e in both)
        return accs

    def fast_tile(accs, r0):
        return fast_tiles(accs, [r0])

    no_prev = jnp.where(qi >= 1, 0.0, MOBA_MASK)
    s_own = [jnp.where(col <= row, scores(q_shift[e], e, q0, bown[e]), -jnp.inf) for e in both]
    s_prev = [scores(q_shift[e], e, qp, bprev[e]) + no_prev for e in both]
    accs = tuple(_mm(jnp.exp(s_own[e]), vaug_ref[e, pl.ds(q0, blk), :])
                 + _mm(jnp.exp(s_prev[e]), vaug_ref[e, pl.ds(qp, blk), :]) for e in both)
    n_far = jnp.maximum(qi - 1, 0)

    def starts(first, count):
        return [pl.multiple_of((first + u) * blk, blk) for u in range(count)]

    group = 4
    accs = lax.fori_loop(0, n_far // group, lambda t, a: fast_tiles(a, starts(group * t, group)), accs)
    done = (n_far // group) * group
    left = n_far - done
    accs = lax.cond(left >= 2, lambda a: fast_tiles(a, starts(done, 2)), lambda a: a, accs)
    last = jnp.maximum(n_far - 1, 0)
    accs = lax.cond(left % 2 == 1, lambda a: fast_tile(a, starts(last, 1)[0]), lambda a: a, accs)

    unsafe = sum(jnp.sum(jnp.where(jnp.abs(acc) < MOBA_EXP_LIMIT, 0.0, 1.0)) for acc in accs)
    overflowed = unsafe > 0.0

    def safe_path(_):
        def update(carry, s, vaug):
            m_i, acc = carry
            m_n = jnp.maximum(m_i, jnp.max(s, axis=-1, keepdims=True))
            return m_n, jnp.exp(m_i - m_n) * acc + _mm(jnp.exp(s - m_n), vaug)

        def block(j, c):
            r0 = pl.multiple_of(j * blk, blk)
            out = []
            for e in range(2):
                s = scores(_bf(qbase[e]), e, r0) + jnp.where(j == qi - 1, bprev[e], 0.0)
                out.append(update(c[e], s, vaug_ref[e, pl.ds(r0, blk), :]))
            return tuple(out)

        init = []
        for e in range(2):
            s = jnp.where(col <= row, scores(_bf(qbase[e]), e, q0, bown[e]), -jnp.inf)
            m0 = jnp.max(s, axis=-1, keepdims=True)
            init.append((m0, _mm(jnp.exp(s - m0), vaug_ref[e, pl.ds(q0, blk), :])))
        c = lax.fori_loop(0, qi, block, tuple(init))
        return finish([c[0][1], c[1][1]])

    out = lax.cond(overflowed, safe_path, lambda _: finish(accs), 0)
    out_ref[...] = out.astype(out_ref.dtype)


def _moba_bucket_tables(seq):
    blk = MOBA_BLOCK
    d = np.arange(blk, dtype=np.int32)[:, None] - np.arange(blk, dtype=np.int32)[None, :]
    far = _t5_bucket(np.arange(blk + 1, max(seq, blk + 2), dtype=np.int32))
    assert (far == far[0]).all()
    return _t5_bucket(d), _t5_bucket(d + blk), int(far[0])


def _moba(moba_in, q_norm, k_norm, rel_bias, batch, seq):
    m = batch * seq
    blk = MOBA_BLOCK
    nb = seq // blk
    assert nb % SUBLANES == 0 and nb + SUBLANES <= HEAD_DIM
    pairs = MOBA_HEADS // 2
    pw = 2 * HEAD_DIM
    ids_own, ids_prev, far_bucket = _moba_bucket_tables(seq)
    same, _ = _head_consts(2)
    qg = jnp.tile(q_norm.astype(F32), 2)[None, :]
    kg = jnp.tile(k_norm.astype(F32), 2)[None, :]
    kern = functools.partial(_moba_kernel, n_blocks=nb, far_bucket=far_bucket)
    return pl.pallas_call(
        kern,
        out_shape=jax.ShapeDtypeStruct((m, MOBA_W), BF16),
        grid=(pairs, batch, nb),
        in_specs=[
            pl.BlockSpec((blk, pw), lambda p, b, i: (b * nb + i, p)),
            pl.BlockSpec((seq, pw), lambda p, b, i: (b, pairs + p)),
            pl.BlockSpec((seq, pw), lambda p, b, i: (b, 2 * pairs + p)),
            _const_spec((blk, blk)),
            _const_spec((blk, blk)),
            pl.BlockSpec(memory_space=pltpu.SMEM),
            _const_spec((1, pw)),
            _const_spec((1, pw)),
            _const_spec((pw, pw)),
            _const_spec((blk, blk)),
        ],
        out_specs=pl.BlockSpec((blk, pw), lambda p, b, i: (b * nb + i, p)),
        scratch_shapes=[pltpu.VMEM((2, seq, pw), BF16), pltpu.VMEM((2, seq, pw), BF16),
                        pltpu.VMEM((nb, pw), F32), pltpu.VMEM((2, 2, blk, blk), F32)],
        compiler_params=_params("arbitrary", "arbitrary", "arbitrary"),
        name="moba",
    )(moba_in, moba_in, moba_in, jnp.asarray(ids_own), jnp.asarray(ids_prev), rel_bias.astype(F32),
      qg, kg, jnp.asarray(same, BF16), jnp.eye(blk, dtype=BF16))


def _tri_inverse(ns, eye):
    xs = [eye - n for n in ns]
    ps = list(ns)
    steps = int(math.log2(GDN_CHUNK)) - 1
    for _ in range(steps):
        ps = [_mm(p, p) for p in ps]
        xs = [x + _mm(x, p) for x, p in zip(xs, ps)]
    rs = [eye - (x + _mm3(n, x)) for n, x in zip(ns, xs)]
    return [x + _mm(x, r) for x, r in zip(xs, rs)]


def _gdn_kernel(in_ref, scol_ref, *rest, n_heads):
    consts, (out_ref, s_ref, tail_ref) = rest[:-3], rest[-3:]

    @pl.when(pl.program_id(1) == 0)
    def _():
        s_ref[...] = jnp.zeros_like(s_ref)
        tail_ref[...] = jnp.zeros_like(tail_ref)

    for i in range(in_ref.shape[0]):
        _gdn_block(in_ref.at[i], scol_ref.at[i], *consts,
                   out_ref.at[i], s_ref.at[i], tail_ref.at[i], n_heads=n_heads)


def _gdn_block(in_ref, scol_ref, cw_ref, alog_b_ref, dtb_b_ref, alog_r_ref, dtb_r_ref, gn_ref,
               lblk_ref, ublk_ref, mstrict_ref, mincl_ref, hmask_ref, same_ref, eye_ref, sela_ref, selb_ref,
               selt_ref, out_ref, s_ref, tail_ref, *, n_heads):
    w = n_heads * HEAD_DIM
    t_len = in_ref.shape[0]
    c_len = GDN_CHUNK
    neg_inf = -jnp.inf

    xin = in_ref[...].astype(F32)
    raw = xin[:, :3 * w]
    z = xin[:, 3 * w:]
    prev = tail_ref[...]
    cw = cw_ref[...]
    acc = raw * cw[GDN_CONV - 1:GDN_CONV, :]
    for kk in range(1, GDN_CONV):
        acc = acc + _shift_rows(raw, prev, kk) * cw[GDN_CONV - 1 - kk:GDN_CONV - kk, :]
    tail_ref[...] = raw[t_len - SUBLANES:, :]
    y = _silu(acc)
    q = y[:, :w]
    k = y[:, w:2 * w]
    v = y[:, 2 * w:]
    same = same_ref[...]
    same_b = _bf(same)
    qn = q * lax.rsqrt(_mm2_l(q * q, same_b) + EPS) * (HEAD_DIM ** -0.5)
    kn = k * lax.rsqrt(_mm2_l(k * k, same_b) + EPS)

    sc = scol_ref[...]
    beta_b = _sigmoid(_mm2_l(sc, selb_ref[...]))
    ld_b = -jnp.exp(alog_b_ref[...]) * _softplus(_mm2_l(sc, sela_ref[...]) + dtb_b_ref[...])
    g_b = _mm2_r(lblk_ref[...], ld_b)
    sc_hi = sc.astype(BF16)
    sc_r = sc - sc_hi.astype(F32)
    sc_mid = sc_r.astype(BF16)
    sc_lo = (sc_r - sc_mid.astype(F32)).astype(BF16)
    sr = sum(_mm_nt(selt_ref[...], part) for part in (sc_hi, sc_mid, sc_lo))
    ld_r = -jnp.exp(alog_r_ref[...]) * _softplus(sr + dtb_r_ref[...])
    g_r = _mm2_l(ld_r, ublk_ref[...])
    rem_r = _mm2_l(ld_r, _bf(mstrict_ref[...]))

    eye = eye_ref[...]
    k_t = _mm_nt(eye, kn)
    rem_rows = jnp.concatenate(
        [jnp.broadcast_to(rem_r[n_heads + h:n_heads + h + 1, :], (HEAD_DIM, t_len)) for h in range(n_heads)],
        axis=0)
    kdec_t = k_t * jnp.exp(rem_rows)
    eg = jnp.exp(g_b)
    qdec = qn * eg
    wrhs = kn * beta_b * eg
    urhs = v * beta_b
    mstrict = mstrict_ref[...]
    mincl = mincl_ref[...]
    eye_f = eye.astype(F32)
    u = jnp.zeros((t_len, w), F32)
    wv = jnp.zeros((t_len, w), F32)
    heads = range(n_heads)
    hms = [hmask_ref[h] for h in heads]
    decs = [jnp.exp(jnp.where(mincl > 0.0, g_b[:, h * HEAD_DIM:h * HEAD_DIM + 1]
                              - g_r[n_heads + h:n_heads + h + 1, :], neg_inf)) for h in heads]
    kks = [_mm_nt(kn * hms[h], kn) for h in heads]
    ns = [jnp.where(mstrict > 0.0, beta_b[:, h * HEAD_DIM:h * HEAD_DIM + 1] * kks[h] * decs[h], 0.0)
          for h in heads]
    t_invs = _tri_inverse(ns, eye_f)
    for h in heads:
        for part in _split(t_invs[h]):
            u = u + _mm(part, urhs * hms[h])
            wv = wv + _mm(part, wrhs * hms[h])
    qks = [_mm_nt(qn * hms[h], kn) * decs[h] for h in heads]

    state = s_ref[...]
    vnews = []
    ointer = []
    for c in range(t_len // c_len):
        rc = slice(c * c_len, (c + 1) * c_len)
        from_state = _mm(jnp.concatenate([wv[rc], qdec[rc]], axis=0), state)
        vnew = u[rc] - from_state[:c_len]
        vnews.append(vnew)
        ointer.append(from_state[c_len:])
        gl = eg[(c + 1) * c_len - 1:(c + 1) * c_len, :]
        state = state * gl + _mm(kdec_t[:, rc], vnew) * same
    s_ref[...] = state
    vn = jnp.concatenate(vnews, axis=0)
    o = jnp.concatenate(ointer, axis=0)
    for h in range(n_heads):
        o = o + _mm(qks[h], vn * hmask_ref[h])
    ms = _mm2_l(o * o, same_b) * (1.0 / HEAD_DIM)
    yo = o * lax.rsqrt(ms + EPS) * gn_ref[...] * _silu(z)
    out_ref[...] = yo.astype(out_ref.dtype)


def _gdn(gdn_in, scol, conv_w, a_log, dt_bias, gdn_norm, batch, seq, t_len):
    n_heads = GDN_HEADS
    w = n_heads * HEAD_DIM
    m = batch * seq
    nt = seq // t_len
    chunk = np.arange(t_len) // GDN_CHUNK
    pos = np.arange(t_len)
    samec = chunk[:, None] == chunk[None, :]
    mincl = (samec & (pos[None, :] <= pos[:, None])).astype(np.float32)
    mstrict = (samec & (pos[None, :] < pos[:, None])).astype(np.float32)
    same, hmask = _head_consts(n_heads)
    head = np.arange(w) // HEAD_DIM
    selb = np.zeros((LANES, w), np.float32)
    sela = np.zeros((LANES, w), np.float32)
    selb[head, np.arange(w)] = 1.0
    sela[n_heads + head, np.arange(w)] = 1.0
    alog_b = jnp.repeat(a_log.astype(F32), HEAD_DIM)[None, :]
    dtb_b = jnp.repeat(dt_bias.astype(F32), HEAD_DIM)[None, :]
    alog_r = jnp.zeros((BF16_ROWS, 1), F32).at[n_heads:2 * n_heads, 0].set(a_log.astype(F32))
    dtb_r = jnp.zeros((BF16_ROWS, 1), F32).at[n_heads:2 * n_heads, 0].set(dt_bias.astype(F32))
    gn = jnp.tile(gdn_norm.astype(F32), n_heads)[None, :]
    kern = functools.partial(_gdn_kernel, n_heads=n_heads)
    g = 1
    selt = np.zeros((BF16_ROWS, LANES), np.float32)
    selt[np.arange(2 * n_heads), np.arange(2 * n_heads)] = 1.0
    out = pl.pallas_call(
        kern,
        out_shape=jax.ShapeDtypeStruct((batch, seq, w), BF16),
        grid=(batch // g, nt),
        in_specs=[
            pl.BlockSpec((g, t_len, 4 * w), lambda b, t: (b, t, 0)),
            pl.BlockSpec((g, t_len, LANES), lambda b, t: (b, t, 0)),
            _const_spec((GDN_CONV, 3 * w)),
            _const_spec((1, w)),
            _const_spec((1, w)),
            _const_spec((BF16_ROWS, 1)),
            _const_spec((BF16_ROWS, 1)),
            _const_spec((1, w)),
            _const_spec((t_len, t_len)),
            _const_spec((t_len, t_len)),
            _const_spec((t_len, t_len)),
            _const_spec((t_len, t_len)),
            _const_spec((n_heads, 1, w)),
            _const_spec((w, w)),
            _const_spec((w, w)),
            _const_spec((LANES, w)),
            _const_spec((LANES, w)),
            _const_spec((BF16_ROWS, LANES)),
        ],
        out_specs=pl.BlockSpec((g, t_len, w), lambda b, t: (b, t, 0)),
        scratch_shapes=[pltpu.VMEM((g, w, w), F32), pltpu.VMEM((g, SUBLANES, 3 * w), F32)],
        compiler_params=_params("parallel", "arbitrary"),
        name="gdn",
    )(gdn_in.reshape(batch, seq, 4 * w), scol.reshape(batch, seq, LANES), conv_w.astype(F32),
      alog_b, dtb_b, alog_r, dtb_r, gn,
      jnp.asarray(mincl, BF16), jnp.asarray(mincl.T, BF16), jnp.asarray(mstrict), jnp.asarray(mincl),
      jnp.asarray(hmask), jnp.asarray(same), jnp.eye(w, dtype=BF16), jnp.asarray(sela, BF16),
      jnp.asarray(selb, BF16), jnp.asarray(selt, BF16))
    return out.reshape(m, w)


def _memkv_kernel(mem_ref, nw_ref, wkv_ref, kg_ref, k_ref, v_ref):
    cw = CROSS_HEADS * CROSS_HEAD_DIM
    h = _bf(_rms_rows(mem_ref[...], nw_ref[...]))
    kv = jnp.dot(h, wkv_ref[...], preferred_element_type=F32)
    for hd in range(CROSS_HEADS):
        sl = slice(hd * CROSS_HEAD_DIM, (hd + 1) * CROSS_HEAD_DIM)
        k_ref[:, sl] = _rms_rows(kv[:, sl], kg_ref[...]).astype(k_ref.dtype)
    v_ref[...] = kv[:, cw:].astype(v_ref.dtype)


def _memkv(mem, norm_mem, wkv, k_norm, tm):
    m, d = mem.shape
    cw = CROSS_HEADS * CROSS_HEAD_DIM
    return pl.pallas_call(
        _memkv_kernel,
        out_shape=(jax.ShapeDtypeStruct((m, cw), BF16), jax.ShapeDtypeStruct((m, cw), BF16)),
        grid=(m // tm,),
        in_specs=[
            pl.BlockSpec((tm, d), lambda i: (i, 0)),
            _const_spec((1, d)),
            _const_spec((d, 2 * cw)),
            _const_spec((1, CROSS_HEAD_DIM)),
        ],
        out_specs=(pl.BlockSpec((tm, cw), lambda i: (i, 0)), pl.BlockSpec((tm, cw), lambda i: (i, 0))),
        compiler_params=_params("parallel"),
        name="memkv",
    )(mem, norm_mem.reshape(1, d), _bf(wkv), k_norm.reshape(1, CROSS_HEAD_DIM).astype(F32))


def _cross_kernel(yr_ref, ym_ref, yg_ref, wr_ref, wm_ref, wg_ref, x_ref, nw_ref, wq_ref, qg_ref, k_ref, v_ref,
                  wo_ref, out_ref):
    d = functools.partial(jnp.dot, preferred_element_type=F32)
    x = x_ref[...] + (d(yr_ref[...], wr_ref[...]) + d(ym_ref[...], wm_ref[...]) + d(yg_ref[...], wg_ref[...]))
    h = _bf(_rms_rows(x, nw_ref[...]))
    q = jnp.dot(h, wq_ref[...], preferred_element_type=F32)
    scale = CROSS_HEAD_DIM ** -0.5
    sls = [slice(hd * CROSS_HEAD_DIM, (hd + 1) * CROSS_HEAD_DIM) for hd in range(CROSS_HEADS)]
    qhs = [_rms_rows(q[:, sl], qg_ref[...]) for sl in sls]
    ss = [_mm_nt(qh, k_ref[:, sl]) * scale for qh, sl in zip(qhs, sls)]
    ps = [jnp.exp(s - jnp.max(s, axis=-1, keepdims=True)) for s in ss]
    ls = [jnp.sum(p, axis=-1, keepdims=True) for p in ps]
    outs = [_mm(p, v_ref[:, sl]) / l for p, sl, l in zip(ps, sls, ls)]
    o = _bf(jnp.concatenate(outs, axis=-1))
    out_ref[...] = x + jnp.dot(o, wo_ref[...], preferred_element_type=F32)


def _cross(y_ret, y_moba, y_gdn, w_out, x, norm_w, wq, q_norm, kn, v, wo, batch, seq, mem_len, tm):
    m, d = x.shape
    cw = CROSS_HEADS * CROSS_HEAD_DIM
    nt = seq // tm
    wb = _bf(w_out)
    wr = wb[:RET_W]
    wm = wb[RET_W:RET_W + MOBA_W]
    wg = wb[RET_W + MOBA_W:]
    return pl.pallas_call(
        _cross_kernel,
        out_shape=jax.ShapeDtypeStruct((m, d), F32),
        grid=(batch, nt),
        in_specs=[
            pl.BlockSpec((tm, RET_W), lambda b, t: (b * nt + t, 0)),
            pl.BlockSpec((tm, MOBA_W), lambda b, t: (b * nt + t, 0)),
            pl.BlockSpec((tm, GDN_W), lambda b, t: (b * nt + t, 0)),
            _const_spec((RET_W, d)),
            _const_spec((MOBA_W, d)),
            _const_spec((GDN_W, d)),
            pl.BlockSpec((tm, d), lambda b, t: (b * nt + t, 0)),
            _const_spec((1, d)),
            _const_spec((d, cw)),
            _const_spec((1, CROSS_HEAD_DIM)),
            pl.BlockSpec((mem_len, cw), lambda b, t: (b, 0)),
            pl.BlockSpec((mem_len, cw), lambda b, t: (b, 0)),
            _const_spec((cw, d)),
        ],
        out_specs=pl.BlockSpec((tm, d), lambda b, t: (b * nt + t, 0)),
        compiler_params=_params("parallel", "parallel"),
        name="cross",
    )(y_ret, y_moba, y_gdn, wr, wm, wg, x, norm_w.reshape(1, d), _bf(wq),
      q_norm.reshape(1, CROSS_HEAD_DIM).astype(F32), kn, v, _bf(wo))


def _ffn_kernel(x_ref, xp_ref, nw_ref, wg_ref, wv_ref, cg_ref, cv_ref, bg_ref, bv_ref, wd_ref, out_ref,
                *, tiles_per_seq, n_chunks):
    x = x_ref[...]
    nw = nw_ref[...]
    h = _bf(_rms_rows(x, nw))
    keep = jnp.where(pl.program_id(0) % tiles_per_seq == 0, 0.0, 1.0)
    hp = _bf(_rms_rows(xp_ref[...], nw))
    out_ref[...] = x

    def conv(u, up, cw, b):
        y = u * cw[FFN_CONV - 1:FFN_CONV, :] + b
        for kk in range(1, FFN_CONV):
            y = y + _shift_rows(u, up, kk) * cw[FFN_CONV - 1 - kk:FFN_CONV - kk, :]
        return y

    d = functools.partial(jnp.dot, preferred_element_type=F32)

    def up(c):
        wg = wg_ref[c]
        wv = wv_ref[c]
        return d(h, wg), d(hp, wg) * keep, d(h, wv), d(hp, wv) * keep

    nxt = up(0)
    for c in range(n_chunks):
        ug, ugp, uv, uvp = nxt
        if c + 1 < n_chunks:
            nxt = up(c + 1)
        gate = conv(ug, ugp, cg_ref[c], bg_ref[c])
        val = conv(uv, uvp, cv_ref[c], bv_ref[c])
        out_ref[...] += d(_bf(_silu(gate) * val), wd_ref[c])


def _ffn(x, norm_w, w_up, conv_w, conv_b, w_down, seq, tm, fc):
    m, d = x.shape
    d_ff = w_down.shape[0]
    nc = d_ff // fc
    halo = BF16_ROWS

    def chunks(a):
        return a.reshape(a.shape[0], nc, fc).transpose(1, 0, 2)

    wg = chunks(_bf(w_up[:, :d_ff]))
    wv = chunks(_bf(w_up[:, d_ff:]))
    cg = chunks(conv_w[:, :d_ff].astype(F32))
    cv = chunks(conv_w[:, d_ff:].astype(F32))
    bg = chunks(conv_b[None, :d_ff].astype(F32))
    bv = chunks(conv_b[None, d_ff:].astype(F32))
    wd = _bf(w_down).reshape(nc, fc, d)
    kern = functools.partial(_ffn_kernel, tiles_per_seq=seq // tm, n_chunks=nc)
    return pl.pallas_call(
        kern,
        out_shape=jax.ShapeDtypeStruct((m, d), F32),
        grid=(m // tm,),
        in_specs=[
            pl.BlockSpec((tm, d), lambda i: (i, 0)),
            pl.BlockSpec((halo, d), lambda i: (jnp.maximum(i * (tm // halo) - 1, 0), 0)),
            _const_spec((1, d)),
            _const_spec((nc, d, fc)),
            _const_spec((nc, d, fc)),
            _const_spec((nc, FFN_CONV, fc)),
            _const_spec((nc, FFN_CONV, fc)),
            _const_spec((nc, 1, fc)),
            _const_spec((nc, 1, fc)),
            _const_spec((nc, fc, d)),
        ],
        out_specs=pl.BlockSpec((tm, d), lambda i: (i, 0)),
        compiler_params=_params("parallel"),
        name="ffn",
    )(x, x, norm_w.reshape(1, d), wg, wv, cg, cv, bg, bv, wd)


def _tiles(seq):
    tm = min(512, seq)
    t_len = min(256, seq)
    return tm, t_len


def kernel(x, mem, norm_mix, w_in, ret_norm, moba_q_norm, moba_k_norm, gdn_conv, gdn_a_log, gdn_dt_bias,
           gdn_norm, w_out, norm_cross, norm_mem, cross_wq, cross_wkv, cross_q_norm, cross_k_norm, cross_wo,
           norm_ffn, ffn_up, ffn_conv, ffn_conv_b, ffn_down, rel_bias):
    batch, seq, d = x.shape
    mem_len = mem.shape[1]
    depth = w_in.shape[0]
    tm, t_len = _tiles(seq)
    xf = x.reshape(batch * seq, d)
    memf = mem.reshape(batch * mem_len, d)
    for l in range(depth):
        ret_in, moba_in, gdn_in, scol = _inproj(xf, norm_mix[l], w_in[l], tm)
        y_ret = _retention(ret_in, ret_norm[l], batch, seq, t_len)
        y_moba = _moba(moba_in, moba_q_norm[l], moba_k_norm[l], rel_bias, batch, seq)
        y_gdn = _gdn(gdn_in, scol, gdn_conv[l], gdn_a_log[l], gdn_dt_bias[l], gdn_norm[l],
                     batch, seq, t_len)
        kn, v = _memkv(memf, norm_mem[l], cross_wkv[l], cross_k_norm[l], mem_len)
        xf = _cross(y_ret, y_moba, y_gdn, w_out[l], xf, norm_cross[l], cross_wq[l], cross_q_norm[l], kn, v,
                    cross_wo[l], batch, seq, mem_len, tm)
        xf = _ffn(xf, norm_ffn[l], ffn_up[l], ffn_conv[l], ffn_conv_b[l], ffn_down[l], seq, tm, 256)
    return xf.reshape(batch, seq, d)
```

```python
import functools
import math

import numpy as np
import jax
import jax.numpy as jnp
from jax import lax
from jax.experimental import pallas as pl
from jax.experimental.pallas import tpu as pltpu

F32 = jnp.float32
BF16 = jnp.bfloat16

HEAD_DIM = 64
RET_HEADS = 4
MOBA_HEADS = 8
GDN_HEADS = 4
RET_W = RET_HEADS * HEAD_DIM
MOBA_W = MOBA_HEADS * HEAD_DIM
GDN_W = GDN_HEADS * HEAD_DIM
ROPE_BASE = 10000.0
MOBA_BLOCK = 256
MOBA_TOPK = 3
REL_BUCKETS = 32
REL_MAX_DIST = 128
GDN_CHUNK = 64
GDN_CONV = 4
CROSS_HEADS = 4
CROSS_HEAD_DIM = 128
FFN_CONV = 3
EPS = 1e-6

LANES = 128
SUBLANES = 8
BF16_ROWS = 16
VMEM_LIMIT = 56 * 1024 * 1024


def _bf(a):
    return a.astype(BF16)


def _mm(a, b):
    return jnp.dot(_bf(a), _bf(b), preferred_element_type=F32)


def _mm_nt(a, b):
    return lax.dot_general(_bf(a), _bf(b), (((1,), (1,)), ((), ())), preferred_element_type=F32)


def _split(a):
    hi = a.astype(BF16)
    lo = (a - hi.astype(F32)).astype(BF16)
    return hi, lo


def _mm3(a, b):
    ah, al = _split(a)
    bh, bl = _split(b)
    d = functools.partial(jnp.dot, preferred_element_type=F32)
    return d(ah, bh) + (d(ah, bl) + d(al, bh))


def _mm3_nt(a, b):
    ah, al = _split(a)
    bh, bl = _split(b)
    d = functools.partial(lax.dot_general, dimension_numbers=(((1,), (1,)), ((), ())),
                          preferred_element_type=F32)
    return d(ah, bh) + (d(ah, bl) + d(al, bh))


def _mm2_l(a, b_exact):
    ah, al = _split(a)
    d = functools.partial(jnp.dot, preferred_element_type=F32)
    return d(ah, b_exact) + d(al, b_exact)


def _mm2_r(a_exact, b):
    bh, bl = _split(b)
    d = functools.partial(jnp.dot, preferred_element_type=F32)
    return d(a_exact, bh) + d(a_exact, bl)


def _sigmoid(x):
    return 1.0 / (1.0 + jnp.exp(-x))


def _silu(x):
    return x * _sigmoid(x)


def _softplus(x):
    return jnp.maximum(x, 0.0) + jnp.log1p(jnp.exp(-jnp.abs(x)))


def _rms_rows(x, w):
    return x * lax.rsqrt(jnp.mean(x * x, axis=-1, keepdims=True) + EPS) * w


def _shift_rows(u, prev, k):
    r = pltpu.roll(u, k, axis=0)
    rp = pltpu.roll(prev, k, axis=0)[:SUBLANES]
    row = lax.broadcasted_iota(jnp.int32, (SUBLANES, u.shape[1]), 0)
    top = jnp.where(row < k, rp, r[:SUBLANES])
    return jnp.concatenate([top, r[SUBLANES:]], axis=0)


def _const_spec(shape):
    nd = len(shape)
    return pl.BlockSpec(shape, lambda *_: (0,) * nd)


def _params(*sem):
    return pltpu.CompilerParams(dimension_semantics=sem, vmem_limit_bytes=VMEM_LIMIT)


def _inproj_kernel(x_ref, nw_ref, w_ref, ws_ref, ret_ref, moba_ref, gdn_ref, scol_ref):
    x = x_ref[...]
    hf = _rms_rows(x, nw_ref[...])
    h = _bf(hf)
    off = 0
    for ref in (ret_ref, moba_ref, gdn_ref):
        width = ref.shape[1]
        for c in range(0, width, 512):
            ref[:, c:c + 512] = jnp.dot(h, w_ref[:, off + c:off + c + 512],
                                        preferred_element_type=F32).astype(ref.dtype)
        off += width
    scol_ref[...] = _mm3(hf, ws_ref[...])


def _inproj(x, norm_w, w_in, tm):
    m, d = x.shape
    main = 4 * RET_W + 3 * MOBA_W + 4 * GDN_W
    w_main = _bf(w_in[:, :main])
    w_small = jnp.zeros((d, LANES), F32).at[:, :2 * GDN_HEADS].set(w_in[:, main:])
    out_shape = (
        jax.ShapeDtypeStruct((m, 4 * RET_W), BF16),
        jax.ShapeDtypeStruct((m, 3 * MOBA_W), BF16),
        jax.ShapeDtypeStruct((m, 4 * GDN_W), BF16),
        jax.ShapeDtypeStruct((m, LANES), F32),
    )
    return pl.pallas_call(
        _inproj_kernel,
        out_shape=out_shape,
        grid=(m // tm,),
        in_specs=[
            pl.BlockSpec((tm, d), lambda i: (i, 0)),
            _const_spec((1, d)),
            _const_spec((d, main)),
            _const_spec((d, LANES)),
        ],
        out_specs=(
            pl.BlockSpec((tm, 4 * RET_W), lambda i: (i, 0)),
            pl.BlockSpec((tm, 3 * MOBA_W), lambda i: (i, 0)),
            pl.BlockSpec((tm, 4 * GDN_W), lambda i: (i, 0)),
            pl.BlockSpec((tm, LANES), lambda i: (i, 0)),
        ),
        compiler_params=_params("parallel"),
        name="inproj",
    )(x, norm_w.reshape(1, d), w_main, w_small)


def _head_consts(n_heads):
    w = n_heads * HEAD_DIM
    head = np.arange(w) // HEAD_DIM
    same = (head[:, None] == head[None, :]).astype(np.float32)
    hmask = (head[None, None, :] == np.arange(n_heads)[:, None, None]).astype(np.float32)
    return same, hmask


def _ret_kernel(in_ref, cos_ref, sin_ref, gn_ref, rot_ref, eye_ref, dmat_ref, xi_ref, zeta_ref, gt_ref,
                hmask_ref, same_ref, out_ref, s_ref, *, n_heads):
    w = n_heads * HEAD_DIM
    t_len = in_ref.shape[0]

    @pl.when(pl.program_id(1) == 0)
    def _():
        s_ref[...] = jnp.zeros_like(s_ref)

    xin = in_ref[...]
    q = xin[:, :w]
    k = xin[:, w:2 * w]
    v = xin[:, 2 * w:3 * w]
    g = xin[:, 3 * w:].astype(F32)
    cos = cos_ref[...]
    sin = sin_ref[...]
    rot = rot_ref[...]
    qr = q.astype(F32) * cos + jnp.dot(q, rot, preferred_element_type=F32) * sin
    kr = k.astype(F32) * cos + jnp.dot(k, rot, preferred_element_type=F32) * sin
    same = same_ref[...]
    k_t = _mm_nt(eye_ref[...], kr)
    state = s_ref[...]
    o = _mm(qr, state) * xi_ref[...]
    vf = v.astype(F32)
    hms = [hmask_ref[h] for h in range(n_heads)]
    ss = [_mm_nt(qr * hm, kr) * dmat_ref[h] for h, hm in enumerate(hms)]
    for s, hm in zip(ss, hms):
        o = o + _mm(s, vf * hm)
    zr = jnp.concatenate(
        [jnp.broadcast_to(zeta_ref[h:h + 1, :], (HEAD_DIM, t_len)) for h in range(n_heads)], axis=0)
    s_ref[...] = state * gt_ref[...] + _mm(k_t * zr, v) * same
    ms = _mm2_l(o * o, _bf(same)) * (1.0 / HEAD_DIM)
    y = o * lax.rsqrt(ms + EPS) * gn_ref[...] * _silu(g)
    out_ref[...] = y.astype(out_ref.dtype)


def _retention(ret_in, ret_norm, batch, seq, t_len):
    n_heads = RET_HEADS
    w = n_heads * HEAD_DIM
    m = batch * seq
    nt = seq // t_len
    half = HEAD_DIM // 2
    pos = np.arange(seq, dtype=np.float32)
    inv_freq = (ROPE_BASE ** (-jnp.arange(half, dtype=F32) / half))
    ang = jnp.asarray(pos)[:, None] * inv_freq[None, :]
    cos = jnp.tile(jnp.concatenate([jnp.cos(ang)] * 2, axis=-1), (1, n_heads))
    sin = jnp.tile(jnp.concatenate([jnp.sin(ang)] * 2, axis=-1), (1, n_heads))
    j = np.arange(w)
    rot = np.zeros((w, w), np.float32)
    first = (j % HEAD_DIM) < half
    rot[j[first] + half, j[first]] = -1.0
    rot[j[~first] - half, j[~first]] = 1.0
    same, hmask = _head_consts(n_heads)
    log_gamma = jnp.log1p(-jnp.exp2(-5.0 - jnp.arange(n_heads, dtype=F32)))
    idx = jnp.arange(t_len, dtype=F32)
    diff = idx[:, None] - idx[None, :]
    scale = HEAD_DIM ** -0.5
    dmat = jnp.where(diff >= 0, jnp.exp(log_gamma[:, None, None] * jnp.maximum(diff, 0.0)), 0.0) * scale
    xi = jnp.repeat(jnp.exp(log_gamma[:, None] * (idx + 1.0)).T, HEAD_DIM, axis=1)
    zeta = jnp.exp(log_gamma[:, None] * (t_len - 1.0 - idx)) * scale
    zeta = jnp.zeros((SUBLANES, t_len), F32).at[:n_heads].set(zeta)
    g_chunk = jnp.repeat(jnp.exp(log_gamma * t_len), HEAD_DIM)[None, :]
    gn = jnp.tile(ret_norm.astype(F32), n_heads)[None, :]
    kern = functools.partial(_ret_kernel, n_heads=n_heads)
    return pl.pallas_call(
        kern,
        out_shape=jax.ShapeDtypeStruct((m, w), BF16),
        grid=(batch, nt),
        in_specs=[
            pl.BlockSpec((t_len, 4 * w), lambda b, t: (b * nt + t, 0)),
            pl.BlockSpec((t_len, w), lambda b, t: (t, 0)),
            pl.BlockSpec((t_len, w), lambda b, t: (t, 0)),
            _const_spec((1, w)),
            _const_spec((w, w)),
            _const_spec((w, w)),
            _const_spec((n_heads, t_len, t_len)),
            _const_spec((t_len, w)),
            _const_spec((SUBLANES, t_len)),
            _const_spec((1, w)),
            _const_spec((n_heads, 1, w)),
            _const_spec((w, w)),
        ],
        out_specs=pl.BlockSpec((t_len, w), lambda b, t: (b * nt + t, 0)),
        scratch_shapes=[pltpu.VMEM((w, w), F32)],
        compiler_params=_params("parallel", "arbitrary"),
        name="retention",
    )(ret_in, cos, sin, gn, jnp.asarray(rot, BF16), jnp.eye(w, dtype=BF16), dmat, xi, zeta, g_chunk,
      jnp.asarray(hmask), jnp.asarray(same))


def _t5_bucket(rel):
    n = np.maximum(rel, 0)
    exact = REL_BUCKETS // 2
    nf = np.maximum(n, exact).astype(np.float32)
    ratio = np.log(nf / np.float32(exact)) / np.float32(math.log(REL_MAX_DIST / exact))
    large = exact + (ratio * np.float32(REL_BUCKETS - exact)).astype(np.int32)
    return np.where(n < exact, n, np.minimum(large, REL_BUCKETS - 1)).astype(np.int32)


MOBA_MASK = -1e30
MOBA_EXP_LIMIT = 1e37


def _moba_kernel(q_ref, k_ref, v_ref, idown_ref, idprev_ref, rb_ref, qg_ref, kg_ref, same_ref,
                 eye_ref, out_ref, kaug_ref, vaug_ref, kmean_ref, bias_ref, *, n_blocks, far_bucket):
    blk = MOBA_BLOCK
    hd = HEAD_DIM
    qi = pl.program_id(2)
    head0 = 2 * pl.program_id(0)
    same = same_ref[...]
    inv_d = 1.0 / hd
    scale = hd ** -0.5
    lane = lax.broadcasted_iota(jnp.int32, (1, 2 * hd), 1)
    in_h0 = lane < hd
    both = range(2)
    mine = [in_h0, jnp.logical_not(in_h0)]
    blk_row = lax.broadcasted_iota(jnp.int32, (n_blocks, blk), 0)

    def augmented_queries(q_blk, j):
        qf = q_blk.astype(F32)
        qn = qf * lax.rsqrt(_mm2_l(qf * qf, same) * inv_d + EPS) * qg_ref[...]
        km = kmean_ref[...]
        gates = [jnp.where(blk_row < j, _mm3_nt(jnp.where(mine[e], km, 0.0), qn), -jnp.inf) for e in both]
        ranks = [jnp.zeros_like(g) for g in gates]
        for i in range(n_blocks):
            tie = jnp.where(blk_row > i, 1.0, 0.0)
            ranks = [r + jnp.where(g[i:i + 1, :] > g, 1.0, jnp.where(g[i:i + 1, :] == g, tie, 0.0))
                     for r, g in zip(ranks, gates)]
        mask_t = [jnp.where(((blk_row < j) & (r < MOBA_TOPK)) | (blk_row == j), 0.0, MOBA_MASK)
                  for r in ranks]
        pad = jnp.zeros((hd - n_blocks - SUBLANES, blk), F32)
        zq = jnp.zeros((hd, blk), F32)
        extras = []
        for e in both:
            far = jnp.full((1, blk), rb_ref[head0 + e, far_bucket], F32)
            far_hi = far.astype(BF16).astype(F32)
            crow = jnp.concatenate([far_hi, far - far_hi, jnp.zeros((SUBLANES - 2, blk), F32)], axis=0)
            extras.append([zq, mask_t[e], crow, pad] if e == 0 else [mask_t[e], crow, pad, zq])
        extras = [_mm_nt(eye_ref[...], jnp.concatenate(parts, axis=0)) for parts in extras]
        return [jnp.where(mine[e], qn * scale, 0.0) + extras[e] for e in both]

    @pl.when((qi == 0) & (pl.program_id(1) == 0))
    def _():
        for e in both:
            for tab, ids_ref in enumerate((idown_ref, idprev_ref)):
                ids = ids_ref[...]
                tile = jnp.zeros((blk, blk), F32)
                for b in range(REL_BUCKETS):
                    tile = jnp.where(ids == b, rb_ref[head0 + e, b], tile)
                bias_ref[tab, e] = tile - rb_ref[head0 + e, far_bucket]

    @pl.when(qi == 0)
    def _():
        def body(j, c):
            r0 = pl.multiple_of(j * blk, blk)
            kb = k_ref[pl.ds(r0, blk), :].astype(F32)
            kn = kb * lax.rsqrt(_mm2_l(kb * kb, same) * inv_d + EPS) * kg_ref[...]
            kmean_ref[pl.ds(j, 1), :] = jnp.mean(kn, axis=0, keepdims=True)
            vb = v_ref[pl.ds(r0, blk), :].astype(F32)
            for e in both:
                xo = (1 - e) * hd
                ones = (lane >= xo + n_blocks) & (lane < xo + n_blocks + 3)
                ext = jnp.where((lane == xo + j) | ones, 1.0, 0.0)
                kaug_ref[e, pl.ds(r0, blk), :] = jnp.where(mine[e], kn, ext).astype(BF16)
                vaug_ref[e, pl.ds(r0, blk), :] = jnp.where(mine[e], vb, 1.0).astype(BF16)
            return c
        lax.fori_loop(0, n_blocks, body, 0)

    row = lax.broadcasted_iota(jnp.int32, (blk, blk), 0)
    col = lax.broadcasted_iota(jnp.int32, (blk, blk), 1)
    q0 = pl.multiple_of(qi * blk, blk)
    qp = pl.multiple_of(jnp.maximum(qi - 1, 0) * blk, blk)
    qbase = augmented_queries(q_ref[...], qi)
    bown = [bias_ref[0, e] for e in both]
    bprev = [bias_ref[1, e] for e in both]

    def scores(qa, e, r0, bias=None):
        s = _mm_nt(qa, kaug_ref[e, pl.ds(r0, blk), :])
        return s if bias is None else s + bias

    def finish(accs):
        o = [acc / pltpu.roll(acc, hd, axis=1) for acc in accs]
        return jnp.where(in_h0, o[0], o[1])

    k_own = [kaug_ref[e, pl.ds(q0, blk), :] for e in both]
    shifts = [jnp.sum(jnp.where(mine[e], qbase[e] * k_own[e].astype(F32), 0.0), axis=-1, keepdims=True)
              .astype(BF16).astype(F32) for e in both]
    q_shift = [_bf(qbase[e] + jnp.where(lane == (1 - e) * hd + n_blocks + 2, -shifts[e], 0.0)) for e in both]

    def fast_tiles(accs, starts):
        ss = [[scores(q_shift[e], e, r0) for e in both] for r0 in starts]
        for s, r0 in zip(ss, starts):
            accs = tuple(accs[e] + _mm(jnp.exp(s[e]), vaug_ref[e, pl.ds(r0, blk), :]) for e in both)
        return accs

    def fast_tile(accs, r0):
        return fast_tiles(accs, [r0])

    no_prev = jnp.where(qi >= 1, 0.0, MOBA_MASK)
    s_own = [jnp.where(col <= row, scores(q_shift[e], e, q0, bown[e]), -jnp.inf) for e in both]
    s_prev = [scores(q_shift[e], e, qp, bprev[e]) + no_prev for e in both]
    accs = tuple(_mm(jnp.exp(s_own[e]), vaug_ref[e, pl.ds(q0, blk), :])
                 + _mm(jnp.exp(s_prev[e]), vaug_ref[e, pl.ds(qp, blk), :]) for e in both)
    n_far = jnp.maximum(qi - 1, 0)

    def starts(first, count):
        return [pl.multiple_of((first + u) * blk, blk) for u in range(count)]

    group = 4
    accs = lax.fori_loop(0, n_far // group, lambda t, a: fast_tiles(a, starts(group * t, group)), accs)
    done = (n_far // group) * group
    left = n_far - done
    accs = lax.cond(left >= 2, lambda a: fast_tiles(a, starts(done, 2)), lambda a: a, accs)
    last = jnp.maximum(n_far - 1, 0)
    accs = lax.cond(left % 2 == 1, lambda a: fast_tile(a, starts(last, 1)[0]), lambda a: a, accs)

    unsafe = sum(jnp.sum(jnp.where(jnp.abs(acc) < MOBA_EXP_LIMIT, 0.0, 1.0)) for acc in accs)
    overflowed = unsafe > 0.0

    def safe_path(_):
        def update(carry, s, vaug):
            m_i, acc = carry
            m_n = jnp.maximum(m_i, jnp.max(s, axis=-1, keepdims=True))
            return m_n, jnp.exp(m_i - m_n) * acc + _mm(jnp.exp(s - m_n), vaug)

        def block(j, c):
            r0 = pl.multiple_of(j * blk, blk)
            out = []
            for e in range(2):
                s = scores(_bf(qbase[e]), e, r0) + jnp.where(j == qi - 1, bprev[e], 0.0)
                out.append(update(c[e], s, vaug_ref[e, pl.ds(r0, blk), :]))
            return tuple(out)

        init = []
        for e in range(2):
            s = jnp.where(col <= row, scores(_bf(qbase[e]), e, q0, bown[e]), -jnp.inf)
            m0 = jnp.max(s, axis=-1, keepdims=True)
            init.append((m0, _mm(jnp.exp(s - m0), vaug_ref[e, pl.ds(q0, blk), :])))
        c = lax.fori_loop(0, qi, block, tuple(init))
        return finish([c[0][1], c[1][1]])

    out = lax.cond(overflowed, safe_path, lambda _: finish(accs), 0)
    out_ref[...] = out.astype(out_ref.dtype)


def _moba_bucket_tables(seq):
    blk = MOBA_BLOCK
    d = np.arange(blk, dtype=np.int32)[:, None] - np.arange(blk, dtype=np.int32)[None, :]
    far = _t5_bucket(np.arange(blk + 1, max(seq, blk + 2), dtype=np.int32))
    assert (far == far[0]).all()
    return _t5_bucket(d), _t5_bucket(d + blk), int(far[0])


def _moba(moba_in, q_norm, k_norm, rel_bias, batch, seq):
    m = batch * seq
    blk = MOBA_BLOCK
    nb = seq // blk
    assert nb % SUBLANES == 0 and nb + SUBLANES <= HEAD_DIM
    pairs = MOBA_HEADS // 2
    pw = 2 * HEAD_DIM
    ids_own, ids_prev, far_bucket = _moba_bucket_tables(seq)
    same, _ = _head_consts(2)
    qg = jnp.tile(q_norm.astype(F32), 2)[None, :]
    kg = jnp.tile(k_norm.astype(F32), 2)[None, :]
    kern = functools.partial(_moba_kernel, n_blocks=nb, far_bucket=far_bucket)
    return pl.pallas_call(
        kern,
        out_shape=jax.ShapeDtypeStruct((m, MOBA_W), BF16),
        grid=(pairs, batch, nb),
        in_specs=[
            pl.BlockSpec((blk, pw), lambda p, b, i: (b * nb + i, p)),
            pl.BlockSpec((seq, pw), lambda p, b, i: (b, pairs + p)),
            pl.BlockSpec((seq, pw), lambda p, b, i: (b, 2 * pairs + p)),
            _const_spec((blk, blk)),
            _const_spec((blk, blk)),
            pl.BlockSpec(memory_space=pltpu.SMEM),
            _const_spec((1, pw)),
            _const_spec((1, pw)),
            _const_spec((pw, pw)),
            _const_spec((blk, blk)),
        ],
        out_specs=pl.BlockSpec((blk, pw), lambda p, b, i: (b * nb + i, p)),
        scratch_shapes=[pltpu.VMEM((2, seq, pw), BF16), pltpu.VMEM((2, seq, pw), BF16),
                        pltpu.VMEM((nb, pw), F32), pltpu.VMEM((2, 2, blk, blk), F32)],
        compiler_params=_params("arbitrary", "arbitrary", "arbitrary"),
        name="moba",
    )(moba_in, moba_in, moba_in, jnp.asarray(ids_own), jnp.asarray(ids_prev), rel_bias.astype(F32),
      qg, kg, jnp.asarray(same, BF16), jnp.eye(blk, dtype=BF16))


def _tri_inverse(ns, eye):
    xs = [eye - n for n in ns]
    ps = list(ns)
    steps = int(math.log2(GDN_CHUNK)) - 1
    for _ in range(steps):
        ps = [_mm(p, p) for p in ps]
        xs = [x + _mm(x, p) for x, p in zip(xs, ps)]
    rs = [eye - (x + _mm3(n, x)) for n, x in zip(ns, xs)]
    return [x + _mm(x, r) for x, r in zip(xs, rs)]


def _gdn_kernel(in_ref, scol_ref, *rest, n_heads):
    consts, (out_ref, s_ref, tail_ref) = rest[:-3], rest[-3:]

    @pl.when(pl.program_id(1) == 0)
    def _():
        s_ref[...] = jnp.zeros_like(s_ref)
        tail_ref[...] = jnp.zeros_like(tail_ref)

    for i in range(in_ref.shape[0]):
        _gdn_block(in_ref.at[i], scol_ref.at[i], *consts,
                   out_ref.at[i], s_ref.at[i], tail_ref.at[i], n_heads=n_heads)


def _gdn_block(in_ref, scol_ref, cw_ref, alog_b_ref, dtb_b_ref, alog_r_ref, dtb_r_ref, gn_ref,
               lblk_ref, ublk_ref, mstrict_ref, mincl_ref, hmask_ref, same_ref, eye_ref, sela_ref, selb_ref,
               selt_ref, out_ref, s_ref, tail_ref, *, n_heads):
    w = n_heads * HEAD_DIM
    t_len = in_ref.shape[0]
    c_len = GDN_CHUNK
    neg_inf = -jnp.inf

    xin = in_ref[...].astype(F32)
    raw = xin[:, :3 * w]
    z = xin[:, 3 * w:]
    prev = tail_ref[...]
    cw = cw_ref[...]
    acc = raw * cw[GDN_CONV - 1:GDN_CONV, :]
    for kk in range(1, GDN_CONV):
        acc = acc + _shift_rows(raw, prev, kk) * cw[GDN_CONV - 1 - kk:GDN_CONV - kk, :]
    tail_ref[...] = raw[t_len - SUBLANES:, :]
    y = _silu(acc)
    q = y[:, :w]
    k = y[:, w:2 * w]
    v = y[:, 2 * w:]
    same = same_ref[...]
    same_b = _bf(same)
    qn = q * lax.rsqrt(_mm2_l(q * q, same_b) + EPS) * (HEAD_DIM ** -0.5)
    kn = k * lax.rsqrt(_mm2_l(k * k, same_b) + EPS)

    sc = scol_ref[...]
    beta_b = _sigmoid(_mm2_l(sc, selb_ref[...]))
    ld_b = -jnp.exp(alog_b_ref[...]) * _softplus(_mm2_l(sc, sela_ref[...]) + dtb_b_ref[...])
    g_b = _mm2_r(lblk_ref[...], ld_b)
    sc_hi = sc.astype(BF16)
    sc_r = sc - sc_hi.astype(F32)
    sc_mid = sc_r.astype(BF16)
    sc_lo = (sc_r - sc_mid.astype(F32)).astype(BF16)
    sr = sum(_mm_nt(selt_ref[...], part) for part in (sc_hi, sc_mid, sc_lo))
    ld_r = -jnp.exp(alog_r_ref[...]) * _softplus(sr + dtb_r_ref[...])
    g_r = _mm2_l(ld_r, ublk_ref[...])
    rem_r = _mm2_l(ld_r, _bf(mstrict_ref[...]))

    eye = eye_ref[...]
    k_t = _mm_nt(eye, kn)
    rem_rows = jnp.concatenate(
        [jnp.broadcast_to(rem_r[n_heads + h:n_heads + h + 1, :], (HEAD_DIM, t_len)) for h in range(n_heads)],
        axis=0)
    kdec_t = k_t * jnp.exp(rem_rows)
    eg = jnp.exp(g_b)
    qdec = qn * eg
    wrhs = kn * beta_b * eg
    urhs = v * beta_b
    mstrict = mstrict_ref[...]
    mincl = mincl_ref[...]
    eye_f = eye.astype(F32)
    u = jnp.zeros((t_len, w), F32)
    wv = jnp.zeros((t_len, w), F32)
    heads = range(n_heads)
    hms = [hmask_ref[h] for h in heads]
    decs = [jnp.exp(jnp.where(mincl > 0.0, g_b[:, h * HEAD_DIM:h * HEAD_DIM + 1]
                              - g_r[n_heads + h:n_heads + h + 1, :], neg_inf)) for h in heads]
    kks = [_mm_nt(kn * hms[h], kn) for h in heads]
    ns = [jnp.where(mstrict > 0.0, beta_b[:, h * HEAD_DIM:h * HEAD_DIM + 1] * kks[h] * decs[h], 0.0)
          for h in heads]
    t_invs = _tri_inverse(ns, eye_f)
    for h in heads:
        for part in _split(t_invs[h]):
            u = u + _mm(part, urhs * hms[h])
            wv = wv + _mm(part, wrhs * hms[h])
    qks = [_mm_nt(qn * hms[h], kn) * decs[h] for h in heads]

    state = s_ref[...]
    vnews = []
    ointer = []
    for c in range(t_len // c_len):
        rc = slice(c * c_len, (c + 1) * c_len)
        from_state = _mm(jnp.concatenate([wv[rc], qdec[rc]], axis=0), state)
        vnew = u[rc] - from_state[:c_len]
        vnews.append(vnew)
        ointer.append(from_state[c_len:])
        gl = eg[(c + 1) * c_len - 1:(c + 1) * c_len, :]
        state = state * gl + _mm(kdec_t[:, rc], vnew) * same
    s_ref[...] = state
    vn = jnp.concatenate(vnews, axis=0)
    o = jnp.concatenate(ointer, axis=0)
    for h in range(n_heads):
        o = o + _mm(qks[h], vn * hmask_ref[h])
    ms = _mm2_l(o * o, same_b) * (1.0 / HEAD_DIM)
    yo = o * lax.rsqrt(ms + EPS) * gn_ref[...] * _silu(z)
    out_ref[...] = yo.astype(out_ref.dtype)


def _gdn(gdn_in, scol, conv_w, a_log, dt_bias, gdn_norm, batch, seq, t_len):
    n_heads = GDN_HEADS
    w = n_heads * HEAD_DIM
    m = batch * seq
    nt = seq // t_len
    chunk = np.arange(t_len) // GDN_CHUNK
    pos = np.arange(t_len)
    samec = chunk[:, None] == chunk[None, :]
    mincl = (samec & (pos[None, :] <= pos[:, None])).astype(np.float32)
    mstrict = (samec & (pos[None, :] < pos[:, None])).astype(np.float32)
    same, hmask = _head_consts(n_heads)
    head = np.arange(w) // HEAD_DIM
    selb = np.zeros((LANES, w), np.float32)
    sela = np.zeros((LANES, w), np.float32)
    selb[head, np.arange(w)] = 1.0
    sela[n_heads + head, np.arange(w)] = 1.0
    alog_b = jnp.repeat(a_log.astype(F32), HEAD_DIM)[None, :]
    dtb_b = jnp.repeat(dt_bias.astype(F32), HEAD_DIM)[None, :]
    alog_r = jnp.zeros((BF16_ROWS, 1), F32).at[n_heads:2 * n_heads, 0].set(a_log.astype(F32))
    dtb_r = jnp.zeros((BF16_ROWS, 1), F32).at[n_heads:2 * n_heads, 0].set(dt_bias.astype(F32))
    gn = jnp.tile(gdn_norm.astype(F32), n_heads)[None, :]
    kern = functools.partial(_gdn_kernel, n_heads=n_heads)
    g = 1
    selt = np.zeros((BF16_ROWS, LANES), np.float32)
    selt[np.arange(2 * n_heads), np.arange(2 * n_heads)] = 1.0
    out = pl.pallas_call(
        kern,
        out_shape=jax.ShapeDtypeStruct((batch, seq, w), BF16),
        grid=(batch // g, nt),
        in_specs=[
            pl.BlockSpec((g, t_len, 4 * w), lambda b, t: (b, t, 0)),
            pl.BlockSpec((g, t_len, LANES), lambda b, t: (b, t, 0)),
            _const_spec((GDN_CONV, 3 * w)),
            _const_spec((1, w)),
            _const_spec((1, w)),
            _const_spec((BF16_ROWS, 1)),
            _const_spec((BF16_ROWS, 1)),
            _const_spec((1, w)),
            _const_spec((t_len, t_len)),
            _const_spec((t_len, t_len)),
            _const_spec((t_len, t_len)),
            _const_spec((t_len, t_len)),
            _const_spec((n_heads, 1, w)),
            _const_spec((w, w)),
            _const_spec((w, w)),
            _const_spec((LANES, w)),
            _const_spec((LANES, w)),
            _const_spec((BF16_ROWS, LANES)),
        ],
        out_specs=pl.BlockSpec((g, t_len, w), lambda b, t: (b, t, 0)),
        scratch_shapes=[pltpu.VMEM((g, w, w), F32), pltpu.VMEM((g, SUBLANES, 3 * w), F32)],
        compiler_params=_params("parallel", "arbitrary"),
        name="gdn",
    )(gdn_in.reshape(batch, seq, 4 * w), scol.reshape(batch, seq, LANES), conv_w.astype(F32),
      alog_b, dtb_b, alog_r, dtb_r, gn,
      jnp.asarray(mincl, BF16), jnp.asarray(mincl.T, BF16), jnp.asarray(mstrict), jnp.asarray(mincl),
      jnp.asarray(hmask), jnp.asarray(same), jnp.eye(w, dtype=BF16), jnp.asarray(sela, BF16),
      jnp.asarray(selb, BF16), jnp.asarray(selt, BF16))
    return out.reshape(m, w)


def _memkv_kernel(mem_ref, nw_ref, wkv_ref, kg_ref, k_ref, v_ref):
    cw = CROSS_HEADS * CROSS_HEAD_DIM
    h = _bf(_rms_rows(mem_ref[...], nw_ref[...]))
    kv = jnp.dot(h, wkv_ref[...], preferred_element_type=F32)
    for hd in range(CROSS_HEADS):
        sl = slice(hd * CROSS_HEAD_DIM, (hd + 1) * CROSS_HEAD_DIM)
        k_ref[:, sl] = _rms_rows(kv[:, sl], kg_ref[...]).astype(k_ref.dtype)
    v_ref[...] = kv[:, cw:].astype(v_ref.dtype)


def _memkv(mem, norm_mem, wkv, k_norm, tm):
    m, d = mem.shape
    cw = CROSS_HEADS * CROSS_HEAD_DIM
    return pl.pallas_call(
        _memkv_kernel,
        out_shape=(jax.ShapeDtypeStruct((m, cw), BF16), jax.ShapeDtypeStruct((m, cw), BF16)),
        grid=(m // tm,),
        in_specs=[
            pl.BlockSpec((tm, d), lambda i: (i, 0)),
            _const_spec((1, d)),
            _const_spec((d, 2 * cw)),
            _const_spec((1, CROSS_HEAD_DIM)),
        ],
        out_specs=(pl.BlockSpec((tm, cw), lambda i: (i, 0)), pl.BlockSpec((tm, cw), lambda i: (i, 0))),
        compiler_params=_params("parallel"),
        name="memkv",
    )(mem, norm_mem.reshape(1, d), _bf(wkv), k_norm.reshape(1, CROSS_HEAD_DIM).astype(F32))


def _cross_kernel(yr_ref, ym_ref, yg_ref, wr_ref, wm_ref, wg_ref, x_ref, nw_ref, wq_ref, qg_ref, k_ref, v_ref,
                  wo_ref, out_ref):
    d = functools.partial(jnp.dot, preferred_element_type=F32)
    x = x_ref[...] + (d(yr_ref[...], wr_ref[...]) + d(ym_ref[...], wm_ref[...]) + d(yg_ref[...], wg_ref[...]))
    h = _bf(_rms_rows(x, nw_ref[...]))
    q = jnp.dot(h, wq_ref[...], preferred_element_type=F32)
    scale = CROSS_HEAD_DIM ** -0.5
    sls = [slice(hd * CROSS_HEAD_DIM, (hd + 1) * CROSS_HEAD_DIM) for hd in range(CROSS_HEADS)]
    qhs = [_rms_rows(q[:, sl], qg_ref[...]) for sl in sls]
    ss = [_mm_nt(qh, k_ref[:, sl]) * scale for qh, sl in zip(qhs, sls)]
    ps = [jnp.exp(s - jnp.max(s, axis=-1, keepdims=True)) for s in ss]
    ls = [jnp.sum(p, axis=-1, keepdims=True) for p in ps]
    outs = [_mm(p, v_ref[:, sl]) / l for p, sl, l in zip(ps, sls, ls)]
    o = _bf(jnp.concatenate(outs, axis=-1))
    out_ref[...] = x + jnp.dot(o, wo_ref[...], preferred_element_type=F32)


def _cross(y_ret, y_moba, y_gdn, w_out, x, norm_w, wq, q_norm, kn, v, wo, batch, seq, mem_len, tm):
    m, d = x.shape
    cw = CROSS_HEADS * CROSS_HEAD_DIM
    nt = seq // tm
    wb = _bf(w_out)
    wr = wb[:RET_W]
    wm = wb[RET_W:RET_W + MOBA_W]
    wg = wb[RET_W + MOBA_W:]
    return pl.pallas_call(
        _cross_kernel,
        out_shape=jax.ShapeDtypeStruct((m, d), F32),
        grid=(batch, nt),
        in_specs=[
            pl.BlockSpec((tm, RET_W), lambda b, t: (b * nt + t, 0)),
            pl.BlockSpec((tm, MOBA_W), lambda b, t: (b * nt + t, 0)),
            pl.BlockSpec((tm, GDN_W), lambda b, t: (b * nt + t, 0)),
            _const_spec((RET_W, d)),
            _const_spec((MOBA_W, d)),
            _const_spec((GDN_W, d)),
            pl.BlockSpec((tm, d), lambda b, t: (b * nt + t, 0)),
            _const_spec((1, d)),
            _const_spec((d, cw)),
            _const_spec((1, CROSS_HEAD_DIM)),
            pl.BlockSpec((mem_len, cw), lambda b, t: (b, 0)),
            pl.BlockSpec((mem_len, cw), lambda b, t: (b, 0)),
            _const_spec((cw, d)),
        ],
        out_specs=pl.BlockSpec((tm, d), lambda b, t: (b * nt + t, 0)),
        compiler_params=_params("parallel", "parallel"),
        name="cross",
    )(y_ret, y_moba, y_gdn, wr, wm, wg, x, norm_w.reshape(1, d), _bf(wq),
      q_norm.reshape(1, CROSS_HEAD_DIM).astype(F32), kn, v, _bf(wo))


def _ffn_kernel(x_ref, xp_ref, nw_ref, wg_ref, wv_ref, cg_ref, cv_ref, bg_ref, bv_ref, wd_ref, out_ref,
                *, tiles_per_seq, n_chunks):
    x = x_ref[...]
    nw = nw_ref[...]
    h = _bf(_rms_rows(x, nw))
    keep = jnp.where(pl.program_id(0) % tiles_per_seq == 0, 0.0, 1.0)
    hp = _bf(_rms_rows(xp_ref[...], nw))
    out_ref[...] = x

    def conv(u, up, cw, b):
        y = u * cw[FFN_CONV - 1:FFN_CONV, :] + b
        for kk in range(1, FFN_CONV):
            y = y + _shift_rows(u, up, kk) * cw[FFN_CONV - 1 - kk:FFN_CONV - kk, :]
        return y

    d = functools.partial(jnp.dot, preferred_element_type=F32)

    def up(c):
        wg = wg_ref[c]
        wv = wv_ref[c]
        return d(h, wg), d(hp, wg) * keep, d(h, wv), d(hp, wv) * keep

    nxt = up(0)
    pending = None
    for c in range(n_chunks):
        ug, ugp, uv, uvp = nxt
        if c + 1 < n_chunks:
            nxt = up(c + 1)
        gate = conv(ug, ugp, cg_ref[c], bg_ref[c])
        val = conv(uv, uvp, cv_ref[c], bv_ref[c])
        down = d(_bf(_silu(gate) * val), wd_ref[c])
        if pending is None and c + 1 < n_chunks:
            pending = down
        else:
            out_ref[...] += down if pending is None else pending + down
            pending = None


def _ffn(x, norm_w, w_up, conv_w, conv_b, w_down, seq, tm, fc):
    m, d = x.shape
    d_ff = w_down.shape[0]
    nc = d_ff // fc
    halo = BF16_ROWS

    def chunks(a):
        return a.reshape(a.shape[0], nc, fc).transpose(1, 0, 2)

    wg = chunks(_bf(w_up[:, :d_ff]))
    wv = chunks(_bf(w_up[:, d_ff:]))
    cg = chunks(conv_w[:, :d_ff].astype(F32))
    cv = chunks(conv_w[:, d_ff:].astype(F32))
    bg = chunks(conv_b[None, :d_ff].astype(F32))
    bv = chunks(conv_b[None, d_ff:].astype(F32))
    wd = _bf(w_down).reshape(nc, fc, d)
    kern = functools.partial(_ffn_kernel, tiles_per_seq=seq // tm, n_chunks=nc)
    return pl.pallas_call(
        kern,
        out_shape=jax.ShapeDtypeStruct((m, d), F32),
        grid=(m // tm,),
        in_specs=[
            pl.BlockSpec((tm, d), lambda i: (i, 0)),
            pl.BlockSpec((halo, d), lambda i: (jnp.maximum(i * (tm // halo) - 1, 0), 0)),
            _const_spec((1, d)),
            _const_spec((nc, d, fc)),
            _const_spec((nc, d, fc)),
            _const_spec((nc, FFN_CONV, fc)),
            _const_spec((nc, FFN_CONV, fc)),
            _const_spec((nc, 1, fc)),
            _const_spec((nc, 1, fc)),
            _const_spec((nc, fc, d)),
        ],
        out_specs=pl.BlockSpec((tm, d), lambda i: (i, 0)),
        compiler_params=_params("parallel"),
        name="ffn",
    )(x, x, norm_w.reshape(1, d), wg, wv, cg, cv, bg, bv, wd)


def _tiles(seq):
    tm = min(512, seq)
    t_len = min(256, seq)
    return tm, t_len


def kernel(x, mem, norm_mix, w_in, ret_norm, moba_q_norm, moba_k_norm, gdn_conv, gdn_a_log, gdn_dt_bias,
           gdn_norm, w_out, norm_cross, norm_mem, cross_wq, cross_wkv, cross_q_norm, cross_k_norm, cross_wo,
           norm_ffn, ffn_up, ffn_conv, ffn_conv_b, ffn_down, rel_bias):
    batch, seq, d = x.shape
    mem_len = mem.shape[1]
    depth = w_in.shape[0]
    tm, t_len = _tiles(seq)
    xf = x.reshape(batch * seq, d)
    memf = mem.reshape(batch * mem_len, d)
    for l in range(depth):
        ret_in, moba_in, gdn_in, scol = _inproj(xf, norm_mix[l], w_in[l], tm)
        y_ret = _retention(ret_in, ret_norm[l], batch, seq, t_len)
        y_moba = _moba(moba_in, moba_q_norm[l], moba_k_norm[l], rel_bias, batch, seq)
        y_gdn = _gdn(gdn_in, scol, gdn_conv[l], gdn_a_log[l], gdn_dt_bias[l], gdn_norm[l],
                     batch, seq, t_len)
        kn, v = _memkv(memf, norm_mem[l], cross_wkv[l], cross_k_norm[l], mem_len)
        xf = _cross(y_ret, y_moba, y_gdn, w_out[l], xf, norm_cross[l], cross_wq[l], cross_q_norm[l], kn, v,
                    cross_wo[l], batch, seq, mem_len, tm)
        xf = _ffn(xf, norm_ffn[l], ffn_up[l], ffn_conv[l], ffn_conv_b[l], ffn_down[l], seq, tm, 256)
    return xf.reshape(batch, seq, d)
```

```python
import functools
import math

import numpy as np
import jax
import jax.numpy as jnp
from jax import lax
from jax.experimental import pallas as pl
from jax.experimental.pallas import tpu as pltpu

F32 = jnp.float32
BF16 = jnp.bfloat16

HEAD_DIM = 64
RET_HEADS = 4
MOBA_HEADS = 8
GDN_HEADS = 4
RET_W = RET_HEADS * HEAD_DIM
MOBA_W = MOBA_HEADS * HEAD_DIM
GDN_W = GDN_HEADS * HEAD_DIM
ROPE_BASE = 10000.0
MOBA_BLOCK = 256
MOBA_TOPK = 3
REL_BUCKETS = 32
REL_MAX_DIST = 128
GDN_CHUNK = 64
GDN_CONV = 4
CROSS_HEADS = 4
CROSS_HEAD_DIM = 128
FFN_CONV = 3
EPS = 1e-6

LANES = 128
SUBLANES = 8
BF16_ROWS = 16
VMEM_LIMIT = 56 * 1024 * 1024


def _bf(a):
    return a.astype(BF16)


def _mm(a, b):
    return jnp.dot(_bf(a), _bf(b), preferred_element_type=F32)


def _mm_nt(a, b):
    return lax.dot_general(_bf(a), _bf(b), (((1,), (1,)), ((), ())), preferred_element_type=F32)


def _split(a):
    hi = a.astype(BF16)
    lo = (a - hi.astype(F32)).astype(BF16)
    return hi, lo


def _mm3(a, b):
    ah, al = _split(a)
    bh, bl = _split(b)
    d = functools.partial(jnp.dot, preferred_element_type=F32)
    return d(ah, bh) + (d(ah, bl) + d(al, bh))


def _mm3_nt(a, b):
    ah, al = _split(a)
    bh, bl = _split(b)
    d = functools.partial(lax.dot_general, dimension_numbers=(((1,), (1,)), ((), ())),
                          preferred_element_type=F32)
    return d(ah, bh) + (d(ah, bl) + d(al, bh))


def _mm2_l(a, b_exact):
    ah, al = _split(a)
    d = functools.partial(jnp.dot, preferred_element_type=F32)
    return d(ah, b_exact) + d(al, b_exact)


def _mm2_r(a_exact, b):
    bh, bl = _split(b)
    d = functools.partial(jnp.dot, preferred_element_type=F32)
    return d(a_exact, bh) + d(a_exact, bl)


def _sigmoid(x):
    return 1.0 / (1.0 + jnp.exp(-x))


def _silu(x):
    return x * _sigmoid(x)


def _softplus(x):
    return jnp.maximum(x, 0.0) + jnp.log1p(jnp.exp(-jnp.abs(x)))


def _rms_rows(x, w):
    return x * lax.rsqrt(jnp.mean(x * x, axis=-1, keepdims=True) + EPS) * w


def _shift_rows(u, prev, k):
    r = pltpu.roll(u, k, axis=0)
    rp = pltpu.roll(prev, k, axis=0)[:SUBLANES]
    row = lax.broadcasted_iota(jnp.int32, (SUBLANES, u.shape[1]), 0)
    top = jnp.where(row < k, rp, r[:SUBLANES])
    return jnp.concatenate([top, r[SUBLANES:]], axis=0)


def _const_spec(shape):
    nd = len(shape)
    return pl.BlockSpec(shape, lambda *_: (0,) * nd)


def _params(*sem):
    return pltpu.CompilerParams(dimension_semantics=sem, vmem_limit_bytes=VMEM_LIMIT)


def _inproj_kernel(x_ref, nw_ref, w_ref, ws_ref, ret_ref, moba_ref, gdn_ref, scol_ref):
    x = x_ref[...]
    hf = _rms_rows(x, nw_ref[...])
    h = _bf(hf)
    off = 0
    for ref in (ret_ref, moba_ref, gdn_ref):
        width = ref.shape[1]
        for c in range(0, width, 512):
            ref[:, c:c + 512] = jnp.dot(h, w_ref[:, off + c:off + c + 512],
                                        preferred_element_type=F32).astype(ref.dtype)
        off += width
    scol_ref[...] = _mm3(hf, ws_ref[...])


def _inproj(x, norm_w, w_in, tm):
    m, d = x.shape
    main = 4 * RET_W + 3 * MOBA_W + 4 * GDN_W
    w_main = _bf(w_in[:, :main])
    w_small = jnp.zeros((d, LANES), F32).at[:, :2 * GDN_HEADS].set(w_in[:, main:])
    out_shape = (
        jax.ShapeDtypeStruct((m, 4 * RET_W), BF16),
        jax.ShapeDtypeStruct((m, 3 * MOBA_W), BF16),
        jax.ShapeDtypeStruct((m, 4 * GDN_W), BF16),
        jax.ShapeDtypeStruct((m, LANES), F32),
    )
    return pl.pallas_call(
        _inproj_kernel,
        out_shape=out_shape,
        grid=(m // tm,),
        in_specs=[
            pl.BlockSpec((tm, d), lambda i: (i, 0)),
            _const_spec((1, d)),
            _const_spec((d, main)),
            _const_spec((d, LANES)),
        ],
        out_specs=(
            pl.BlockSpec((tm, 4 * RET_W), lambda i: (i, 0)),
            pl.BlockSpec((tm, 3 * MOBA_W), lambda i: (i, 0)),
            pl.BlockSpec((tm, 4 * GDN_W), lambda i: (i, 0)),
            pl.BlockSpec((tm, LANES), lambda i: (i, 0)),
        ),
        compiler_params=_params("parallel"),
        name="inproj",
    )(x, norm_w.reshape(1, d), w_main, w_small)


def _head_consts(n_heads):
    w = n_heads * HEAD_DIM
    head = np.arange(w) // HEAD_DIM
    same = (head[:, None] == head[None, :]).astype(np.float32)
    hmask = (head[None, None, :] == np.arange(n_heads)[:, None, None]).astype(np.float32)
    return same, hmask


def _ret_kernel(in_ref, cos_ref, sin_ref, gn_ref, rot_ref, eye_ref, dmat_ref, xi_ref, zeta_ref, gt_ref,
                hmask_ref, same_ref, out_ref, s_ref, *, n_heads):
    w = n_heads * HEAD_DIM
    t_len = in_ref.shape[0]

    @pl.when(pl.program_id(1) == 0)
    def _():
        s_ref[...] = jnp.zeros_like(s_ref)

    xin = in_ref[...]
    q = xin[:, :w]
    k = xin[:, w:2 * w]
    v = xin[:, 2 * w:3 * w]
    g = xin[:, 3 * w:].astype(F32)
    cos = cos_ref[...]
    sin = sin_ref[...]
    rot = rot_ref[...]
    qr = q.astype(F32) * cos + jnp.dot(q, rot, preferred_element_type=F32) * sin
    kr = k.astype(F32) * cos + jnp.dot(k, rot, preferred_element_type=F32) * sin
    same = same_ref[...]
    k_t = _mm_nt(eye_ref[...], kr)
    state = s_ref[...]
    o = _mm(qr, state) * xi_ref[...]
    vf = v.astype(F32)
    hms = [hmask_ref[h] for h in range(n_heads)]
    ss = [_mm_nt(qr * hm, kr) * dmat_ref[h] for h, hm in enumerate(hms)]
    for s, hm in zip(ss, hms):
        o = o + _mm(s, vf * hm)
    zr = jnp.concatenate(
        [jnp.broadcast_to(zeta_ref[h:h + 1, :], (HEAD_DIM, t_len)) for h in range(n_heads)], axis=0)
    s_ref[...] = state * gt_ref[...] + _mm(k_t * zr, v) * same
    ms = _mm2_l(o * o, _bf(same)) * (1.0 / HEAD_DIM)
    y = o * lax.rsqrt(ms + EPS) * gn_ref[...] * _silu(g)
    out_ref[...] = y.astype(out_ref.dtype)


def _retention(ret_in, ret_norm, batch, seq, t_len):
    n_heads = RET_HEADS
    w = n_heads * HEAD_DIM
    m = batch * seq
    nt = seq // t_len
    half = HEAD_DIM // 2
    pos = np.arange(seq, dtype=np.float32)
    inv_freq = (ROPE_BASE ** (-jnp.arange(half, dtype=F32) / half))
    ang = jnp.asarray(pos)[:, None] * inv_freq[None, :]
    cos = jnp.tile(jnp.concatenate([jnp.cos(ang)] * 2, axis=-1), (1, n_heads))
    sin = jnp.tile(jnp.concatenate([jnp.sin(ang)] * 2, axis=-1), (1, n_heads))
    j = np.arange(w)
    rot = np.zeros((w, w), np.float32)
    first = (j % HEAD_DIM) < half
    rot[j[first] + half, j[first]] = -1.0
    rot[j[~first] - half, j[~first]] = 1.0
    same, hmask = _head_consts(n_heads)
    log_gamma = jnp.log1p(-jnp.exp2(-5.0 - jnp.arange(n_heads, dtype=F32)))
    idx = jnp.arange(t_len, dtype=F32)
    diff = idx[:, None] - idx[None, :]
    scale = HEAD_DIM ** -0.5
    dmat = jnp.where(diff >= 0, jnp.exp(log_gamma[:, None, None] * jnp.maximum(diff, 0.0)), 0.0) * scale
    xi = jnp.repeat(jnp.exp(log_gamma[:, None] * (idx + 1.0)).T, HEAD_DIM, axis=1)
    zeta = jnp.exp(log_gamma[:, None] * (t_len - 1.0 - idx)) * scale
    zeta = jnp.zeros((SUBLANES, t_len), F32).at[:n_heads].set(zeta)
    g_chunk = jnp.repeat(jnp.exp(log_gamma * t_len), HEAD_DIM)[None, :]
    gn = jnp.tile(ret_norm.astype(F32), n_heads)[None, :]
    kern = functools.partial(_ret_kernel, n_heads=n_heads)
    return pl.pallas_call(
        kern,
        out_shape=jax.ShapeDtypeStruct((m, w), BF16),
        grid=(batch, nt),
        in_specs=[
            pl.BlockSpec((t_len, 4 * w), lambda b, t: (b * nt + t, 0)),
            pl.BlockSpec((t_len, w), lambda b, t: (t, 0)),
            pl.BlockSpec((t_len, w), lambda b, t: (t, 0)),
            _const_spec((1, w)),
            _const_spec((w, w)),
            _const_spec((w, w)),
            _const_spec((n_heads, t_len, t_len)),
            _const_spec((t_len, w)),
            _const_spec((SUBLANES, t_len)),
            _const_spec((1, w)),
            _const_spec((n_heads, 1, w)),
            _const_spec((w, w)),
        ],
        out_specs=pl.BlockSpec((t_len, w), lambda b, t: (b * nt + t, 0)),
        scratch_shapes=[pltpu.VMEM((w, w), F32)],
        compiler_params=_params("parallel", "arbitrary"),
        name="retention",
    )(ret_in, cos, sin, gn, jnp.asarray(rot, BF16), jnp.eye(w, dtype=BF16), dmat, xi, zeta, g_chunk,
      jnp.asarray(hmask), jnp.asarray(same))


def _t5_bucket(rel):
    n = np.maximum(rel, 0)
    exact = REL_BUCKETS // 2
    nf = np.maximum(n, exact).astype(np.float32)
    ratio = np.log(nf / np.float32(exact)) / np.float32(math.log(REL_MAX_DIST / exact))
    large = exact + (ratio * np.float32(REL_BUCKETS - exact)).astype(np.int32)
    return np.where(n < exact, n, np.minimum(large, REL_BUCKETS - 1)).astype(np.int32)


MOBA_MASK = -1e30
MOBA_EXP_LIMIT = 1e37


def _moba_kernel(q_ref, k_ref, v_ref, idown_ref, idprev_ref, rb_ref, qg_ref, kg_ref, same_ref,
                 eye_ref, out_ref, kaug_ref, vaug_ref, kmean_ref, bias_ref, *, n_blocks, far_bucket):
    blk = MOBA_BLOCK
    hd = HEAD_DIM
    qi = pl.program_id(2)
    head0 = 2 * pl.program_id(0)
    same = same_ref[...]
    inv_d = 1.0 / hd
    scale = hd ** -0.5
    lane = lax.broadcasted_iota(jnp.int32, (1, 2 * hd), 1)
    in_h0 = lane < hd
    both = range(2)
    mine = [in_h0, jnp.logical_not(in_h0)]
    blk_row = lax.broadcasted_iota(jnp.int32, (n_blocks, blk), 0)

    def augmented_queries(q_blk, j):
        qf = q_blk.astype(F32)
        qn = qf * lax.rsqrt(_mm2_l(qf * qf, same) * inv_d + EPS) * qg_ref[...]
        km = kmean_ref[...]
        gates = [jnp.where(blk_row < j, _mm3_nt(jnp.where(mine[e], km, 0.0), qn), -jnp.inf) for e in both]
        ranks = [jnp.zeros_like(g) for g in gates]
        for i in range(n_blocks):
            tie = jnp.where(blk_row > i, 1.0, 0.0)
            ranks = [r + jnp.where(g[i:i + 1, :] > g, 1.0, jnp.where(g[i:i + 1, :] == g, tie, 0.0))
                     for r, g in zip(ranks, gates)]
        mask_t = [jnp.where(((blk_row < j) & (r < MOBA_TOPK)) | (blk_row == j), 0.0, MOBA_MASK)
                  for r in ranks]
        pad = jnp.zeros((hd - n_blocks - SUBLANES, blk), F32)
        zq = jnp.zeros((hd, blk), F32)
        extras = []
        for e in both:
            far = jnp.full((1, blk), rb_ref[head0 + e, far_bucket], F32)
            far_hi = far.astype(BF16).astype(F32)
            crow = jnp.concatenate([far_hi, far - far_hi, jnp.zeros((SUBLANES - 2, blk), F32)], axis=0)
            extras.append([zq, mask_t[e], crow, pad] if e == 0 else [mask_t[e], crow, pad, zq])
        extras = [_mm_nt(eye_ref[...], jnp.concatenate(parts, axis=0)) for parts in extras]
        return [jnp.where(mine[e], qn * scale, 0.0) + extras[e] for e in both]

    @pl.when((qi == 0) & (pl.program_id(1) == 0))
    def _():
        for e in both:
            for tab, ids_ref in enumerate((idown_ref, idprev_ref)):
                ids = ids_ref[...]
                tile = jnp.zeros((blk, blk), F32)
                for b in range(REL_BUCKETS):
                    tile = jnp.where(ids == b, rb_ref[head0 + e, b], tile)
                bias_ref[tab, e] = tile - rb_ref[head0 + e, far_bucket]

    @pl.when(qi == 0)
    def _():
        def body(j, c):
            r0 = pl.multiple_of(j * blk, blk)
            kb = k_ref[pl.ds(r0, blk), :].astype(F32)
            kn = kb * lax.rsqrt(_mm2_l(kb * kb, same) * inv_d + EPS) * kg_ref[...]
            kmean_ref[pl.ds(j, 1), :] = jnp.mean(kn, axis=0, keepdims=True)
            vb = v_ref[pl.ds(r0, blk), :].astype(F32)
            for e in both:
                xo = (1 - e) * hd
                ones = (lane >= xo + n_blocks) & (lane < xo + n_blocks + 3)
                ext = jnp.where((lane == xo + j) | ones, 1.0, 0.0)
                kaug_ref[e, pl.ds(r0, blk), :] = jnp.where(mine[e], kn, ext).astype(BF16)
                vaug_ref[e, pl.ds(r0, blk), :] = jnp.where(mine[e], vb, 1.0).astype(BF16)
            return c
        lax.fori_loop(0, n_blocks, body, 0)

    row = lax.broadcasted_iota(jnp.int32, (blk, blk), 0)
    col = lax.broadcasted_iota(jnp.int32, (blk, blk), 1)
    q0 = pl.multiple_of(qi * blk, blk)
    qp = pl.multiple_of(jnp.maximum(qi - 1, 0) * blk, blk)
    qbase = augmented_queries(q_ref[...], qi)
    bown = [bias_ref[0, e] for e in both]
    bprev = [bias_ref[1, e] for e in both]

    def scores(qa, e, r0, bias=None):
        s = _mm_nt(qa, kaug_ref[e, pl.ds(r0, blk), :])
        return s if bias is None else s + bias

    def finish(accs):
        o = [acc / pltpu.roll(acc, hd, axis=1) for acc in accs]
        return jnp.where(in_h0, o[0], o[1])

    k_own = [kaug_ref[e, pl.ds(q0, blk), :] for e in both]
    shifts = [jnp.sum(jnp.where(mine[e], qbase[e] * k_own[e].astype(F32), 0.0), axis=-1, keepdims=True)
              .astype(BF16).astype(F32) for e in both]
    q_shift = [_bf(qbase[e] + jnp.where(lane == (1 - e) * hd + n_blocks + 2, -shifts[e], 0.0)) for e in both]

    def fast_tiles(accs, starts):
        ss = [[scores(q_shift[e], e, r0) for e in both] for r0 in starts]
        for s, r0 in zip(ss, starts):
            accs = tuple(accs[e] + _mm(jnp.exp(s[e]), vaug_ref[e, pl.ds(r0, blk), :]) for e in both)
        return accs

    def fast_tile(accs, r0):
        return fast_tiles(accs, [r0])

    no_prev = jnp.where(qi >= 1, 0.0, MOBA_MASK)
    s_own = [jnp.where(col <= row, scores(q_shift[e], e, q0, bown[e]), -jnp.inf) for e in both]
    s_prev = [scores(q_shift[e], e, qp, bprev[e]) + no_prev for e in both]
    accs = tuple(_mm(jnp.exp(s_own[e]), vaug_ref[e, pl.ds(q0, blk), :])
                 + _mm(jnp.exp(s_prev[e]), vaug_ref[e, pl.ds(qp, blk), :]) for e in both)
    n_far = jnp.maximum(qi - 1, 0)

    def starts(first, count):
        return [pl.multiple_of((first + u) * blk, blk) for u in range(count)]

    group = 4
    accs = lax.fori_loop(0, n_far // group, lambda t, a: fast_tiles(a, starts(group * t, group)), accs)
    done = (n_far // group) * group
    left = n_far - done
    accs = lax.cond(left >= 2, lambda a: fast_tiles(a, starts(done, 2)), lambda a: a, accs)
    last = jnp.maximum(n_far - 1, 0)
    accs = lax.cond(left % 2 == 1, lambda a: fast_tile(a, starts(last, 1)[0]), lambda a: a, accs)

    unsafe = sum(jnp.sum(jnp.where(jnp.abs(acc) < MOBA_EXP_LIMIT, 0.0, 1.0)) for acc in accs)
    overflowed = unsafe > 0.0

    def safe_path(_):
        def update(carry, s, vaug):
            m_i, acc = carry
            m_n = jnp.maximum(m_i, jnp.max(s, axis=-1, keepdims=True))
            return m_n, jnp.exp(m_i - m_n) * acc + _mm(jnp.exp(s - m_n), vaug)

        def block(j, c):
            r0 = pl.multiple_of(j * blk, blk)
            out = []
            for e in range(2):
                s = scores(_bf(qbase[e]), e, r0) + jnp.where(j == qi - 1, bprev[e], 0.0)
                out.append(update(c[e], s, vaug_ref[e, pl.ds(r0, blk), :]))
            return tuple(out)

        init = []
        for e in range(2):
            s = jnp.where(col <= row, scores(_bf(qbase[e]), e, q0, bown[e]), -jnp.inf)
            m0 = jnp.max(s, axis=-1, keepdims=True)
            init.append((m0, _mm(jnp.exp(s - m0), vaug_ref[e, pl.ds(q0, blk), :])))
        c = lax.fori_loop(0, qi, block, tuple(init))
        return finish([c[0][1], c[1][1]])

    out = lax.cond(overflowed, safe_path, lambda _: finish(accs), 0)
    out_ref[...] = out.astype(out_ref.dtype)


def _moba_bucket_tables(seq):
    blk = MOBA_BLOCK
    d = np.arange(blk, dtype=np.int32)[:, None] - np.arange(blk, dtype=np.int32)[None, :]
    far = _t5_bucket(np.arange(blk + 1, max(seq, blk + 2), dtype=np.int32))
    assert (far == far[0]).all()
    return _t5_bucket(d), _t5_bucket(d + blk), int(far[0])


def _moba(moba_in, q_norm, k_norm, rel_bias, batch, seq):
    m = batch * seq
    blk = MOBA_BLOCK
    nb = seq // blk
    assert nb % SUBLANES == 0 and nb + SUBLANES <= HEAD_DIM
    pairs = MOBA_HEADS // 2
    pw = 2 * HEAD_DIM
    ids_own, ids_prev, far_bucket = _moba_bucket_tables(seq)
    same, _ = _head_consts(2)
    qg = jnp.tile(q_norm.astype(F32), 2)[None, :]
    kg = jnp.tile(k_norm.astype(F32), 2)[None, :]
    kern = functools.partial(_moba_kernel, n_blocks=nb, far_bucket=far_bucket)
    return pl.pallas_call(
        kern,
        out_shape=jax.ShapeDtypeStruct((m, MOBA_W), BF16),
        grid=(pairs, batch, nb),
        in_specs=[
            pl.BlockSpec((blk, pw), lambda p, b, i: (b * nb + i, p)),
            pl.BlockSpec((seq, pw), lambda p, b, i: (b, pairs + p)),
            pl.BlockSpec((seq, pw), lambda p, b, i: (b, 2 * pairs + p)),
            _const_spec((blk, blk)),
            _const_spec((blk, blk)),
            pl.BlockSpec(memory_space=pltpu.SMEM),
            _const_spec((1, pw)),
            _const_spec((1, pw)),
            _const_spec((pw, pw)),
            _const_spec((blk, blk)),
        ],
        out_specs=pl.BlockSpec((blk, pw), lambda p, b, i: (b * nb + i, p)),
        scratch_shapes=[pltpu.VMEM((2, seq, pw), BF16), pltpu.VMEM((2, seq, pw), BF16),
                        pltpu.VMEM((nb, pw), F32), pltpu.VMEM((2, 2, blk, blk), F32)],
        compiler_params=_params("arbitrary", "arbitrary", "arbitrary"),
        name="moba",
    )(moba_in, moba_in, moba_in, jnp.asarray(ids_own), jnp.asarray(ids_prev), rel_bias.astype(F32),
      qg, kg, jnp.asarray(same, BF16), jnp.eye(blk, dtype=BF16))


def _tri_inverse(ns, eye):
    xs = [eye - n for n in ns]
    ps = list(ns)
    steps = int(math.log2(GDN_CHUNK)) - 1
    for _ in range(steps):
        ps = [_mm(p, p) for p in ps]
        xs = [x + _mm(x, p) for x, p in zip(xs, ps)]
    rs = [eye - (x + _mm3(n, x)) for n, x in zip(ns, xs)]
    return [x + _mm(x, r) for x, r in zip(xs, rs)]


def _gdn_kernel(in_ref, scol_ref, *rest, n_heads):
    consts, (out_ref, s_ref, tail_ref) = rest[:-3], rest[-3:]

    @pl.when(pl.program_id(1) == 0)
    def _():
        s_ref[...] = jnp.zeros_like(s_ref)
        tail_ref[...] = jnp.zeros_like(tail_ref)

    for i in range(in_ref.shape[0]):
        _gdn_block(in_ref.at[i], scol_ref.at[i], *consts,
                   out_ref.at[i], s_ref.at[i], tail_ref.at[i], n_heads=n_heads)


def _gdn_block(in_ref, scol_ref, cw_ref, alog_b_ref, dtb_b_ref, alog_r_ref, dtb_r_ref, gn_ref,
               lblk_ref, ublk_ref, mstrict_ref, mincl_ref, hmask_ref, same_ref, eye_ref, sela_ref, selb_ref,
               selt_ref, out_ref, s_ref, tail_ref, *, n_heads):
    w = n_heads * HEAD_DIM
    t_len = in_ref.shape[0]
    c_len = GDN_CHUNK
    neg_inf = -jnp.inf

    xin = in_ref[...].astype(F32)
    raw = xin[:, :3 * w]
    z = xin[:, 3 * w:]
    prev = tail_ref[...]
    cw = cw_ref[...]
    acc = raw * cw[GDN_CONV - 1:GDN_CONV, :]
    for kk in range(1, GDN_CONV):
        acc = acc + _shift_rows(raw, prev, kk) * cw[GDN_CONV - 1 - kk:GDN_CONV - kk, :]
    tail_ref[...] = raw[t_len - SUBLANES:, :]
    y = _silu(acc)
    q = y[:, :w]
    k = y[:, w:2 * w]
    v = y[:, 2 * w:]
    same = same_ref[...]
    same_b = _bf(same)
    qn = q * lax.rsqrt(_mm2_l(q * q, same_b) + EPS) * (HEAD_DIM ** -0.5)
    kn = k * lax.rsqrt(_mm2_l(k * k, same_b) + EPS)

    sc = scol_ref[...]
    beta_b = _sigmoid(_mm2_l(sc, selb_ref[...]))
    ld_b = -jnp.exp(alog_b_ref[...]) * _softplus(_mm2_l(sc, sela_ref[...]) + dtb_b_ref[...])
    g_b = _mm2_r(lblk_ref[...], ld_b)
    sc_hi = sc.astype(BF16)
    sc_r = sc - sc_hi.astype(F32)
    sc_mid = sc_r.astype(BF16)
    sc_lo = (sc_r - sc_mid.astype(F32)).astype(BF16)
    sr = sum(_mm_nt(selt_ref[...], part) for part in (sc_hi, sc_mid, sc_lo))
    ld_r = -jnp.exp(alog_r_ref[...]) * _softplus(sr + dtb_r_ref[...])
    g_r = _mm2_l(ld_r, ublk_ref[...])
    rem_r = _mm2_l(ld_r, _bf(mstrict_ref[...]))

    eye = eye_ref[...]
    k_t = _mm_nt(eye, kn)
    rem_rows = jnp.concatenate(
        [jnp.broadcast_to(rem_r[n_heads + h:n_heads + h + 1, :], (HEAD_DIM, t_len)) for h in range(n_heads)],
        axis=0)
    kdec_t = k_t * jnp.exp(rem_rows)
    eg = jnp.exp(g_b)
    qdec = qn * eg
    wrhs = kn * beta_b * eg
    urhs = v * beta_b
    mstrict = mstrict_ref[...]
    mincl = mincl_ref[...]
    eye_f = eye.astype(F32)
    u = jnp.zeros((t_len, w), F32)
    wv = jnp.zeros((t_len, w), F32)
    heads = range(n_heads)
    hms = [hmask_ref[h] for h in heads]
    decs = [jnp.exp(jnp.where(mincl > 0.0, g_b[:, h * HEAD_DIM:h * HEAD_DIM + 1]
                              - g_r[n_heads + h:n_heads + h + 1, :], neg_inf)) for h in heads]
    kks = [_mm_nt(kn * hms[h], kn) for h in heads]
    ns = [jnp.where(mstrict > 0.0, beta_b[:, h * HEAD_DIM:h * HEAD_DIM + 1] * kks[h] * decs[h], 0.0)
          for h in heads]
    t_invs = _tri_inverse(ns, eye_f)
    for h in heads:
        for part in _split(t_invs[h]):
            u = u + _mm(part, urhs * hms[h])
            wv = wv + _mm(part, wrhs * hms[h])
    qks = [_mm_nt(qn * hms[h], kn) * decs[h] for h in heads]

    state = s_ref[...]
    vnews = []
    ointer = []
    for c in range(t_len // c_len):
        rc = slice(c * c_len, (c + 1) * c_len)
        from_state = _mm(jnp.concatenate([wv[rc], qdec[rc]], axis=0), state)
        vnew = u[rc] - from_state[:c_len]
        vnews.append(vnew)
        ointer.append(from_state[c_len:])
        gl = eg[(c + 1) * c_len - 1:(c + 1) * c_len, :]
        state = state * gl + _mm(kdec_t[:, rc], vnew) * same
    s_ref[...] = state
    vn = jnp.concatenate(vnews, axis=0)
    o = jnp.concatenate(ointer, axis=0)
    for h in range(n_heads):
        o = o + _mm(qks[h], vn * hmask_ref[h])
    ms = _mm2_l(o * o, same_b) * (1.0 / HEAD_DIM)
    yo = o * lax.rsqrt(ms + EPS) * gn_ref[...] * _silu(z)
    out_ref[...] = yo.astype(out_ref.dtype)


def _gdn(gdn_in, scol, conv_w, a_log, dt_bias, gdn_norm, batch, seq, t_len):
    n_heads = GDN_HEADS
    w = n_heads * HEAD_DIM
    m = batch * seq
    nt = seq // t_len
    chunk = np.arange(t_len) // GDN_CHUNK
    pos = np.arange(t_len)
    samec = chunk[:, None] == chunk[None, :]
    mincl = (samec & (pos[None, :] <= pos[:, None])).astype(np.float32)
    mstrict = (samec & (pos[None, :] < pos[:, None])).astype(np.float32)
    same, hmask = _head_consts(n_heads)
    head = np.arange(w) // HEAD_DIM
    selb = np.zeros((LANES, w), np.float32)
    sela = np.zeros((LANES, w), np.float32)
    selb[head, np.arange(w)] = 1.0
    sela[n_heads + head, np.arange(w)] = 1.0
    alog_b = jnp.repeat(a_log.astype(F32), HEAD_DIM)[None, :]
    dtb_b = jnp.repeat(dt_bias.astype(F32), HEAD_DIM)[None, :]
    alog_r = jnp.zeros((BF16_ROWS, 1), F32).at[n_heads:2 * n_heads, 0].set(a_log.astype(F32))
    dtb_r = jnp.zeros((BF16_ROWS, 1), F32).at[n_heads:2 * n_heads, 0].set(dt_bias.astype(F32))
    gn = jnp.tile(gdn_norm.astype(F32), n_heads)[None, :]
    kern = functools.partial(_gdn_kernel, n_heads=n_heads)
    g = 1
    selt = np.zeros((BF16_ROWS, LANES), np.float32)
    selt[np.arange(2 * n_heads), np.arange(2 * n_heads)] = 1.0
    out = pl.pallas_call(
        kern,
        out_shape=jax.ShapeDtypeStruct((batch, seq, w), BF16),
        grid=(batch // g, nt),
        in_specs=[
            pl.BlockSpec((g, t_len, 4 * w), lambda b, t: (b, t, 0)),
            pl.BlockSpec((g, t_len, LANES), lambda b, t: (b, t, 0)),
            _const_spec((GDN_CONV, 3 * w)),
            _const_spec((1, w)),
            _const_spec((1, w)),
            _const_spec((BF16_ROWS, 1)),
            _const_spec((BF16_ROWS, 1)),
            _const_spec((1, w)),
            _const_spec((t_len, t_len)),
            _const_spec((t_len, t_len)),
            _const_spec((t_len, t_len)),
            _const_spec((t_len, t_len)),
            _const_spec((n_heads, 1, w)),
            _const_spec((w, w)),
            _const_spec((w, w)),
            _const_spec((LANES, w)),
            _const_spec((LANES, w)),
            _const_spec((BF16_ROWS, LANES)),
        ],
        out_specs=pl.BlockSpec((g, t_len, w), lambda b, t: (b, t, 0)),
        scratch_shapes=[pltpu.VMEM((g, w, w), F32), pltpu.VMEM((g, SUBLANES, 3 * w), F32)],
        compiler_params=_params("parallel", "arbitrary"),
        name="gdn",
    )(gdn_in.reshape(batch, seq, 4 * w), scol.reshape(batch, seq, LANES), conv_w.astype(F32),
      alog_b, dtb_b, alog_r, dtb_r, gn,
      jnp.asarray(mincl, BF16), jnp.asarray(mincl.T, BF16), jnp.asarray(mstrict), jnp.asarray(mincl),
      jnp.asarray(hmask), jnp.asarray(same), jnp.eye(w, dtype=BF16), jnp.asarray(sela, BF16),
      jnp.asarray(selb, BF16), jnp.asarray(selt, BF16))
    return out.reshape(m, w)


def _memkv_kernel(mem_ref, nw_ref, wkv_ref, kg_ref, k_ref, v_ref):
    cw = CROSS_HEADS * CROSS_HEAD_DIM
    h = _bf(_rms_rows(mem_ref[...], nw_ref[...]))
    kv = jnp.dot(h, wkv_ref[...], preferred_element_type=F32)
    for hd in range(CROSS_HEADS):
        sl = slice(hd * CROSS_HEAD_DIM, (hd + 1) * CROSS_HEAD_DIM)
        k_ref[:, sl] = _rms_rows(kv[:, sl], kg_ref[...]).astype(k_ref.dtype)
    v_ref[...] = kv[:, cw:].astype(v_ref.dtype)


def _memkv(mem, norm_mem, wkv, k_norm, tm):
    m, d = mem.shape
    cw = CROSS_HEADS * CROSS_HEAD_DIM
    return pl.pallas_call(
        _memkv_kernel,
        out_shape=(jax.ShapeDtypeStruct((m, cw), BF16), jax.ShapeDtypeStruct((m, cw), BF16)),
        grid=(m // tm,),
        in_specs=[
            pl.BlockSpec((tm, d), lambda i: (i, 0)),
            _const_spec((1, d)),
            _const_spec((d, 2 * cw)),
            _const_spec((1, CROSS_HEAD_DIM)),
        ],
        out_specs=(pl.BlockSpec((tm, cw), lambda i: (i, 0)), pl.BlockSpec((tm, cw), lambda i: (i, 0))),
        compiler_params=_params("parallel"),
        name="memkv",
    )(mem, norm_mem.reshape(1, d), _bf(wkv), k_norm.reshape(1, CROSS_HEAD_DIM).astype(F32))


def _cross_kernel(yr_ref, ym_ref, yg_ref, wr_ref, wm_ref, wg_ref, x_ref, nw_ref, wq_ref, qg_ref, k_ref, v_ref,
                  wo_ref, out_ref):
    d = functools.partial(jnp.dot, preferred_element_type=F32)
    x = x_ref[...] + (d(yr_ref[...], wr_ref[...]) + d(ym_ref[...], wm_ref[...]) + d(yg_ref[...], wg_ref[...]))
    h = _bf(_rms_rows(x, nw_ref[...]))
    q = jnp.dot(h, wq_ref[...], preferred_element_type=F32)
    scale = CROSS_HEAD_DIM ** -0.5
    sls = [slice(hd * CROSS_HEAD_DIM, (hd + 1) * CROSS_HEAD_DIM) for hd in range(CROSS_HEADS)]
    qhs = [_rms_rows(q[:, sl], qg_ref[...]) for sl in sls]
    ss = [_mm_nt(qh, k_ref[:, sl]) * scale for qh, sl in zip(qhs, sls)]
    ps = [jnp.exp(s - jnp.max(s, axis=-1, keepdims=True)) for s in ss]
    ls = [jnp.sum(p, axis=-1, keepdims=True) for p in ps]
    outs = [_mm(p, v_ref[:, sl]) / l for p, sl, l in zip(ps, sls, ls)]
    o = _bf(jnp.concatenate(outs, axis=-1))
    out_ref[...] = x + jnp.dot(o, wo_ref[...], preferred_element_type=F32)


def _cross(y_ret, y_moba, y_gdn, w_out, x, norm_w, wq, q_norm, kn, v, wo, batch, seq, mem_len, tm):
    m, d = x.shape
    cw = CROSS_HEADS * CROSS_HEAD_DIM
    nt = seq // tm
    wb = _bf(w_out)
    wr = wb[:RET_W]
    wm = wb[RET_W:RET_W + MOBA_W]
    wg = wb[RET_W + MOBA_W:]
    return pl.pallas_call(
        _cross_kernel,
        out_shape=jax.ShapeDtypeStruct((m, d), F32),
        grid=(batch, nt),
        in_specs=[
            pl.BlockSpec((tm, RET_W), lambda b, t: (b * nt + t, 0)),
            pl.BlockSpec((tm, MOBA_W), lambda b, t: (b * nt + t, 0)),
            pl.BlockSpec((tm, GDN_W), lambda b, t: (b * nt + t, 0)),
            _const_spec((RET_W, d)),
            _const_spec((MOBA_W, d)),
            _const_spec((GDN_W, d)),
            pl.BlockSpec((tm, d), lambda b, t: (b * nt + t, 0)),
            _const_spec((1, d)),
            _const_spec((d, cw)),
            _const_spec((1, CROSS_HEAD_DIM)),
            pl.BlockSpec((mem_len, cw), lambda b, t: (b, 0)),
            pl.BlockSpec((mem_len, cw), lambda b, t: (b, 0)),
            _const_spec((cw, d)),
        ],
        out_specs=pl.BlockSpec((tm, d), lambda b, t: (b * nt + t, 0)),
        compiler_params=_params("parallel", "parallel"),
        name="cross",
    )(y_ret, y_moba, y_gdn, wr, wm, wg, x, norm_w.reshape(1, d), _bf(wq),
      q_norm.reshape(1, CROSS_HEAD_DIM).astype(F32), kn, v, _bf(wo))


def _ffn_kernel(x_ref, xp_ref, nw_ref, wg_ref, wv_ref, cg_ref, cv_ref, bg_ref, bv_ref, wd_ref, out_ref,
                *, tiles_per_seq, n_chunks):
    x = x_ref[...]
    nw = nw_ref[...]
    h = _bf(_rms_rows(x, nw))
    keep = jnp.where(pl.program_id(0) % tiles_per_seq == 0, 0.0, 1.0)
    hp = _bf(_rms_rows(xp_ref[...], nw))
    out_ref[...] = x

    def conv(u, up, cw, b):
        y = u * cw[FFN_CONV - 1:FFN_CONV, :] + b
        for kk in range(1, FFN_CONV):
            y = y + _shift_rows(u, up, kk) * cw[FFN_CONV - 1 - kk:FFN_CONV - kk, :]
        return y

    d = functools.partial(jnp.dot, preferred_element_type=F32)

    def up(c):
        wg = wg_ref[c]
        wv = wv_ref[c]
        return d(h, wg), d(hp, wg) * keep, d(h, wv), d(hp, wv) * keep

    per_flush = 4
    nxt = up(0)
    pending = None
    for c in range(n_chunks):
        ug, ugp, uv, uvp = nxt
        if c + 1 < n_chunks:
            nxt = up(c + 1)
        gate = conv(ug, ugp, cg_ref[c], bg_ref[c])
        val = conv(uv, uvp, cv_ref[c], bv_ref[c])
        down = d(_bf(_silu(gate) * val), wd_ref[c])
        pending = down if pending is None else pending + down
        if (c + 1) % per_flush == 0 or c + 1 == n_chunks:
            out_ref[...] += pending
            pending = None


def _ffn(x, norm_w, w_up, conv_w, conv_b, w_down, seq, tm, fc):
    m, d = x.shape
    d_ff = w_down.shape[0]
    nc = d_ff // fc
    halo = BF16_ROWS

    def chunks(a):
        return a.reshape(a.shape[0], nc, fc).transpose(1, 0, 2)

    wg = chunks(_bf(w_up[:, :d_ff]))
    wv = chunks(_bf(w_up[:, d_ff:]))
    cg = chunks(conv_w[:, :d_ff].astype(F32))
    cv = chunks(conv_w[:, d_ff:].astype(F32))
    bg = chunks(conv_b[None, :d_ff].astype(F32))
    bv = chunks(conv_b[None, d_ff:].astype(F32))
    wd = _bf(w_down).reshape(nc, fc, d)
    kern = functools.partial(_ffn_kernel, tiles_per_seq=seq // tm, n_chunks=nc)
    return pl.pallas_call(
        kern,
        out_shape=jax.ShapeDtypeStruct((m, d), F32),
        grid=(m // tm,),
        in_specs=[
            pl.BlockSpec((tm, d), lambda i: (i, 0)),
            pl.BlockSpec((halo, d), lambda i: (jnp.maximum(i * (tm // halo) - 1, 0), 0)),
            _const_spec((1, d)),
            _const_spec((nc, d, fc)),
            _const_spec((nc, d, fc)),
            _const_spec((nc, FFN_CONV, fc)),
            _const_spec((nc, FFN_CONV, fc)),
            _const_spec((nc, 1, fc)),
            _const_spec((nc, 1, fc)),
            _const_spec((nc, fc, d)),
        ],
        out_specs=pl.BlockSpec((tm, d), lambda i: (i, 0)),
        compiler_params=_params("parallel"),
        name="ffn",
    )(x, x, norm_w.reshape(1, d), wg, wv, cg, cv, bg, bv, wd)


def _tiles(seq):
    tm = min(512, seq)
    t_len = min(256, seq)
    return tm, t_len


def kernel(x, mem, norm_mix, w_in, ret_norm, moba_q_norm, moba_k_norm, gdn_conv, gdn_a_log, gdn_dt_bias,
           gdn_norm, w_out, norm_cross, norm_mem, cross_wq, cross_wkv, cross_q_norm, cross_k_norm, cross_wo,
           norm_ffn, ffn_up, ffn_conv, ffn_conv_b, ffn_down, rel_bias):
    batch, seq, d = x.shape
    mem_len = mem.shape[1]
    depth = w_in.shape[0]
    tm, t_len = _tiles(seq)
    xf = x.reshape(batch * seq, d)
    memf = mem.reshape(batch * mem_len, d)
    for l in range(depth):
        ret_in, moba_in, gdn_in, scol = _inproj(xf, norm_mix[l], w_in[l], tm)
        y_ret = _retention(ret_in, ret_norm[l], batch, seq, t_len)
        y_moba = _moba(moba_in, moba_q_norm[l], moba_k_norm[l], rel_bias, batch, seq)
        y_gdn = _gdn(gdn_in, scol, gdn_conv[l], gdn_a_log[l], gdn_dt_bias[l], gdn_norm[l],
                     batch, seq, t_len)
        kn, v = _memkv(memf, norm_mem[l], cross_wkv[l], cross_k_norm[l], mem_len)
        xf = _cross(y_ret, y_moba, y_gdn, w_out[l], xf, norm_cross[l], cross_wq[l], cross_q_norm[l], kn, v,
                    cross_wo[l], batch, seq, mem_len, tm)
        xf = _ffn(xf, norm_ffn[l], ffn_up[l], ffn_conv[l], ffn_conv_b[l], ffn_down[l], seq, tm, 256)
    return xf.reshape(batch, seq, d)
```

```python
import functools
import math

import numpy as np
import jax
import jax.numpy as jnp
from jax import lax
from jax.experimental import pallas as pl
from jax.experimental.pallas import tpu as pltpu

F32 = jnp.float32
BF16 = jnp.bfloat16

HEAD_DIM = 64
RET_HEADS = 4
MOBA_HEADS = 8
GDN_HEADS = 4
RET_W = RET_HEADS * HEAD_DIM
MOBA_W = MOBA_HEADS * HEAD_DIM
GDN_W = GDN_HEADS * HEAD_DIM
ROPE_BASE = 10000.0
MOBA_BLOCK = 256
MOBA_TOPK = 3
REL_BUCKETS = 32
REL_MAX_DIST = 128
GDN_CHUNK = 64
GDN_CONV = 4
CROSS_HEADS = 4
CROSS_HEAD_DIM = 128
FFN_CONV = 3
EPS = 1e-6

LANES = 128
SUBLANES = 8
BF16_ROWS = 16
VMEM_LIMIT = 56 * 1024 * 1024


def _bf(a):
    return a.astype(BF16)


def _mm(a, b):
    return jnp.dot(_bf(a), _bf(b), preferred_element_type=F32)


def _mm_nt(a, b):
    return lax.dot_general(_bf(a), _bf(b), (((1,), (1,)), ((), ())), preferred_element_type=F32)


def _split(a):
    hi = a.astype(BF16)
    lo = (a - hi.astype(F32)).astype(BF16)
    return hi, lo


def _mm3(a, b):
    ah, al = _split(a)
    bh, bl = _split(b)
    d = functools.partial(jnp.dot, preferred_element_type=F32)
    return d(ah, bh) + (d(ah, bl) + d(al, bh))


def _mm3_nt(a, b):
    ah, al = _split(a)
    bh, bl = _split(b)
    d = functools.partial(lax.dot_general, dimension_numbers=(((1,), (1,)), ((), ())),
                          preferred_element_type=F32)
    return d(ah, bh) + (d(ah, bl) + d(al, bh))


def _mm2_l(a, b_exact):
    ah, al = _split(a)
    d = functools.partial(jnp.dot, preferred_element_type=F32)
    return d(ah, b_exact) + d(al, b_exact)


def _mm2_r(a_exact, b):
    bh, bl = _split(b)
    d = functools.partial(jnp.dot, preferred_element_type=F32)
    return d(a_exact, bh) + d(a_exact, bl)


def _sigmoid(x):
    return 1.0 / (1.0 + jnp.exp(-x))


def _silu(x):
    return x * _sigmoid(x)


def _softplus(x):
    return jnp.maximum(x, 0.0) + jnp.log1p(jnp.exp(-jnp.abs(x)))


def _rms_rows(x, w):
    return x * lax.rsqrt(jnp.mean(x * x, axis=-1, keepdims=True) + EPS) * w


def _shift_rows(u, prev, k):
    r = pltpu.roll(u, k, axis=0)
    rp = pltpu.roll(prev, k, axis=0)[:SUBLANES]
    row = lax.broadcasted_iota(jnp.int32, (SUBLANES, u.shape[1]), 0)
    top = jnp.where(row < k, rp, r[:SUBLANES])
    return jnp.concatenate([top, r[SUBLANES:]], axis=0)


def _const_spec(shape):
    nd = len(shape)
    return pl.BlockSpec(shape, lambda *_: (0,) * nd)


def _resident_spec(shape):
    nd = len(shape)
    return pl.BlockSpec(shape, lambda *_: (0,) * nd, pipeline_mode=pl.Buffered(1))


def _params(*sem):
    return pltpu.CompilerParams(dimension_semantics=sem, vmem_limit_bytes=VMEM_LIMIT)


def _inproj_kernel(x_ref, nw_ref, w_ref, ws_ref, ret_ref, moba_ref, gdn_ref, scol_ref):
    x = x_ref[...]
    hf = _rms_rows(x, nw_ref[...])
    h = _bf(hf)
    off = 0
    for ref in (ret_ref, moba_ref, gdn_ref):
        width = ref.shape[1]
        for c in range(0, width, 512):
            ref[:, c:c + 512] = jnp.dot(h, w_ref[:, off + c:off + c + 512],
                                        preferred_element_type=F32).astype(ref.dtype)
        off += width
    scol_ref[...] = _mm3(hf, ws_ref[...])


def _inproj(x, norm_w, w_in, tm):
    m, d = x.shape
    main = 4 * RET_W + 3 * MOBA_W + 4 * GDN_W
    w_main = _bf(w_in[:, :main])
    w_small = jnp.zeros((d, LANES), F32).at[:, :2 * GDN_HEADS].set(w_in[:, main:])
    out_shape = (
        jax.ShapeDtypeStruct((m, 4 * RET_W), BF16),
        jax.ShapeDtypeStruct((m, 3 * MOBA_W), BF16),
        jax.ShapeDtypeStruct((m, 4 * GDN_W), BF16),
        jax.ShapeDtypeStruct((m, LANES), F32),
    )
    return pl.pallas_call(
        _inproj_kernel,
        out_shape=out_shape,
        grid=(m // tm,),
        in_specs=[
            pl.BlockSpec((tm, d), lambda i: (i, 0)),
            _const_spec((1, d)),
            _resident_spec((d, main)),
            _resident_spec((d, LANES)),
        ],
        out_specs=(
            pl.BlockSpec((tm, 4 * RET_W), lambda i: (i, 0)),
            pl.BlockSpec((tm, 3 * MOBA_W), lambda i: (i, 0)),
            pl.BlockSpec((tm, 4 * GDN_W), lambda i: (i, 0)),
            pl.BlockSpec((tm, LANES), lambda i: (i, 0)),
        ),
        compiler_params=_params("parallel"),
        name="inproj",
    )(x, norm_w.reshape(1, d), w_main, w_small)


def _head_consts(n_heads):
    w = n_heads * HEAD_DIM
    head = np.arange(w) // HEAD_DIM
    same = (head[:, None] == head[None, :]).astype(np.float32)
    hmask = (head[None, None, :] == np.arange(n_heads)[:, None, None]).astype(np.float32)
    return same, hmask


def _ret_kernel(in_ref, cos_ref, sin_ref, gn_ref, rot_ref, eye_ref, dmat_ref, xi_ref, zeta_ref, gt_ref,
                hmask_ref, same_ref, out_ref, s_ref, *, n_heads):
    w = n_heads * HEAD_DIM
    t_len = in_ref.shape[0]

    @pl.when(pl.program_id(1) == 0)
    def _():
        s_ref[...] = jnp.zeros_like(s_ref)

    xin = in_ref[...]
    q = xin[:, :w]
    k = xin[:, w:2 * w]
    v = xin[:, 2 * w:3 * w]
    g = xin[:, 3 * w:].astype(F32)
    cos = cos_ref[...]
    sin = sin_ref[...]
    rot = rot_ref[...]
    qr = q.astype(F32) * cos + jnp.dot(q, rot, preferred_element_type=F32) * sin
    kr = k.astype(F32) * cos + jnp.dot(k, rot, preferred_element_type=F32) * sin
    same = same_ref[...]
    k_t = _mm_nt(eye_ref[...], kr)
    state = s_ref[...]
    o = _mm(qr, state) * xi_ref[...]
    vf = v.astype(F32)
    hms = [hmask_ref[h] for h in range(n_heads)]
    ss = [_mm_nt(qr * hm, kr) * dmat_ref[h] for h, hm in enumerate(hms)]
    for s, hm in zip(ss, hms):
        o = o + _mm(s, vf * hm)
    zr = jnp.concatenate(
        [jnp.broadcast_to(zeta_ref[h:h + 1, :], (HEAD_DIM, t_len)) for h in range(n_heads)], axis=0)
    s_ref[...] = state * gt_ref[...] + _mm(k_t * zr, v) * same
    ms = _mm2_l(o * o, _bf(same)) * (1.0 / HEAD_DIM)
    y = o * lax.rsqrt(ms + EPS) * gn_ref[...] * _silu(g)
    out_ref[...] = y.astype(out_ref.dtype)


def _retention(ret_in, ret_norm, batch, seq, t_len):
    n_heads = RET_HEADS
    w = n_heads * HEAD_DIM
    m = batch * seq
    nt = seq // t_len
    half = HEAD_DIM // 2
    pos = np.arange(seq, dtype=np.float32)
    inv_freq = (ROPE_BASE ** (-jnp.arange(half, dtype=F32) / half))
    ang = jnp.asarray(pos)[:, None] * inv_freq[None, :]
    cos = jnp.tile(jnp.concatenate([jnp.cos(ang)] * 2, axis=-1), (1, n_heads))
    sin = jnp.tile(jnp.concatenate([jnp.sin(ang)] * 2, axis=-1), (1, n_heads))
    j = np.arange(w)
    rot = np.zeros((w, w), np.float32)
    first = (j % HEAD_DIM) < half
    rot[j[first] + half, j[first]] = -1.0
    rot[j[~first] - half, j[~first]] = 1.0
    same, hmask = _head_consts(n_heads)
    log_gamma = jnp.log1p(-jnp.exp2(-5.0 - jnp.arange(n_heads, dtype=F32)))
    idx = jnp.arange(t_len, dtype=F32)
    diff = idx[:, None] - idx[None, :]
    scale = HEAD_DIM ** -0.5
    dmat = jnp.where(diff >= 0, jnp.exp(log_gamma[:, None, None] * jnp.maximum(diff, 0.0)), 0.0) * scale
    xi = jnp.repeat(jnp.exp(log_gamma[:, None] * (idx + 1.0)).T, HEAD_DIM, axis=1)
    zeta = jnp.exp(log_gamma[:, None] * (t_len - 1.0 - idx)) * scale
    zeta = jnp.zeros((SUBLANES, t_len), F32).at[:n_heads].set(zeta)
    g_chunk = jnp.repeat(jnp.exp(log_gamma * t_len), HEAD_DIM)[None, :]
    gn = jnp.tile(ret_norm.astype(F32), n_heads)[None, :]
    kern = functools.partial(_ret_kernel, n_heads=n_heads)
    return pl.pallas_call(
        kern,
        out_shape=jax.ShapeDtypeStruct((m, w), BF16),
        grid=(batch, nt),
        in_specs=[
            pl.BlockSpec((t_len, 4 * w), lambda b, t: (b * nt + t, 0)),
            pl.BlockSpec((t_len, w), lambda b, t: (t, 0)),
            pl.BlockSpec((t_len, w), lambda b, t: (t, 0)),
            _const_spec((1, w)),
            _const_spec((w, w)),
            _const_spec((w, w)),
            _const_spec((n_heads, t_len, t_len)),
            _const_spec((t_len, w)),
            _const_spec((SUBLANES, t_len)),
            _const_spec((1, w)),
            _const_spec((n_heads, 1, w)),
            _const_spec((w, w)),
        ],
        out_specs=pl.BlockSpec((t_len, w), lambda b, t: (b * nt + t, 0)),
        scratch_shapes=[pltpu.VMEM((w, w), F32)],
        compiler_params=_params("parallel", "arbitrary"),
        name="retention",
    )(ret_in, cos, sin, gn, jnp.asarray(rot, BF16), jnp.eye(w, dtype=BF16), dmat, xi, zeta, g_chunk,
      jnp.asarray(hmask), jnp.asarray(same))


def _t5_bucket(rel):
    n = np.maximum(rel, 0)
    exact = REL_BUCKETS // 2
    nf = np.maximum(n, exact).astype(np.float32)
    ratio = np.log(nf / np.float32(exact)) / np.float32(math.log(REL_MAX_DIST / exact))
    large = exact + (ratio * np.float32(REL_BUCKETS - exact)).astype(np.int32)
    return np.where(n < exact, n, np.minimum(large, REL_BUCKETS - 1)).astype(np.int32)


MOBA_MASK = -1e30
MOBA_EXP_LIMIT = 1e37


def _moba_kernel(q_ref, k_ref, v_ref, idown_ref, idprev_ref, rb_ref, qg_ref, kg_ref, same_ref,
                 eye_ref, out_ref, kaug_ref, vaug_ref, kmean_ref, bias_ref, *, n_blocks, far_bucket):
    blk = MOBA_BLOCK
    hd = HEAD_DIM
    qi = pl.program_id(2)
    head0 = 2 * pl.program_id(0)
    same = same_ref[...]
    inv_d = 1.0 / hd
    scale = hd ** -0.5
    lane = lax.broadcasted_iota(jnp.int32, (1, 2 * hd), 1)
    in_h0 = lane < hd
    both = range(2)
    mine = [in_h0, jnp.logical_not(in_h0)]
    blk_row = lax.broadcasted_iota(jnp.int32, (n_blocks, blk), 0)

    def augmented_queries(q_blk, j):
        qf = q_blk.astype(F32)
        qn = qf * lax.rsqrt(_mm2_l(qf * qf, same) * inv_d + EPS) * qg_ref[...]
        km = kmean_ref[...]
        gates = [jnp.where(blk_row < j, _mm3_nt(jnp.where(mine[e], km, 0.0), qn), -jnp.inf) for e in both]
        ranks = [jnp.zeros_like(g) for g in gates]
        for i in range(n_blocks):
            tie = jnp.where(blk_row > i, 1.0, 0.0)
            ranks = [r + jnp.where(g[i:i + 1, :] > g, 1.0, jnp.where(g[i:i + 1, :] == g, tie, 0.0))
                     for r, g in zip(ranks, gates)]
        mask_t = [jnp.where(((blk_row < j) & (r < MOBA_TOPK)) | (blk_row == j), 0.0, MOBA_MASK)
                  for r in ranks]
        pad = jnp.zeros((hd - n_blocks - SUBLANES, blk), F32)
        zq = jnp.zeros((hd, blk), F32)
        extras = []
        for e in both:
            far = jnp.full((1, blk), rb_ref[head0 + e, far_bucket], F32)
            far_hi = far.astype(BF16).astype(F32)
            crow = jnp.concatenate([far_hi, far - far_hi, jnp.zeros((SUBLANES - 2, blk), F32)], axis=0)
            extras.append([zq, mask_t[e], crow, pad] if e == 0 else [mask_t[e], crow, pad, zq])
        extras = [_mm_nt(eye_ref[...], jnp.concatenate(parts, axis=0)) for parts in extras]
        return [jnp.where(mine[e], qn * scale, 0.0) + extras[e] for e in both]

    @pl.when((qi == 0) & (pl.program_id(1) == 0))
    def _():
        for e in both:
            for tab, ids_ref in enumerate((idown_ref, idprev_ref)):
                ids = ids_ref[...]
                tile = jnp.zeros((blk, blk), F32)
                for b in range(REL_BUCKETS):
                    tile = jnp.where(ids == b, rb_ref[head0 + e, b], tile)
                bias_ref[tab, e] = tile - rb_ref[head0 + e, far_bucket]

    @pl.when(qi == 0)
    def _():
        def body(j, c):
            r0 = pl.multiple_of(j * blk, blk)
            kb = k_ref[pl.ds(r0, blk), :].astype(F32)
            kn = kb * lax.rsqrt(_mm2_l(kb * kb, same) * inv_d + EPS) * kg_ref[...]
            kmean_ref[pl.ds(j, 1), :] = jnp.mean(kn, axis=0, keepdims=True)
            vb = v_ref[pl.ds(r0, blk), :].astype(F32)
            for e in both:
                xo = (1 - e) * hd
                ones = (lane >= xo + n_blocks) & (lane < xo + n_blocks + 3)
                ext = jnp.where((lane == xo + j) | ones, 1.0, 0.0)
                kaug_ref[e, pl.ds(r0, blk), :] = jnp.where(mine[e], kn, ext).astype(BF16)
                vaug_ref[e, pl.ds(r0, blk), :] = jnp.where(mine[e], vb, 1.0).astype(BF16)
            return c
        lax.fori_loop(0, n_blocks, body, 0)

    row = lax.broadcasted_iota(jnp.int32, (blk, blk), 0)
    col = lax.broadcasted_iota(jnp.int32, (blk, blk), 1)
    q0 = pl.multiple_of(qi * blk, blk)
    qp = pl.multiple_of(jnp.maximum(qi - 1, 0) * blk, blk)
    qbase = augmented_queries(q_ref[...], qi)
    bown = [bias_ref[0, e] for e in both]
    bprev = [bias_ref[1, e] for e in both]

    def scores(qa, e, r0, bias=None):
        s = _mm_nt(qa, kaug_ref[e, pl.ds(r0, blk), :])
        return s if bias is None else s + bias

    def finish(accs):
        o = [acc / pltpu.roll(acc, hd, axis=1) for acc in accs]
        return jnp.where(in_h0, o[0], o[1])

    k_own = [kaug_ref[e, pl.ds(q0, blk), :] for e in both]
    shifts = [jnp.sum(jnp.where(mine[e], qbase[e] * k_own[e].astype(F32), 0.0), axis=-1, keepdims=True)
              .astype(BF16).astype(F32) for e in both]
    q_shift = [_bf(qbase[e] + jnp.where(lane == (1 - e) * hd + n_blocks + 2, -shifts[e], 0.0)) for e in both]

    def fast_tiles(accs, starts):
        ss = [[scores(q_shift[e], e, r0) for e in both] for r0 in starts]
        for s, r0 in zip(ss, starts):
            accs = tuple(accs[e] + _mm(jnp.exp(s[e]), vaug_ref[e, pl.ds(r0, blk), :]) for e in both)
        return accs

    def fast_tile(accs, r0):
        return fast_tiles(accs, [r0])

    no_prev = jnp.where(qi >= 1, 0.0, MOBA_MASK)
    s_own = [jnp.where(col <= row, scores(q_shift[e], e, q0, bown[e]), -jnp.inf) for e in both]
    s_prev = [scores(q_shift[e], e, qp, bprev[e]) + no_prev for e in both]
    accs = tuple(_mm(jnp.exp(s_own[e]), vaug_ref[e, pl.ds(q0, blk), :])
                 + _mm(jnp.exp(s_prev[e]), vaug_ref[e, pl.ds(qp, blk), :]) for e in both)
    n_far = jnp.maximum(qi - 1, 0)

    def starts(first, count):
        return [pl.multiple_of((first + u) * blk, blk) for u in range(count)]

    group = 4
    accs = lax.fori_loop(0, n_far // group, lambda t, a: fast_tiles(a, starts(group * t, group)), accs)
    done = (n_far // group) * group
    left = n_far - done
    accs = lax.cond(left >= 2, lambda a: fast_tiles(a, starts(done, 2)), lambda a: a, accs)
    last = jnp.maximum(n_far - 1, 0)
    accs = lax.cond(left % 2 == 1, lambda a: fast_tile(a, starts(last, 1)[0]), lambda a: a, accs)

    unsafe = sum(jnp.sum(jnp.where(jnp.abs(acc) < MOBA_EXP_LIMIT, 0.0, 1.0)) for acc in accs)
    overflowed = unsafe > 0.0

    def safe_path(_):
        def update(carry, s, vaug):
            m_i, acc = carry
            m_n = jnp.maximum(m_i, jnp.max(s, axis=-1, keepdims=True))
            return m_n, jnp.exp(m_i - m_n) * acc + _mm(jnp.exp(s - m_n), vaug)

        def block(j, c):
            r0 = pl.multiple_of(j * blk, blk)
            out = []
            for e in range(2):
                s = scores(_bf(qbase[e]), e, r0) + jnp.where(j == qi - 1, bprev[e], 0.0)
                out.append(update(c[e], s, vaug_ref[e, pl.ds(r0, blk), :]))
            return tuple(out)

        init = []
        for e in range(2):
            s = jnp.where(col <= row, scores(_bf(qbase[e]), e, q0, bown[e]), -jnp.inf)
            m0 = jnp.max(s, axis=-1, keepdims=True)
            init.append((m0, _mm(jnp.exp(s - m0), vaug_ref[e, pl.ds(q0, blk), :])))
        c = lax.fori_loop(0, qi, block, tuple(init))
        return finish([c[0][1], c[1][1]])

    out = lax.cond(overflowed, safe_path, lambda _: finish(accs), 0)
    out_ref[...] = out.astype(out_ref.dtype)


def _moba_bucket_tables(seq):
    blk = MOBA_BLOCK
    d = np.arange(blk, dtype=np.int32)[:, None] - np.arange(blk, dtype=np.int32)[None, :]
    far = _t5_bucket(np.arange(blk + 1, max(seq, blk + 2), dtype=np.int32))
    assert (far == far[0]).all()
    return _t5_bucket(d), _t5_bucket(d + blk), int(far[0])


def _moba(moba_in, q_norm, k_norm, rel_bias, batch, seq):
    m = batch * seq
    blk = MOBA_BLOCK
    nb = seq // blk
    assert nb % SUBLANES == 0 and nb + SUBLANES <= HEAD_DIM
    pairs = MOBA_HEADS // 2
    pw = 2 * HEAD_DIM
    ids_own, ids_prev, far_bucket = _moba_bucket_tables(seq)
    same, _ = _head_consts(2)
    qg = jnp.tile(q_norm.astype(F32), 2)[None, :]
    kg = jnp.tile(k_norm.astype(F32), 2)[None, :]
    kern = functools.partial(_moba_kernel, n_blocks=nb, far_bucket=far_bucket)
    return pl.pallas_call(
        kern,
        out_shape=jax.ShapeDtypeStruct((m, MOBA_W), BF16),
        grid=(pairs, batch, nb),
        in_specs=[
            pl.BlockSpec((blk, pw), lambda p, b, i: (b * nb + i, p)),
            pl.BlockSpec((seq, pw), lambda p, b, i: (b, pairs + p)),
            pl.BlockSpec((seq, pw), lambda p, b, i: (b, 2 * pairs + p)),
            _const_spec((blk, blk)),
            _const_spec((blk, blk)),
            pl.BlockSpec(memory_space=pltpu.SMEM),
            _const_spec((1, pw)),
            _const_spec((1, pw)),
            _const_spec((pw, pw)),
            _const_spec((blk, blk)),
        ],
        out_specs=pl.BlockSpec((blk, pw), lambda p, b, i: (b * nb + i, p)),
        scratch_shapes=[pltpu.VMEM((2, seq, pw), BF16), pltpu.VMEM((2, seq, pw), BF16),
                        pltpu.VMEM((nb, pw), F32), pltpu.VMEM((2, 2, blk, blk), F32)],
        compiler_params=_params("arbitrary", "arbitrary", "arbitrary"),
        name="moba",
    )(moba_in, moba_in, moba_in, jnp.asarray(ids_own), jnp.asarray(ids_prev), rel_bias.astype(F32),
      qg, kg, jnp.asarray(same, BF16), jnp.eye(blk, dtype=BF16))


def _tri_inverse(ns, eye):
    xs = [eye - n for n in ns]
    ps = list(ns)
    steps = int(math.log2(GDN_CHUNK)) - 1
    for _ in range(steps):
        ps = [_mm(p, p) for p in ps]
        xs = [x + _mm(x, p) for x, p in zip(xs, ps)]
    rs = [eye - (x + _mm3(n, x)) for n, x in zip(ns, xs)]
    return [x + _mm(x, r) for x, r in zip(xs, rs)]


def _gdn_kernel(in_ref, scol_ref, *rest, n_heads):
    consts, (out_ref, s_ref, tail_ref) = rest[:-3], rest[-3:]

    @pl.when(pl.program_id(1) == 0)
    def _():
        s_ref[...] = jnp.zeros_like(s_ref)
        tail_ref[...] = jnp.zeros_like(tail_ref)

    for i in range(in_ref.shape[0]):
        _gdn_block(in_ref.at[i], scol_ref.at[i], *consts,
                   out_ref.at[i], s_ref.at[i], tail_ref.at[i], n_heads=n_heads)


def _gdn_block(in_ref, scol_ref, cw_ref, alog_b_ref, dtb_b_ref, alog_r_ref, dtb_r_ref, gn_ref,
               lblk_ref, ublk_ref, mstrict_ref, mincl_ref, hmask_ref, same_ref, eye_ref, sela_ref, selb_ref,
               selt_ref, out_ref, s_ref, tail_ref, *, n_heads):
    w = n_heads * HEAD_DIM
    t_len = in_ref.shape[0]
    c_len = GDN_CHUNK
    neg_inf = -jnp.inf

    xin = in_ref[...].astype(F32)
    raw = xin[:, :3 * w]
    z = xin[:, 3 * w:]
    prev = tail_ref[...]
    cw = cw_ref[...]
    acc = raw * cw[GDN_CONV - 1:GDN_CONV, :]
    for kk in range(1, GDN_CONV):
        acc = acc + _shift_rows(raw, prev, kk) * cw[GDN_CONV - 1 - kk:GDN_CONV - kk, :]
    tail_ref[...] = raw[t_len - SUBLANES:, :]
    y = _silu(acc)
    q = y[:, :w]
    k = y[:, w:2 * w]
    v = y[:, 2 * w:]
    same = same_ref[...]
    same_b = _bf(same)
    qn = q * lax.rsqrt(_mm2_l(q * q, same_b) + EPS) * (HEAD_DIM ** -0.5)
    kn = k * lax.rsqrt(_mm2_l(k * k, same_b) + EPS)

    sc = scol_ref[...]
    beta_b = _sigmoid(_mm2_l(sc, selb_ref[...]))
    ld_b = -jnp.exp(alog_b_ref[...]) * _softplus(_mm2_l(sc, sela_ref[...]) + dtb_b_ref[...])
    g_b = _mm2_r(lblk_ref[...], ld_b)
    sc_hi = sc.astype(BF16)
    sc_r = sc - sc_hi.astype(F32)
    sc_mid = sc_r.astype(BF16)
    sc_lo = (sc_r - sc_mid.astype(F32)).astype(BF16)
    sr = sum(_mm_nt(selt_ref[...], part) for part in (sc_hi, sc_mid, sc_lo))
    ld_r = -jnp.exp(alog_r_ref[...]) * _softplus(sr + dtb_r_ref[...])
    g_r = _mm2_l(ld_r, ublk_ref[...])
    rem_r = _mm2_l(ld_r, _bf(mstrict_ref[...]))

    eye = eye_ref[...]
    k_t = _mm_nt(eye, kn)
    rem_rows = jnp.concatenate(
        [jnp.broadcast_to(rem_r[n_heads + h:n_heads + h + 1, :], (HEAD_DIM, t_len)) for h in range(n_heads)],
        axis=0)
    kdec_t = k_t * jnp.exp(rem_rows)
    eg = jnp.exp(g_b)
    qdec = qn * eg
    wrhs = kn * beta_b * eg
    urhs = v * beta_b
    mstrict = mstrict_ref[...]
    mincl = mincl_ref[...]
    eye_f = eye.astype(F32)
    u = jnp.zeros((t_len, w), F32)
    wv = jnp.zeros((t_len, w), F32)
    heads = range(n_heads)
    hms = [hmask_ref[h] for h in heads]
    decs = [jnp.exp(jnp.where(mincl > 0.0, g_b[:, h * HEAD_DIM:h * HEAD_DIM + 1]
                              - g_r[n_heads + h:n_heads + h + 1, :], neg_inf)) for h in heads]
    kks = [_mm_nt(kn * hms[h], kn) for h in heads]
    ns = [jnp.where(mstrict > 0.0, beta_b[:, h * HEAD_DIM:h * HEAD_DIM + 1] * kks[h] * decs[h], 0.0)
          for h in heads]
    t_invs = _tri_inverse(ns, eye_f)
    for h in heads:
        for part in _split(t_invs[h]):
            u = u + _mm(part, urhs * hms[h])
            wv = wv + _mm(part, wrhs * hms[h])
    qks = [_mm_nt(qn * hms[h], kn) * decs[h] for h in heads]

    state = s_ref[...]
    vnews = []
    ointer = []
    for c in range(t_len // c_len):
        rc = slice(c * c_len, (c + 1) * c_len)
        from_state = _mm(jnp.concatenate([wv[rc], qdec[rc]], axis=0), state)
        vnew = u[rc] - from_state[:c_len]
        vnews.append(vnew)
        ointer.append(from_state[c_len:])
        gl = eg[(c + 1) * c_len - 1:(c + 1) * c_len, :]
        state = state * gl + _mm(kdec_t[:, rc], vnew) * same
    s_ref[...] = state
    vn = jnp.concatenate(vnews, axis=0)
    o = jnp.concatenate(ointer, axis=0)
    for h in range(n_heads):
        o = o + _mm(qks[h], vn * hmask_ref[h])
    ms = _mm2_l(o * o, same_b) * (1.0 / HEAD_DIM)
    yo = o * lax.rsqrt(ms + EPS) * gn_ref[...] * _silu(z)
    out_ref[...] = yo.astype(out_ref.dtype)


def _gdn(gdn_in, scol, conv_w, a_log, dt_bias, gdn_norm, batch, seq, t_len):
    n_heads = GDN_HEADS
    w = n_heads * HEAD_DIM
    m = batch * seq
    nt = seq // t_len
    chunk = np.arange(t_len) // GDN_CHUNK
    pos = np.arange(t_len)
    samec = chunk[:, None] == chunk[None, :]
    mincl = (samec & (pos[None, :] <= pos[:, None])).astype(np.float32)
    mstrict = (samec & (pos[None, :] < pos[:, None])).astype(np.float32)
    same, hmask = _head_consts(n_heads)
    head = np.arange(w) // HEAD_DIM
    selb = np.zeros((LANES, w), np.float32)
    sela = np.zeros((LANES, w), np.float32)
    selb[head, np.arange(w)] = 1.0
    sela[n_heads + head, np.arange(w)] = 1.0
    alog_b = jnp.repeat(a_log.astype(F32), HEAD_DIM)[None, :]
    dtb_b = jnp.repeat(dt_bias.astype(F32), HEAD_DIM)[None, :]
    alog_r = jnp.zeros((BF16_ROWS, 1), F32).at[n_heads:2 * n_heads, 0].set(a_log.astype(F32))
    dtb_r = jnp.zeros((BF16_ROWS, 1), F32).at[n_heads:2 * n_heads, 0].set(dt_bias.astype(F32))
    gn = jnp.tile(gdn_norm.astype(F32), n_heads)[None, :]
    kern = functools.partial(_gdn_kernel, n_heads=n_heads)
    g = 1
    selt = np.zeros((BF16_ROWS, LANES), np.float32)
    selt[np.arange(2 * n_heads), np.arange(2 * n_heads)] = 1.0
    out = pl.pallas_call(
        kern,
        out_shape=jax.ShapeDtypeStruct((batch, seq, w), BF16),
        grid=(batch // g, nt),
        in_specs=[
            pl.BlockSpec((g, t_len, 4 * w), lambda b, t: (b, t, 0)),
            pl.BlockSpec((g, t_len, LANES), lambda b, t: (b, t, 0)),
            _const_spec((GDN_CONV, 3 * w)),
            _const_spec((1, w)),
            _const_spec((1, w)),
            _const_spec((BF16_ROWS, 1)),
            _const_spec((BF16_ROWS, 1)),
            _const_spec((1, w)),
            _const_spec((t_len, t_len)),
            _const_spec((t_len, t_len)),
            _const_spec((t_len, t_len)),
            _const_spec((t_len, t_len)),
            _const_spec((n_heads, 1, w)),
            _const_spec((w, w)),
            _const_spec((w, w)),
            _const_spec((LANES, w)),
            _const_spec((LANES, w)),
            _const_spec((BF16_ROWS, LANES)),
        ],
        out_specs=pl.BlockSpec((g, t_len, w), lambda b, t: (b, t, 0)),
        scratch_shapes=[pltpu.VMEM((g, w, w), F32), pltpu.VMEM((g, SUBLANES, 3 * w), F32)],
        compiler_params=_params("parallel", "arbitrary"),
        name="gdn",
    )(gdn_in.reshape(batch, seq, 4 * w), scol.reshape(batch, seq, LANES), conv_w.astype(F32),
      alog_b, dtb_b, alog_r, dtb_r, gn,
      jnp.asarray(mincl, BF16), jnp.asarray(mincl.T, BF16), jnp.asarray(mstrict), jnp.asarray(mincl),
      jnp.asarray(hmask), jnp.asarray(same), jnp.eye(w, dtype=BF16), jnp.asarray(sela, BF16),
      jnp.asarray(selb, BF16), jnp.asarray(selt, BF16))
    return out.reshape(m, w)


def _memkv_kernel(mem_ref, nw_ref, wkv_ref, kg_ref, k_ref, v_ref):
    cw = CROSS_HEADS * CROSS_HEAD_DIM
    h = _bf(_rms_rows(mem_ref[...], nw_ref[...]))
    kv = jnp.dot(h, wkv_ref[...], preferred_element_type=F32)
    for hd in range(CROSS_HEADS):
        sl = slice(hd * CROSS_HEAD_DIM, (hd + 1) * CROSS_HEAD_DIM)
        k_ref[:, sl] = _rms_rows(kv[:, sl], kg_ref[...]).astype(k_ref.dtype)
    v_ref[...] = kv[:, cw:].astype(v_ref.dtype)


def _memkv(mem, norm_mem, wkv, k_norm, tm):
    m, d = mem.shape
    cw = CROSS_HEADS * CROSS_HEAD_DIM
    return pl.pallas_call(
        _memkv_kernel,
        out_shape=(jax.ShapeDtypeStruct((m, cw), BF16), jax.ShapeDtypeStruct((m, cw), BF16)),
        grid=(m // tm,),
        in_specs=[
            pl.BlockSpec((tm, d), lambda i: (i, 0)),
            _const_spec((1, d)),
            _const_spec((d, 2 * cw)),
            _const_spec((1, CROSS_HEAD_DIM)),
        ],
        out_specs=(pl.BlockSpec((tm, cw), lambda i: (i, 0)), pl.BlockSpec((tm, cw), lambda i: (i, 0))),
        compiler_params=_params("parallel"),
        name="memkv",
    )(mem, norm_mem.reshape(1, d), _bf(wkv), k_norm.reshape(1, CROSS_HEAD_DIM).astype(F32))


def _cross_kernel(yr_ref, ym_ref, yg_ref, wr_ref, wm_ref, wg_ref, x_ref, nw_ref, wq_ref, qg_ref, k_ref, v_ref,
                  wo_ref, out_ref):
    d = functools.partial(jnp.dot, preferred_element_type=F32)
    x = x_ref[...] + (d(yr_ref[...], wr_ref[...]) + d(ym_ref[...], wm_ref[...]) + d(yg_ref[...], wg_ref[...]))
    h = _bf(_rms_rows(x, nw_ref[...]))
    q = jnp.dot(h, wq_ref[...], preferred_element_type=F32)
    scale = CROSS_HEAD_DIM ** -0.5
    sls = [slice(hd * CROSS_HEAD_DIM, (hd + 1) * CROSS_HEAD_DIM) for hd in range(CROSS_HEADS)]
    qhs = [_rms_rows(q[:, sl], qg_ref[...]) for sl in sls]
    ss = [_mm_nt(qh, k_ref[:, sl]) * scale for qh, sl in zip(qhs, sls)]
    ps = [jnp.exp(s - jnp.max(s, axis=-1, keepdims=True)) for s in ss]
    ls = [jnp.sum(p, axis=-1, keepdims=True) for p in ps]
    outs = [_mm(p, v_ref[:, sl]) / l for p, sl, l in zip(ps, sls, ls)]
    o = _bf(jnp.concatenate(outs, axis=-1))
    out_ref[...] = x + jnp.dot(o, wo_ref[...], preferred_element_type=F32)


def _cross(y_ret, y_moba, y_gdn, w_out, x, norm_w, wq, q_norm, kn, v, wo, batch, seq, mem_len, tm):
    m, d = x.shape
    cw = CROSS_HEADS * CROSS_HEAD_DIM
    nt = seq // tm
    wb = _bf(w_out)
    wr = wb[:RET_W]
    wm = wb[RET_W:RET_W + MOBA_W]
    wg = wb[RET_W + MOBA_W:]
    return pl.pallas_call(
        _cross_kernel,
        out_shape=jax.ShapeDtypeStruct((m, d), F32),
        grid=(batch, nt),
        in_specs=[
            pl.BlockSpec((tm, RET_W), lambda b, t: (b * nt + t, 0)),
            pl.BlockSpec((tm, MOBA_W), lambda b, t: (b * nt + t, 0)),
            pl.BlockSpec((tm, GDN_W), lambda b, t: (b * nt + t, 0)),
            _const_spec((RET_W, d)),
            _const_spec((MOBA_W, d)),
            _const_spec((GDN_W, d)),
            pl.BlockSpec((tm, d), lambda b, t: (b * nt + t, 0)),
            _const_spec((1, d)),
            _const_spec((d, cw)),
            _const_spec((1, CROSS_HEAD_DIM)),
            pl.BlockSpec((mem_len, cw), lambda b, t: (b, 0)),
            pl.BlockSpec((mem_len, cw), lambda b, t: (b, 0)),
            _const_spec((cw, d)),
        ],
        out_specs=pl.BlockSpec((tm, d), lambda b, t: (b * nt + t, 0)),
        compiler_params=_params("parallel", "parallel"),
        name="cross",
    )(y_ret, y_moba, y_gdn, wr, wm, wg, x, norm_w.reshape(1, d), _bf(wq),
      q_norm.reshape(1, CROSS_HEAD_DIM).astype(F32), kn, v, _bf(wo))


def _ffn_kernel(x_ref, xp_ref, nw_ref, wg_ref, wv_ref, cg_ref, cv_ref, bg_ref, bv_ref, wd_ref, out_ref,
                *, tiles_per_seq, n_chunks):
    x = x_ref[...]
    nw = nw_ref[...]
    h = _bf(_rms_rows(x, nw))
    keep = jnp.where(pl.program_id(0) % tiles_per_seq == 0, 0.0, 1.0)
    hp = _bf(_rms_rows(xp_ref[...], nw))
    out_ref[...] = x

    def conv(u, up, cw, b):
        y = u * cw[FFN_CONV - 1:FFN_CONV, :] + b
        for kk in range(1, FFN_CONV):
            y = y + _shift_rows(u, up, kk) * cw[FFN_CONV - 1 - kk:FFN_CONV - kk, :]
        return y

    d = functools.partial(jnp.dot, preferred_element_type=F32)

    def up(c):
        wg = wg_ref[c]
        wv = wv_ref[c]
        return d(h, wg), d(hp, wg) * keep, d(h, wv), d(hp, wv) * keep

    per_flush = 4
    nxt = up(0)
    pending = None
    for c in range(n_chunks):
        ug, ugp, uv, uvp = nxt
        if c + 1 < n_chunks:
            nxt = up(c + 1)
        gate = conv(ug, ugp, cg_ref[c], bg_ref[c])
        val = conv(uv, uvp, cv_ref[c], bv_ref[c])
        down = d(_bf(_silu(gate) * val), wd_ref[c])
        pending = down if pending is None else pending + down
        if (c + 1) % per_flush == 0 or c + 1 == n_chunks:
            out_ref[...] += pending
            pending = None


def _ffn(x, norm_w, w_up, conv_w, conv_b, w_down, seq, tm, fc):
    m, d = x.shape
    d_ff = w_down.shape[0]
    nc = d_ff // fc
    halo = BF16_ROWS

    def chunks(a):
        return a.reshape(a.shape[0], nc, fc).transpose(1, 0, 2)

    wg = chunks(_bf(w_up[:, :d_ff]))
    wv = chunks(_bf(w_up[:, d_ff:]))
    cg = chunks(conv_w[:, :d_ff].astype(F32))
    cv = chunks(conv_w[:, d_ff:].astype(F32))
    bg = chunks(conv_b[None, :d_ff].astype(F32))
    bv = chunks(conv_b[None, d_ff:].astype(F32))
    wd = _bf(w_down).reshape(nc, fc, d)
    kern = functools.partial(_ffn_kernel, tiles_per_seq=seq // tm, n_chunks=nc)
    return pl.pallas_call(
        kern,
        out_shape=jax.ShapeDtypeStruct((m, d), F32),
        grid=(m // tm,),
        in_specs=[
            pl.BlockSpec((tm, d), lambda i: (i, 0)),
            pl.BlockSpec((halo, d), lambda i: (jnp.maximum(i * (tm // halo) - 1, 0), 0)),
            _const_spec((1, d)),
            _const_spec((nc, d, fc)),
            _const_spec((nc, d, fc)),
            _const_spec((nc, FFN_CONV, fc)),
            _const_spec((nc, FFN_CONV, fc)),
            _const_spec((nc, 1, fc)),
            _const_spec((nc, 1, fc)),
            _const_spec((nc, fc, d)),
        ],
        out_specs=pl.BlockSpec((tm, d), lambda i: (i, 0)),
        compiler_params=_params("parallel"),
        name="ffn",
    )(x, x, norm_w.reshape(1, d), wg, wv, cg, cv, bg, bv, wd)


def _tiles(seq):
    tm = min(512, seq)
    t_len = min(256, seq)
    return tm, t_len


def kernel(x, mem, norm_mix, w_in, ret_norm, moba_q_norm, moba_k_norm, gdn_conv, gdn_a_log, gdn_dt_bias,
           gdn_norm, w_out, norm_cross, norm_mem, cross_wq, cross_wkv, cross_q_norm, cross_k_norm, cross_wo,
           norm_ffn, ffn_up, ffn_conv, ffn_conv_b, ffn_down, rel_bias):
    batch, seq, d = x.shape
    mem_len = mem.shape[1]
    depth = w_in.shape[0]
    tm, t_len = _tiles(seq)
    tm_in = 2 * tm if (batch * seq) % (2 * tm) == 0 else tm
    xf = x.reshape(batch * seq, d)
    memf = mem.reshape(batch * mem_len, d)
    for l in range(depth):
        ret_in, moba_in, gdn_in, scol = _inproj(xf, norm_mix[l], w_in[l], tm_in)
        y_ret = _retention(ret_in, ret_norm[l], batch, seq, t_len)
        y_moba = _moba(moba_in, moba_q_norm[l], moba_k_norm[l], rel_bias, batch, seq)
        y_gdn = _gdn(gdn_in, scol, gdn_conv[l], gdn_a_log[l], gdn_dt_bias[l], gdn_norm[l],
                     batch, seq, t_len)
        kn, v = _memkv(memf, norm_mem[l], cross_wkv[l], cross_k_norm[l], mem_len)
        xf = _cross(y_ret, y_moba, y_gdn, w_out[l], xf, norm_cross[l], cross_wq[l], cross_q_norm[l], kn, v,
                    cross_wo[l], batch, seq, mem_len, tm)
        xf = _ffn(xf, norm_ffn[l], ffn_up[l], ffn_conv[l], ffn_conv_b[l], ffn_down[l], seq, tm, 256)
    return xf.reshape(batch, seq, d)
```
